```python
import jax, jax.numpy as jnp
from jax import lax
import numpy as np

D_MODEL = 4096
BATCH = 1
SEQ = 8192
DEPTH = 2

HEAD_DIM = 128
N_HEADS = D_MODEL // HEAD_DIM
N_HEADS_A = 3 * N_HEADS // 8
N_HEADS_B = 3 * N_HEADS // 8
N_HEADS_C = N_HEADS - N_HEADS_A - N_HEADS_B
N_KV_A = 2
IDX_HEADS = 16
IDX_DIM = 64
IDX_TOPK_MAX = 256
CHUNK = 64
N_LEFT_CHUNKS = 8
BAND = (N_LEFT_CHUNKS + 1) * CHUNK
REL_CLIP = 256
Q_BLOCK = 128
ROPE_THETA = 10000.0
D_FF = ((8 * D_MODEL + 3 * 256 - 1) // (3 * 256)) * 256
EPS = 1e-6
PROJ_SIZES = (
    N_HEADS_A * HEAD_DIM,
    N_KV_A * HEAD_DIM,
    N_KV_A * HEAD_DIM,
    IDX_HEADS * IDX_DIM,
    IDX_DIM,
    IDX_HEADS,
    N_HEADS_B * HEAD_DIM,
    N_HEADS_B * HEAD_DIM,
    N_HEADS_B * HEAD_DIM,
    N_HEADS_C * HEAD_DIM,
    N_HEADS_C * HEAD_DIM,
    N_HEADS_C * HEAD_DIM,
)
PROJ_WIDTH = sum(PROJ_SIZES)

kernel_name = "hybrid_dsa_stickbreak_chunkband_adaln"


def rms_norm(x, g):
    xf = x.astype(jnp.float32)
    y = xf * lax.rsqrt(jnp.mean(xf * xf, axis=-1, keepdims=True) + EPS)
    return (y * g.astype(jnp.float32)).astype(x.dtype)


def rope(x, pos_f):
    d = x.shape[-1]
    inv = ROPE_THETA ** (-jnp.arange(0, d, 2, dtype=jnp.float32) / d)
    ang = pos_f[:, None] * inv[None, :]
    cos = jnp.cos(ang)[None, :, None, :].astype(x.dtype)
    sin = jnp.sin(ang)[None, :, None, :].astype(x.dtype)
    x1, x2 = jnp.split(x, 2, axis=-1)
    return jnp.concatenate([x1 * cos - x2 * sin, x2 * cos + x1 * sin], axis=-1)


def to_blocks(a):
    b, s = a.shape[:2]
    return jnp.moveaxis(a.reshape(b, s // Q_BLOCK, Q_BLOCK, *a.shape[2:]), 1, 0)


def from_blocks(a):
    nb, b, q = a.shape[:3]
    return jnp.moveaxis(a, 0, 1).reshape(b, nb * q, *a.shape[3:])


def dsa_mixer(q, k, v, iq, ik, iw, pos):
    b, s = q.shape[:2]
    topk = min(IDX_TOPK_MAX, s // 4)
    chunk_id = pos // CHUNK
    rep = N_HEADS_A // N_KV_A
    idx_scale = (IDX_HEADS ** -0.5) * (IDX_DIM ** -0.5)
    attn_scale = HEAD_DIM ** -0.5
    gather = jax.vmap(lambda a, i: a[i])

    def block(args):
        qb, iqb, iwb, start = args
        q_chunk = (start + jnp.arange(Q_BLOCK, dtype=jnp.int32)) // CHUNK
        admissible = chunk_id[None, :] <= q_chunk[:, None]
        dots = jnp.einsum('bqhe,bse->bqhs', iqb, ik).astype(jnp.float32)
        score = jnp.einsum('bqh,bqhs->bqs', iwb.astype(jnp.float32), jax.nn.relu(dots)) * idx_scale
        score = jnp.where(admissible[None], score, -jnp.inf)
        _, sel = lax.top_k(score, topk)
        valid = chunk_id[sel] <= q_chunk[None, :, None]
        ks = gather(k, sel)
        vs = gather(v, sel)
        qg = qb.reshape(b, Q_BLOCK, N_KV_A, rep, HEAD_DIM)
        sc = jnp.einsum('bqgrd,bqkgd->bqgrk', qg, ks).astype(jnp.float32) * attn_scale
        sc = jnp.where(valid[:, :, None, None, :], sc, -jnp.inf)
        p = jax.nn.softmax(sc, axis=-1).astype(vs.dtype)
        o = jnp.einsum('bqgrk,bqkgd->bqgrd', p, vs)
        return o.reshape(b, Q_BLOCK, N_HEADS_A, HEAD_DIM)

    starts = jnp.arange(s // Q_BLOCK, dtype=jnp.int32) * Q_BLOCK
    out = lax.map(block, (to_blocks(q), to_blocks(iq), to_blocks(iw), starts))
    return from_blocks(out)


def stick_breaking_mixer(q, k, v, pos):
    s = q.shape[1]
    scale = HEAD_DIM ** -0.5

    def block(args):
        qb, start = args
        tq = start + jnp.arange(Q_BLOCK, dtype=jnp.int32)
        causal = pos[None, :] < tq[:, None]
        z = jnp.einsum('bqhd,bshd->bhqs', qb, k).astype(jnp.float32) * scale
        log_keep = jnp.where(causal, jax.nn.log_sigmoid(-z), 0.0)
        between = lax.cumsum(log_keep, axis=3, reverse=True) - log_keep
        log_a = jax.nn.log_sigmoid(z) + between
        a = jnp.where(causal, jnp.exp(log_a), 0.0).astype(v.dtype)
        return jnp.einsum('bhqs,bshd->bqhd', a, v)

    starts = jnp.arange(s // Q_BLOCK, dtype=jnp.int32) * Q_BLOCK
    return from_blocks(lax.map(block, (to_blocks(q), starts)))


def chunk_band_mixer(q, k, v, rel_bias):
    b, s, h, d = q.shape
    n = s // CHUNK
    qc = q.reshape(b, n, CHUNK, h, d)

    def band(a):
        a = jnp.pad(a.reshape(b, n, CHUNK, h, d), ((0, 0), (N_LEFT_CHUNKS, 0), (0, 0), (0, 0), (0, 0)))
        return jnp.concatenate([a[:, o:o + n] for o in range(N_LEFT_CHUNKS + 1)], axis=2)

    kb, vb = band(k), band(v)
    slot = np.arange(BAND) // CHUNK
    kj = np.arange(BAND) % CHUNK
    qi = np.arange(CHUNK)
    dist = (N_LEFT_CHUNKS - slot)[None, :] * CHUNK + qi[:, None] - kj[None, :]
    rel_idx = np.clip(dist, -REL_CLIP, REL_CLIP) + REL_CLIP
    bias = rel_bias[:, rel_idx].astype(jnp.float32)
    valid = (np.arange(n)[:, None] - N_LEFT_CHUNKS + slot[None, :]) >= 0
    sc = jnp.einsum('bnqhd,bnkhd->bnhqk', qc, kb).astype(jnp.float32) * (d ** -0.5) + bias[None, None]
    sc = jnp.where(valid[None, :, None, None, :], sc, -jnp.inf)
    p = jax.nn.softmax(sc, axis=-1).astype(vb.dtype)
    o = jnp.einsum('bnhqk,bnkhd->bnqhd', p, vb)
    return o.reshape(b, s, h, d)


def setup_inputs(seed: int = 0) -> dict:
    key = jax.random.key(seed)
    ks = jax.random.split(key, 13)
    f32 = jnp.float32
    nrm = lambda k, shape, sc: jax.random.normal(k, shape, f32) * sc
    return {
        "x": nrm(ks[0], (BATCH, SEQ, D_MODEL), 1.0),
        "c": nrm(ks[1], (BATCH, D_MODEL), 1.0),
        "w_ada": nrm(ks[2], (DEPTH, D_MODEL, 6 * D_MODEL), 0.5 * D_MODEL ** -0.5),
        "b_ada": nrm(ks[3], (DEPTH, 6 * D_MODEL), 0.02),
        "norm_attn_g": 1.0 + nrm(ks[4], (DEPTH, D_MODEL), 0.02),
        "w_in": nrm(ks[5], (DEPTH, D_MODEL, PROJ_WIDTH), D_MODEL ** -0.5),
        "rel_bias": nrm(ks[6], (DEPTH, N_HEADS_C, 2 * REL_CLIP + 1), 0.1),
        "head_norm_g": 1.0 + nrm(ks[7], (DEPTH, N_HEADS * HEAD_DIM), 0.02),
        "w_out": nrm(ks[8], (DEPTH, N_HEADS * HEAD_DIM, D_MODEL), (N_HEADS * HEAD_DIM) ** -0.5),
        "norm_ffn_g": 1.0 + nrm(ks[9], (DEPTH, D_MODEL), 0.02),
        "w_gate_up": nrm(ks[10], (DEPTH, D_MODEL, 2 * D_FF), D_MODEL ** -0.5),
        "w_down": nrm(ks[11], (DEPTH, D_FF, D_MODEL), D_FF ** -0.5),
        "final_norm_g": 1.0 + nrm(ks[12], (D_MODEL,), 0.02),
    }


def reference(x, c, w_ada, b_ada, norm_attn_g, w_in, rel_bias, head_norm_g, w_out,
              norm_ffn_g, w_gate_up, w_down, final_norm_g):
    b, s, _ = x.shape
    pos = jnp.arange(s, dtype=jnp.int32)
    pos_f = pos.astype(jnp.float32)
    cond = jax.nn.silu(c)
    offsets = np.cumsum(np.array(PROJ_SIZES))[:-1].tolist()
    for l in range(DEPTH):
        mod = (cond @ w_ada[l] + b_ada[l])[:, None, :]
        sh1, sc1, g1, sh2, sc2, g2 = jnp.split(mod, 6, axis=-1)

        h = rms_norm(x, norm_attn_g[l]) * (1.0 + sc1) + sh1
        proj = h @ w_in[l]
        qa, ka, va, iq, ik, iw, qb, kb, vb, qc, kc, vc = jnp.split(proj, offsets, axis=-1)
        heads = lambda t, n, d: t.reshape(b, s, n, d)
        qa = rope(heads(qa, N_HEADS_A, HEAD_DIM), pos_f)
        ka = rope(heads(ka, N_KV_A, HEAD_DIM), pos_f)
        va = heads(va, N_KV_A, HEAD_DIM)
        iq = rope(heads(iq, IDX_HEADS, IDX_DIM), pos_f)
        ik = rope(heads(ik, 1, IDX_DIM), pos_f)[:, :, 0, :]
        o_a = dsa_mixer(qa, ka, va, iq, ik, iw, pos)
        o_b = stick_breaking_mixer(heads(qb, N_HEADS_B, HEAD_DIM), heads(kb, N_HEADS_B, HEAD_DIM),
                                   heads(vb, N_HEADS_B, HEAD_DIM), pos)
        o_c = chunk_band_mixer(heads(qc, N_HEADS_C, HEAD_DIM), heads(kc, N_HEADS_C, HEAD_DIM),
                               heads(vc, N_HEADS_C, HEAD_DIM), rel_bias[l])
        mix = jnp.concatenate([o_a.astype(x.dtype), o_b.astype(x.dtype), o_c.astype(x.dtype)], axis=2)
        mix = rms_norm(mix, head_norm_g[l].reshape(N_HEADS, HEAD_DIM)).reshape(b, s, N_HEADS * HEAD_DIM)
        x = x + g1 * (mix @ w_out[l])

        h = rms_norm(x, norm_ffn_g[l]) * (1.0 + sc2) + sh2
        gate, up = jnp.split(h @ w_gate_up[l], 2, axis=-1)
        x = x + g2 * ((jax.nn.silu(gate) * up) @ w_down[l])
    return rms_norm(x, final_norm_g)
```

```python
import functools
import math

import numpy as np
import jax
import jax.numpy as jnp
from jax import lax
from jax.experimental import pallas as pl
from jax.experimental.pallas import tpu as pltpu

F32 = jnp.float32
BF16 = jnp.bfloat16
I32 = jnp.int32

HEAD_DIM = 128
N_KV_A = 2
IDX_HEADS = 16
IDX_DIM = 64
IDX_TOPK_MAX = 256
CHUNK = 64
N_LEFT_CHUNKS = 8
REL_CLIP = 256
ROPE_THETA = 10000.0
EPS = 1e-6

V7X_LANES = 128
V7X_SUBLANES = 8
V7X_VMEM_BYTES = 64 * 1024 * 1024

NEG_BIG = -1e30
INT_MIN = -(2 ** 31)

DSA_TQ = 128
DSA_TK = 512
SB_T = 256
CB_TQ = 128


def _cparams(semantics, vmem_bytes):
    limit = int(min(V7X_VMEM_BYTES * 7 // 8, max(vmem_bytes, 16 * 1024 * 1024)))
    return pltpu.CompilerParams(dimension_semantics=semantics, vmem_limit_bytes=limit)


def _largest_tile(n, unit, cap):
    best = None
    t = unit
    while t <= min(n, cap):
        if n % t == 0:
            best = t
        t += unit
    assert best is not None, (n, unit, cap)
    return best


def _adaln_kernel(c_ref, w_ref, b_ref, o_ref):
    c = c_ref[...]
    cond = c * jax.nn.sigmoid(c)
    tn = w_ref.shape[-1]
    cols = []
    for j in range(tn // V7X_LANES):
        wj = w_ref[:, j * V7X_LANES:(j + 1) * V7X_LANES]
        cols.append(jnp.sum(wj * cond, axis=0, keepdims=True))
    o_ref[...] = jnp.concatenate(cols, axis=1) + b_ref[...]


def _adaln(c, w_ada, b_ada):
    depth, d, n = w_ada.shape
    tn = _largest_tile(n, V7X_LANES, 512)
    c_b = jnp.broadcast_to(c.reshape(d, 1), (d, V7X_LANES))
    out = pl.pallas_call(
        _adaln_kernel,
        grid=(depth, n // tn),
        in_specs=[
            pl.BlockSpec((d, V7X_LANES), lambda l, j: (0, 0)),
            pl.BlockSpec((None, d, tn), lambda l, j: (l, 0, j)),
            pl.BlockSpec((None, 1, tn), lambda l, j: (l, 0, j)),
        ],
        out_specs=pl.BlockSpec((None, 1, tn), lambda l, j: (l, 0, j)),
        out_shape=jax.ShapeDtypeStruct((depth, 1, n), F32),
        compiler_params=_cparams(("parallel", "parallel"), 3 * d * tn * 4 + 2 * d * V7X_LANES * 4),
        name="adaln",
    )(c_b, w_ada, b_ada.reshape(depth, 1, n))
    return out


def _norm_kernel(x_ref, g_ref, *rest, modulate):
    o_ref = rest[-1]
    x = x_ref[...]
    ms = jnp.mean(x * x, axis=-1, keepdims=True)
    y = x * lax.rsqrt(ms + EPS) * g_ref[...]
    if modulate:
        sc_ref, sh_ref = rest[0], rest[1]
        y = y * (1.0 + sc_ref[...]) + sh_ref[...]
    o_ref[...] = y.astype(o_ref.dtype)


def _norm(x, g, scale=None, shift=None, out_dtype=BF16):
    s, d = x.shape
    ts = _largest_tile(s, V7X_SUBLANES, 256)
    modulate = scale is not None
    row = pl.BlockSpec((1, d), lambda i: (0, 0))
    args = [x, g.reshape(1, d)]
    specs = [pl.BlockSpec((ts, d), lambda i: (i, 0)), row]
    if modulate:
        args += [scale.reshape(1, d), shift.reshape(1, d)]
        specs += [row, row]
    return pl.pallas_call(
        functools.partial(_norm_kernel, modulate=modulate),
        grid=(s // ts,),
        in_specs=specs,
        out_specs=pl.BlockSpec((ts, d), lambda i: (i, 0)),
        out_shape=jax.ShapeDtypeStruct((s, d), out_dtype),
        compiler_params=_cparams(("parallel",), 6 * ts * d * 4),
        name="rmsnorm",
    )(*args)


def _dot(a, w):
    return jnp.dot(a, w.astype(BF16), preferred_element_type=F32)


def _mm_plain_kernel(a_ref, w_ref, o_ref):
    o_ref[...] = _dot(a_ref[...], w_ref[...]).astype(o_ref.dtype)


def _mm_swiglu_kernel(a_ref, wg_ref, wu_ref, o_ref):
    a = a_ref[...]
    gate = _dot(a, wg_ref[...])
    up = _dot(a, wu_ref[...])
    o_ref[...] = (gate * jax.nn.sigmoid(gate) * up).astype(o_ref.dtype)


def _mm_resid_kernel(a_ref, w_ref, r_ref, g_ref, o_ref, *acc, nk):
    if nk == 1:
        o_ref[...] = r_ref[...] + g_ref[...] * _dot(a_ref[...], w_ref[...])
        return
    acc_ref, = acc
    k = pl.program_id(2)

    @pl.when(k == 0)
    def _():
        acc_ref[...] = jnp.zeros_like(acc_ref)

    acc_ref[...] += _dot(a_ref[...], w_ref[...])

    @pl.when(k == nk - 1)
    def _():
        o_ref[...] = r_ref[...] + g_ref[...] * acc_ref[...]


def _mm_tiles(m, k, n, w_itemsize):
    tm = _largest_tile(m, V7X_SUBLANES, 1024)
    tn = _largest_tile(n, V7X_LANES, 768 if w_itemsize == 2 else 256)
    tk = k if k <= 4096 else _largest_tile(k, V7X_LANES, 5632)
    return tm, tn, tk


def _mm_vmem(tm, tn, tk, w_itemsize, n_w=1):
    return (2 * tm * tk * 2 + n_w * (2 * tk * tn * w_itemsize + tk * tn * 2)
            + (4 + 2 * n_w) * tm * tn * 4)


def _matmul(a, w, out_dtype):
    m, k = a.shape
    n = w.shape[1]
    tm, tn, tk = _mm_tiles(m, k, n, w.dtype.itemsize)
    assert tk == k
    return pl.pallas_call(
        _mm_plain_kernel,
        grid=(m // tm, n // tn),
        in_specs=[pl.BlockSpec((tm, k), lambda i, j: (i, 0)),
                  pl.BlockSpec((k, tn), lambda i, j: (0, j))],
        out_specs=pl.BlockSpec((tm, tn), lambda i, j: (i, j)),
        out_shape=jax.ShapeDtypeStruct((m, n), out_dtype),
        compiler_params=_cparams(("parallel", "parallel"), _mm_vmem(tm, tn, tk, w.dtype.itemsize)),
        name="matmul",
    )(a, w)


def _matmul_swiglu(a, w_gate_up):
    m, k = a.shape
    n = w_gate_up.shape[1] // 2
    tm, tn, tk = _mm_tiles(m, k, n, w_gate_up.dtype.itemsize)
    assert tk == k
    nb = n // tn
    return pl.pallas_call(
        _mm_swiglu_kernel,
        grid=(m // tm, nb),
        in_specs=[pl.BlockSpec((tm, k), lambda i, j: (i, 0)),
                  pl.BlockSpec((k, tn), lambda i, j: (0, j)),
                  pl.BlockSpec((k, tn), lambda i, j: (0, j + nb))],
        out_specs=pl.BlockSpec((tm, tn), lambda i, j: (i, j)),
        out_shape=jax.ShapeDtypeStruct((m, n), BF16),
        compiler_params=_cparams(("parallel", "parallel"),
                                 _mm_vmem(tm, tn, tk, w_gate_up.dtype.itemsize, n_w=2)),
        name="matmul_swiglu",
    )(a, w_gate_up, w_gate_up)


def _matmul_resid(a, w, resid, gate):
    m, k = a.shape
    n = w.shape[1]
    tm, tn, tk = _mm_tiles(m, k, n, w.dtype.itemsize)
    nk = k // tk
    scratch = [pltpu.VMEM((tm, tn), F32)] if nk > 1 else []
    return pl.pallas_call(
        functools.partial(_mm_resid_kernel, nk=nk),
        grid=(m // tm, n // tn, nk),
        in_specs=[pl.BlockSpec((tm, tk), lambda i, j, kk: (i, kk)),
                  pl.BlockSpec((tk, tn), lambda i, j, kk: (kk, j)),
                  pl.BlockSpec((tm, tn), lambda i, j, kk: (i, j)),
                  pl.BlockSpec((1, tn), lambda i, j, kk: (0, j))],
        out_specs=pl.BlockSpec((tm, tn), lambda i, j, kk: (i, j)),
        out_shape=jax.ShapeDtypeStruct((m, n), F32),
        scratch_shapes=scratch,
        compiler_params=_cparams(("parallel", "parallel", "arbitrary"),
                                 _mm_vmem(tm, tn, tk, w.dtype.itemsize) + 3 * tm * tn * 4),
        name="matmul_resid",
    )(a, w, resid, gate.reshape(1, n))


def _rope_tables(s, d, lane_groups):
    pos = jnp.arange(s, dtype=jnp.int32).astype(F32)
    inv = ROPE_THETA ** (-jnp.arange(0, d, 2, dtype=F32) / d)
    ang = pos[:, None] * inv[None, :]
    cos, sin = jnp.cos(ang), jnp.sin(ang)
    cos_t = jnp.tile(jnp.concatenate([cos, cos], axis=1), (1, lane_groups))
    sin_t = jnp.tile(jnp.concatenate([-sin, sin], axis=1), (1, lane_groups))
    return cos_t, sin_t


def _rot_half(x, d):
    if d == V7X_LANES:
        return pltpu.roll(x, d // 2, 1)
    lane = lax.broadcasted_iota(I32, x.shape, 1)
    first = (lane & (d - 1)) < d // 2
    return jnp.where(first, pltpu.roll(x, V7X_LANES - d // 2, 1), pltpu.roll(x, d // 2, 1))


def _prep_kernel(x_ref, *rest, rope_dim, scale):
    o_ref = rest[-1]
    x = x_ref[...]
    if rope_dim:
        cos_ref, sin_ref = rest[0], rest[1]
        x = x * cos_ref[...] + _rot_half(x, rope_dim) * sin_ref[...]
    if scale != 1.0:
        x = x * scale
    o_ref[...] = x.astype(o_ref.dtype)


def _prep_heads(proj, col0, n_blocks, tables=None, rope_dim=0, scale=1.0, head_major=True):
    s = proj.shape[0]
    ts = _largest_tile(s, V7X_SUBLANES, 1024)
    cb0 = col0 // V7X_LANES
    args = [proj]
    specs = [pl.BlockSpec((ts, V7X_LANES), lambda h, i: (i, cb0 + h))]
    if rope_dim:
        args += list(tables)
        specs += [pl.BlockSpec((ts, V7X_LANES), lambda h, i: (i, 0))] * 2
    if head_major:
        out_spec = pl.BlockSpec((None, ts, V7X_LANES), lambda h, i: (h, i, 0))
        out_shape = jax.ShapeDtypeStruct((n_blocks, s, V7X_LANES), BF16)
    else:
        out_spec = pl.BlockSpec((ts, V7X_LANES), lambda h, i: (i, h))
        out_shape = jax.ShapeDtypeStruct((s, n_blocks * V7X_LANES), BF16)
    return pl.pallas_call(
        functools.partial(_prep_kernel, rope_dim=rope_dim, scale=scale),
        grid=(n_blocks, s // ts),
        in_specs=specs,
        out_specs=out_spec,
        out_shape=out_shape,
        compiler_params=_cparams(("parallel", "parallel"), 16 * ts * V7X_LANES * 4),
        name="prep_heads",
    )(*args)


def _ikw_kernel(x_ref, cos_ref, sin_ref, ik_ref, iw_ref):
    x = x_ref[...]
    y = x * cos_ref[...] + _rot_half(x, IDX_DIM) * sin_ref[...]
    lane = lax.broadcasted_iota(I32, x.shape, 1)
    ik_ref[...] = jnp.where(lane < IDX_DIM, y, pltpu.roll(y, IDX_DIM, 1)).astype(ik_ref.dtype)
    iw_ref[...] = pltpu.roll(x, V7X_LANES - IDX_DIM, 1)


def _prep_ikw(proj, col0, tables):
    s = proj.shape[0]
    ts = _largest_tile(s, V7X_SUBLANES, 1024)
    cb = col0 // V7X_LANES
    blk = pl.BlockSpec((ts, V7X_LANES), lambda i: (i, 0))
    return pl.pallas_call(
        _ikw_kernel,
        grid=(s // ts,),
        in_specs=[pl.BlockSpec((ts, V7X_LANES), lambda i: (i, cb)), blk, blk],
        out_specs=[blk, blk],
        out_shape=[jax.ShapeDtypeStruct((s, V7X_LANES), BF16),
                   jax.ShapeDtypeStruct((s, V7X_LANES), F32)],
        compiler_params=_cparams(("parallel",), 16 * ts * V7X_LANES * 4),
        name="prep_ikw",
    )(proj, *tables)


def _dsa_kernel(iq_ref, iw_ref, ik_ref, q_ref, k_ref, v_ref, o_ref,
                lhs_scr, key_scr, m_scr, l_scr, acc_scr, *, topk, rep, idx_scale):
    tq, tk = DSA_TQ, DSA_TK
    i = pl.program_id(0)
    start = i * tq
    n_kt = (start + tq + tk - 1) >> int(math.log2(tk))

    lane = lax.broadcasted_iota(I32, (tq, V7X_LANES), 1)
    for h in range(IDX_HEADS):
        pair = iq_ref[:, (h // 2) * V7X_LANES:(h // 2 + 1) * V7X_LANES]
        own = (lane < IDX_DIM) if h % 2 == 0 else (lane >= IDX_DIM)
        lhs_scr[h] = jnp.where(own, pair, jnp.zeros_like(pair))

    iw = iw_ref[...]
    row = lax.broadcasted_iota(I32, (tq, tk), 0)
    col = lax.broadcasted_iota(I32, (tq, tk), 1)
    row_lim = start + ((row >> int(math.log2(CHUNK))) + 1) * CHUNK

    def score_body(kt, carry):
        off = pl.multiple_of(kt * tk, tk)
        ik_t = ik_ref[pl.ds(off, tk), :]
        acc = jnp.zeros((tq, tk), F32)
        for h in range(IDX_HEADS):
            d = lax.dot_general(lhs_scr[h], ik_t, (((1,), (1,)), ((), ())),
                                preferred_element_type=F32)
            acc = acc + iw[:, h:h + 1] * jnp.maximum(d, 0.0)
        bits = pltpu.bitcast(acc * idx_scale, I32)
        key = jnp.where(bits < 0, bits ^ 0x7FFFFFFF, bits)
        key_scr[kt] = jnp.where(col + off < row_lim, key, INT_MIN)
        return carry

    lax.fori_loop(0, n_kt, score_body, 0)

    def count_ge(cand):
        cb = jnp.broadcast_to(cand, (tq, V7X_LANES))

        def body(kt, part):
            keys = key_scr[kt]
            for c in range(tk // V7X_LANES):
                hit = keys[:, c * V7X_LANES:(c + 1) * V7X_LANES] >= cb
                part = part + jnp.where(hit, 1, 0)
            return part

        part = lax.fori_loop(0, n_kt, body, jnp.zeros((tq, V7X_LANES), I32))
        return jnp.sum(part, axis=1, keepdims=True)

    zero = jnp.zeros((tq, 1), I32)
    thr = jnp.where(count_ge(zero) >= topk, zero, zero + INT_MIN)

    def bisect(b, thr):
        cand = thr + lax.shift_left(jnp.int32(1), 30 - b)
        return jnp.where(count_ge(cand) >= topk, cand, thr)

    thr = lax.fori_loop(0, 31, bisect, thr)
    thr = jnp.maximum(thr, INT_MIN + 1)

    for g in range(N_KV_A):
        qg = q_ref[g * rep:(g + 1) * rep].reshape(rep * tq, HEAD_DIM)
        m_scr[...] = jnp.full(m_scr.shape, NEG_BIG, F32)
        l_scr[...] = jnp.zeros(l_scr.shape, F32)
        acc_scr[...] = jnp.zeros(acc_scr.shape, F32)

        def att_body(kt, carry):
            off = pl.multiple_of(kt * tk, tk)
            k_t = k_ref[g, pl.ds(off, tk), :]
            v_t = v_ref[g, pl.ds(off, tk), :]
            s = lax.dot_general(qg, k_t, (((1,), (1,)), ((), ())), preferred_element_type=F32)
            sel = key_scr[kt] >= thr
            bias = jnp.where(sel, 0.0, NEG_BIG)
            s = s + jnp.concatenate([bias] * rep, axis=0)
            m_old = m_scr[...]
            m_new = jnp.maximum(m_old, jnp.max(s, axis=-1, keepdims=True))
            p = jnp.exp(s - m_new)
            alpha = jnp.exp(m_old - m_new)
            l_scr[...] = alpha * l_scr[...] + jnp.sum(p, axis=-1, keepdims=True)
            acc_scr[...] = alpha * acc_scr[...] + jnp.dot(p.astype(BF16), v_t,
                                                          preferred_element_type=F32)
            m_scr[...] = m_new
            return carry

        lax.fori_loop(0, n_kt, att_body, 0)
        out = acc_scr[...] / l_scr[...]
        for r in range(rep):
            hh = g * rep + r
            o_ref[:, hh * HEAD_DIM:(hh + 1) * HEAD_DIM] = out[r * tq:(r + 1) * tq]


def _dsa(iq, iw, ik2, qa, ka, va, topk):
    n_heads, s, _ = qa.shape
    rep = n_heads // N_KV_A
    tq, tk = DSA_TQ, DSA_TK
    assert s % tk == 0
    idx_scale = (IDX_HEADS ** -0.5) * (IDX_DIM ** -0.5)
    vmem = (4 * N_KV_A * s * HEAD_DIM * 2 * 2 + 2 * s * V7X_LANES * 2 + tq * s * 4
            + 24 * rep * tq * tk * 4)
    return pl.pallas_call(
        functools.partial(_dsa_kernel, topk=topk, rep=rep, idx_scale=idx_scale),
        grid=(s // tq,),
        in_specs=[
            pl.BlockSpec((tq, IDX_HEADS * IDX_DIM), lambda i: (i, 0)),
            pl.BlockSpec((tq, V7X_LANES), lambda i: (i, 0)),
            pl.BlockSpec((s, V7X_LANES), lambda i: (0, 0)),
            pl.BlockSpec((n_heads, tq, HEAD_DIM), lambda i: (0, i, 0)),
            pl.BlockSpec((N_KV_A, s, HEAD_DIM), lambda i: (0, 0, 0)),
            pl.BlockSpec((N_KV_A, s, HEAD_DIM), lambda i: (0, 0, 0)),
        ],
        out_specs=pl.BlockSpec((tq, n_heads * HEAD_DIM), lambda i: (i, 0)),
        out_shape=jax.ShapeDtypeStruct((s, n_heads * HEAD_DIM), F32),
        scratch_shapes=[
            pltpu.VMEM((IDX_HEADS, tq, V7X_LANES), BF16),
            pltpu.VMEM((s // tk, tq, tk), I32),
            pltpu.VMEM((rep * tq, 1), F32),
            pltpu.VMEM((rep * tq, 1), F32),
            pltpu.VMEM((rep * tq, HEAD_DIM), F32),
        ],
        compiler_params=_cparams(("parallel",), vmem),
        name="dsa",
    )(iq, iw, ik2, qa, ka, va)


def _sb_kernel(q_ref, k_ref, vt_ref, o_ref):
    t = SB_T
    nrg = t // V7X_SUBLANES
    i = pl.program_id(1)
    q = q_ref[...]
    o_ref[...] = jnp.zeros(o_ref.shape, F32)

    def tile(j, carry, diagonal):
        z = lax.dot_general(k_ref[j], q, (((1,), (1,)), ((), ())), preferred_element_type=F32)
        e = jnp.exp(-jnp.abs(z))
        r = 1.0 / (1.0 + e)
        er = e * r
        pos = z >= 0.0
        beta = jnp.where(pos, r, er)
        keep = jnp.where(pos, er, r)
        if diagonal:
            rr = lax.broadcasted_iota(I32, (t, t), 0)
            tt = lax.broadcasted_iota(I32, (t, t), 1)
            key_pos = (rr & (V7X_SUBLANES - 1)) * nrg + (rr >> 3)
            causal = key_pos < tt
            beta = jnp.where(causal, beta, 0.0)
            keep = jnp.where(causal, keep, 1.0)
        run8 = jnp.ones((V7X_SUBLANES, t), F32)
        a_loc = [None] * nrg
        for g in range(nrg - 1, -1, -1):
            rows = slice(g * V7X_SUBLANES, (g + 1) * V7X_SUBLANES)
            a_loc[g] = beta[rows] * run8
            run8 = run8 * keep[rows]
        run = carry
        offs = [None] * V7X_SUBLANES
        for u in range(V7X_SUBLANES - 1, -1, -1):
            offs[u] = run
            run = run * run8[u:u + 1]
        off8 = jnp.concatenate(offs, axis=0)
        a = jnp.concatenate([al * off8 for al in a_loc], axis=0)
        o_ref[...] += jnp.dot(vt_ref[j], a.astype(BF16), preferred_element_type=F32)
        return run

    carry = tile(i, jnp.ones((1, t), F32), True)

    def cond(state):
        j, c = state
        return jnp.logical_and(j >= 0, jnp.max(c) > 0.0)

    def body(state):
        j, c = state
        return j - 1, tile(j, c, False)

    lax.while_loop(cond, body, (i - 1, carry))


def _sb(q, k_perm, v_t):
    n_heads, s, _ = q.shape
    t = SB_T
    nb = s // t
    return pl.pallas_call(
        _sb_kernel,
        grid=(n_heads, nb),
        in_specs=[
            pl.BlockSpec((None, t, HEAD_DIM), lambda h, i: (h, i, 0)),
            pl.BlockSpec((None, nb, t, HEAD_DIM), lambda h, i: (h, 0, 0, 0)),
            pl.BlockSpec((None, nb, HEAD_DIM, t), lambda h, i: (h, 0, 0, 0)),
        ],
        out_specs=pl.BlockSpec((None, HEAD_DIM, t), lambda h, i: (h, 0, i)),
        out_shape=jax.ShapeDtypeStruct((n_heads, HEAD_DIM, s), F32),
        compiler_params=_cparams(("parallel", "parallel"),
                                 8 * s * HEAD_DIM * 2 + 40 * t * t * 4),
        name="stick_breaking",
    )(q, k_perm, v_t)


def _cb_kernel(q_ref, k_ref, v_ref, b_ref, o_ref, *, win, pad):
    tq = CB_TQ
    i = pl.program_id(1)
    off = pl.multiple_of(i * tq, tq)
    k_w = k_ref[pl.ds(off, win), :]
    v_w = v_ref[pl.ds(off, win), :]
    s = lax.dot_general(q_ref[...], k_w, (((1,), (1,)), ((), ())), preferred_element_type=F32)
    s = s + b_ref[...]
    col = lax.broadcasted_iota(I32, (tq, win), 1)
    s = jnp.where(col + off >= pad, s, NEG_BIG)
    m = jnp.max(s, axis=-1, keepdims=True)
    p = jnp.exp(s - m)
    l = jnp.sum(p, axis=-1, keepdims=True)
    o_ref[...] = jnp.dot(p.astype(BF16), v_w, preferred_element_type=F32) / l


def _cb(q, k_pad, v_pad, bias, pad):
    n_heads, s, _ = q.shape
    tq = CB_TQ
    win = pad + tq
    sp = k_pad.shape[1]
    return pl.pallas_call(
        functools.partial(_cb_kernel, win=win, pad=pad),
        grid=(n_heads, s // tq),
        in_specs=[
            pl.BlockSpec((None, tq, HEAD_DIM), lambda h, i: (h, i, 0)),
            pl.BlockSpec((None, sp, HEAD_DIM), lambda h, i: (h, 0, 0)),
            pl.BlockSpec((None, sp, HEAD_DIM), lambda h, i: (h, 0, 0)),
            pl.BlockSpec((None, tq, win), lambda h, i: (h, 0, 0)),
        ],
        out_specs=pl.BlockSpec((tq, HEAD_DIM), lambda h, i: (i, h)),
        out_shape=jax.ShapeDtypeStruct((s, n_heads * HEAD_DIM), F32),
        compiler_params=_cparams(("parallel", "parallel"),
                                 8 * sp * HEAD_DIM * 2 + 16 * tq * win * 4),
        name="chunk_band",
    )(q, k_pad, v_pad, bias)


def _cb_bias(rel_bias):
    tq = CB_TQ
    pad = N_LEFT_CHUNKS * CHUNK
    win = pad + tq
    qi = np.arange(tq)[:, None]
    kj = np.arange(win)[None, :]
    dist = qi - kj + pad
    rel_idx = np.clip(dist, -REL_CLIP, REL_CLIP) + REL_CLIP
    q_chunk = qi // CHUNK
    k_chunk = kj // CHUNK - N_LEFT_CHUNKS
    in_band = (k_chunk <= q_chunk) & (k_chunk >= q_chunk - N_LEFT_CHUNKS)
    bias = rel_bias[:, rel_idx].astype(F32)
    return jnp.where(jnp.asarray(in_band)[None], bias, NEG_BIG)


def _headnorm_kernel(a_ref, b_ref, c_ref, g_ref, o_ref):
    col = 0
    for ref in (a_ref, b_ref, c_ref):
        for h in range(ref.shape[1] // HEAD_DIM):
            x = ref[:, h * HEAD_DIM:(h + 1) * HEAD_DIM]
            ms = jnp.mean(x * x, axis=-1, keepdims=True)
            y = x * lax.rsqrt(ms + EPS) * g_ref[:, col:col + HEAD_DIM]
            o_ref[:, col:col + HEAD_DIM] = y.astype(o_ref.dtype)
            col += HEAD_DIM


def _headnorm(o_a, o_b, o_c, g):
    s = o_a.shape[0]
    d = o_a.shape[1] + o_b.shape[1] + o_c.shape[1]
    ts = _largest_tile(s, V7X_SUBLANES, 256)
    return pl.pallas_call(
        _headnorm_kernel,
        grid=(s // ts,),
        in_specs=[pl.BlockSpec((ts, o_a.shape[1]), lambda i: (i, 0)),
                  pl.BlockSpec((ts, o_b.shape[1]), lambda i: (i, 0)),
                  pl.BlockSpec((ts, o_c.shape[1]), lambda i: (i, 0)),
                  pl.BlockSpec((1, d), lambda i: (0, 0))],
        out_specs=pl.BlockSpec((ts, d), lambda i: (i, 0)),
        out_shape=jax.ShapeDtypeStruct((s, d), BF16),
        compiler_params=_cparams(("parallel",), 6 * ts * d * 4),
        name="headnorm",
    )(o_a, o_b, o_c, g.reshape(1, d))


def _pack_w_in(w, sizes):
    pieces, offsets, src, dst = [], [], 0, 0
    for width in sizes:
        fill = (-dst) % V7X_LANES if width >= V7X_LANES else 0
        if fill:
            pieces.append(jnp.zeros((w.shape[0], fill), BF16))
            dst += fill
        offsets.append(dst)
        pieces.append(w[:, src:src + width].astype(BF16))
        src += width
        dst += width
    fill = (-dst) % V7X_LANES
    if fill:
        pieces.append(jnp.zeros((w.shape[0], fill), BF16))
    return jnp.concatenate(pieces, axis=1), offsets


def kernel(x, c, w_ada, b_ada, norm_attn_g, w_in, rel_bias, head_norm_g, w_out, norm_ffn_g,
           w_gate_up, w_down, final_norm_g):
    b, s, d = x.shape
    assert b == 1, "single-sequence prefill only"
    depth = w_ada.shape[0]
    n_heads = d // HEAD_DIM
    ha = 3 * n_heads // 8
    hb = 3 * n_heads // 8
    hc = n_heads - ha - hb
    assert ha % N_KV_A == 0 and s % SB_T == 0
    sizes = (ha * HEAD_DIM, N_KV_A * HEAD_DIM, N_KV_A * HEAD_DIM, IDX_HEADS * IDX_DIM, IDX_DIM,
             IDX_HEADS, hb * HEAD_DIM, hb * HEAD_DIM, hb * HEAD_DIM,
             hc * HEAD_DIM, hc * HEAD_DIM, hc * HEAD_DIM)
    assert w_in.shape[2] == sum(sizes)
    topk = min(IDX_TOPK_MAX, s // 4)
    scale = HEAD_DIM ** -0.5
    pad = N_LEFT_CHUNKS * CHUNK

    rope128 = _rope_tables(s, HEAD_DIM, 1)
    rope64 = _rope_tables(s, IDX_DIM, V7X_LANES // IDX_DIM)

    xs = x.reshape(s, d)
    mod = _adaln(c, w_ada, b_ada)

    for l in range(depth):
        sh1, sc1, g1, sh2, sc2, g2 = [mod[l, :, n * d:(n + 1) * d] for n in range(6)]

        h = _norm(xs, norm_attn_g[l], sc1, sh1)
        w_packed, offs = _pack_w_in(w_in[l], sizes)
        o_qa, o_ka, o_va, o_iq, o_ik, o_iw, o_qb, o_kb, o_vb, o_qc, o_kc, o_vc = offs
        assert o_iw == o_ik + IDX_DIM
        proj = _matmul(h, w_packed, F32)

        qa = _prep_heads(proj, o_qa, ha, rope128, HEAD_DIM, scale)
        ka = _prep_heads(proj, o_ka, N_KV_A, rope128, HEAD_DIM)
        va = _prep_heads(proj, o_va, N_KV_A)
        iq = _prep_heads(proj, o_iq, IDX_HEADS * IDX_DIM // V7X_LANES, rope64, IDX_DIM,
                         head_major=False)
        ik2, iw = _prep_ikw(proj, o_ik, rope64)
        o_a = _dsa(iq, iw, ik2, qa, ka, va, topk)

        qb = _prep_heads(proj, o_qb, hb, scale=scale)
        kb = _prep_heads(proj, o_kb, hb)
        vb = _prep_heads(proj, o_vb, hb)
        nb, sub = s // SB_T, V7X_SUBLANES
        kb = kb.reshape(hb, nb, sub, SB_T // sub, HEAD_DIM).swapaxes(2, 3).reshape(hb, nb, SB_T, HEAD_DIM)
        vbt = vb.reshape(hb, nb, sub, SB_T // sub, HEAD_DIM).swapaxes(2, 3).reshape(hb, nb, SB_T, HEAD_DIM)
        vbt = vbt.swapaxes(2, 3)
        o_bt = _sb(qb, kb, vbt)
        o_b = jnp.transpose(o_bt, (2, 0, 1)).reshape(s, hb * HEAD_DIM)

        qc = _prep_heads(proj, o_qc, hc, scale=scale)
        kc = _prep_heads(proj, o_kc, hc)
        vc = _prep_heads(proj, o_vc, hc)
        kc = jnp.pad(kc, ((0, 0), (pad, 0), (0, 0)))
        vc = jnp.pad(vc, ((0, 0), (pad, 0), (0, 0)))
        o_c = _cb(qc, kc, vc, _cb_bias(rel_bias[l]), pad)

        mix = _headnorm(o_a, o_b, o_c, head_norm_g[l])
        xs = _matmul_resid(mix, w_out[l], xs, g1)

        h = _norm(xs, norm_ffn_g[l], sc2, sh2)
        act = _matmul_swiglu(h, w_gate_up[l])
        xs = _matmul_resid(act, w_down[l], xs, g2)

    out = _norm(xs, final_norm_g, out_dtype=x.dtype)
    return out.reshape(b, s, d)
```

```python
import functools
import math

import numpy as np
import jax
import jax.numpy as jnp
from jax import lax
from jax.experimental import pallas as pl
from jax.experimental.pallas import tpu as pltpu

F32 = jnp.float32
BF16 = jnp.bfloat16
I32 = jnp.int32

HEAD_DIM = 128
N_KV_A = 2
IDX_HEADS = 16
IDX_DIM = 64
IDX_TOPK_MAX = 256
CHUNK = 64
N_LEFT_CHUNKS = 8
REL_CLIP = 256
ROPE_THETA = 10000.0
EPS = 1e-6

V7X_LANES = 128
V7X_SUBLANES = 8
V7X_VMEM_BYTES = 64 * 1024 * 1024

NEG_BIG = -1e30
INT_MIN = -(2 ** 31)

DSA_TQ = 128
DSA_TK = 512
SB_T = 256
SB_HEADS = 2
CB_TQ = 128
CB_HEADS = 2


def _cparams(semantics, vmem_bytes, flags=None):
    limit = int(min(V7X_VMEM_BYTES * 7 // 8, max(vmem_bytes, 16 * 1024 * 1024)))
    return pltpu.CompilerParams(dimension_semantics=semantics, vmem_limit_bytes=limit, flags=flags)


def _largest_tile(n, unit, cap):
    best = None
    t = unit
    while t <= min(n, cap):
        if n % t == 0:
            best = t
        t += unit
    assert best is not None, (n, unit, cap)
    return best


def _adaln_kernel(c_ref, w_ref, b_ref, o_ref):
    c = c_ref[...]
    cond = c * jax.nn.sigmoid(c)
    tn = w_ref.shape[-1]
    cols = []
    for j in range(tn // V7X_LANES):
        wj = w_ref[:, j * V7X_LANES:(j + 1) * V7X_LANES]
        cols.append(jnp.sum(wj * cond, axis=0, keepdims=True))
    o_ref[...] = jnp.concatenate(cols, axis=1) + b_ref[...]


def _adaln(c, w_ada, b_ada):
    depth, d, n = w_ada.shape
    tn = _largest_tile(n, V7X_LANES, 512)
    c_b = jnp.broadcast_to(c.reshape(d, 1), (d, V7X_LANES))
    out = pl.pallas_call(
        _adaln_kernel,
        grid=(depth, n // tn),
        in_specs=[
            pl.BlockSpec((d, V7X_LANES), lambda l, j: (0, 0)),
            pl.BlockSpec((None, d, tn), lambda l, j: (l, 0, j)),
            pl.BlockSpec((None, 1, tn), lambda l, j: (l, 0, j)),
        ],
        out_specs=pl.BlockSpec((None, 1, tn), lambda l, j: (l, 0, j)),
        out_shape=jax.ShapeDtypeStruct((depth, 1, n), F32),
        compiler_params=_cparams(("parallel", "parallel"), 3 * d * tn * 4 + 2 * d * V7X_LANES * 4),
        name="adaln",
    )(c_b, w_ada, b_ada.reshape(depth, 1, n))
    return out


def _norm_kernel(x_ref, g_ref, *rest, modulate):
    o_ref = rest[-1]
    x = x_ref[...]
    ms = jnp.mean(x * x, axis=-1, keepdims=True)
    y = x * lax.rsqrt(ms + EPS) * g_ref[...]
    if modulate:
        sc_ref, sh_ref = rest[0], rest[1]
        y = y * (1.0 + sc_ref[...]) + sh_ref[...]
    o_ref[...] = y.astype(o_ref.dtype)


def _norm(x, g, scale=None, shift=None, out_dtype=BF16):
    s, d = x.shape
    ts = _largest_tile(s, V7X_SUBLANES, 256)
    modulate = scale is not None
    row = pl.BlockSpec((1, d), lambda i: (0, 0))
    args = [x, g.reshape(1, d)]
    specs = [pl.BlockSpec((ts, d), lambda i: (i, 0)), row]
    if modulate:
        args += [scale.reshape(1, d), shift.reshape(1, d)]
        specs += [row, row]
    return pl.pallas_call(
        functools.partial(_norm_kernel, modulate=modulate),
        grid=(s // ts,),
        in_specs=specs,
        out_specs=pl.BlockSpec((ts, d), lambda i: (i, 0)),
        out_shape=jax.ShapeDtypeStruct((s, d), out_dtype),
        compiler_params=_cparams(("parallel",), 6 * ts * d * 4),
        name="rmsnorm",
    )(*args)


def _dot(a, w):
    return jnp.dot(a, w.astype(BF16), preferred_element_type=F32)


def _mm_plain_kernel(a_ref, w_ref, o_ref):
    o_ref[...] = _dot(a_ref[...], w_ref[...]).astype(o_ref.dtype)


def _mm_swiglu_kernel(a_ref, wg_ref, wu_ref, o_ref):
    a = a_ref[...]
    gate = _dot(a, wg_ref[...])
    up = _dot(a, wu_ref[...])
    o_ref[...] = (gate * jax.nn.sigmoid(gate) * up).astype(o_ref.dtype)


def _mm_resid_kernel(a_ref, w_ref, r_ref, g_ref, o_ref, *acc, nk):
    if nk == 1:
        o_ref[...] = r_ref[...] + g_ref[...] * _dot(a_ref[...], w_ref[...])
        return
    acc_ref, = acc
    k = pl.program_id(2)

    @pl.when(k == 0)
    def _():
        acc_ref[...] = jnp.zeros_like(acc_ref)

    acc_ref[...] += _dot(a_ref[...], w_ref[...])

    @pl.when(k == nk - 1)
    def _():
        o_ref[...] = r_ref[...] + g_ref[...] * acc_ref[...]


def _mm_tiles(m, k, n, w_itemsize):
    tn = _largest_tile(n, V7X_LANES, 768 if w_itemsize == 2 else 256)
    if k <= 4096:
        return _largest_tile(m, V7X_SUBLANES, 1024), tn, k
    if w_itemsize == 2:
        return _largest_tile(m, V7X_SUBLANES, 512), _largest_tile(n, V7X_LANES, 256), k
    return _largest_tile(m, V7X_SUBLANES, 1024), tn, _largest_tile(k, V7X_LANES, 5632)


def _mm_vmem(tm, tn, tk, w_itemsize, n_w=1):
    return (2 * tm * tk * 2 + n_w * (2 * tk * tn * w_itemsize + tk * tn * 2)
            + (4 + 2 * n_w) * tm * tn * 4)


def _matmul(a, w, out_dtype):
    m, k = a.shape
    n = w.shape[1]
    tm, tn, tk = _mm_tiles(m, k, n, w.dtype.itemsize)
    assert tk == k
    return pl.pallas_call(
        _mm_plain_kernel,
        grid=(m // tm, n // tn),
        in_specs=[pl.BlockSpec((tm, k), lambda i, j: (i, 0)),
                  pl.BlockSpec((k, tn), lambda i, j: (0, j))],
        out_specs=pl.BlockSpec((tm, tn), lambda i, j: (i, j)),
        out_shape=jax.ShapeDtypeStruct((m, n), out_dtype),
        compiler_params=_cparams(("parallel", "parallel"), _mm_vmem(tm, tn, tk, w.dtype.itemsize)),
        name="matmul",
    )(a, w)


def _matmul_swiglu(a, w_gate_up):
    m, k = a.shape
    n = w_gate_up.shape[1] // 2
    tm, tn, tk = _mm_tiles(m, k, n, w_gate_up.dtype.itemsize)
    assert tk == k
    nb = n // tn
    return pl.pallas_call(
        _mm_swiglu_kernel,
        grid=(m // tm, nb),
        in_specs=[pl.BlockSpec((tm, k), lambda i, j: (i, 0)),
                  pl.BlockSpec((k, tn), lambda i, j: (0, j)),
                  pl.BlockSpec((k, tn), lambda i, j: (0, j + nb))],
        out_specs=pl.BlockSpec((tm, tn), lambda i, j: (i, j)),
        out_shape=jax.ShapeDtypeStruct((m, n), BF16),
        compiler_params=_cparams(("parallel", "parallel"),
                                 _mm_vmem(tm, tn, tk, w_gate_up.dtype.itemsize, n_w=2)),
        name="matmul_swiglu",
    )(a, w_gate_up, w_gate_up)


def _matmul_resid(a, w, resid, gate):
    m, k = a.shape
    n = w.shape[1]
    tm, tn, tk = _mm_tiles(m, k, n, w.dtype.itemsize)
    nk = k // tk
    scratch = [pltpu.VMEM((tm, tn), F32)] if nk > 1 else []
    return pl.pallas_call(
        functools.partial(_mm_resid_kernel, nk=nk),
        grid=(m // tm, n // tn, nk),
        in_specs=[pl.BlockSpec((tm, tk), lambda i, j, kk: (i, kk)),
                  pl.BlockSpec((tk, tn), lambda i, j, kk: (kk, j)),
                  pl.BlockSpec((tm, tn), lambda i, j, kk: (i, j)),
                  pl.BlockSpec((1, tn), lambda i, j, kk: (0, j))],
        out_specs=pl.BlockSpec((tm, tn), lambda i, j, kk: (i, j)),
        out_shape=jax.ShapeDtypeStruct((m, n), F32),
        scratch_shapes=scratch,
        compiler_params=_cparams(("parallel", "parallel", "arbitrary"),
                                 _mm_vmem(tm, tn, tk, w.dtype.itemsize) + 3 * tm * tn * 4),
        name="matmul_resid",
    )(a, w, resid, gate.reshape(1, n))


def _rope_tables(s, d, lane_groups):
    pos = jnp.arange(s, dtype=jnp.int32).astype(F32)
    inv = ROPE_THETA ** (-jnp.arange(0, d, 2, dtype=F32) / d)
    ang = pos[:, None] * inv[None, :]
    cos, sin = jnp.cos(ang), jnp.sin(ang)
    cos_t = jnp.tile(jnp.concatenate([cos, cos], axis=1), (1, lane_groups))
    sin_t = jnp.tile(jnp.concatenate([-sin, sin], axis=1), (1, lane_groups))
    return cos_t, sin_t


def _rot_half(x, d):
    if d == V7X_LANES:
        return pltpu.roll(x, d // 2, 1)
    lane = lax.broadcasted_iota(I32, x.shape, 1)
    first = (lane & (d - 1)) < d // 2
    return jnp.where(first, pltpu.roll(x, V7X_LANES - d // 2, 1), pltpu.roll(x, d // 2, 1))


def _prep_kernel(x_ref, *rest, rope_dim, scale):
    o_ref = rest[-1]
    x = x_ref[...]
    if rope_dim:
        cos_ref, sin_ref = rest[0], rest[1]
        x = x * cos_ref[...] + _rot_half(x, rope_dim) * sin_ref[...]
    if scale != 1.0:
        x = x * scale
    o_ref[...] = x.astype(o_ref.dtype)


def _prep_heads(proj, col0, n_blocks, tables=None, rope_dim=0, scale=1.0, head_major=True):
    s = proj.shape[0]
    ts = _largest_tile(s, V7X_SUBLANES, 1024)
    cb0 = col0 // V7X_LANES
    args = [proj]
    specs = [pl.BlockSpec((ts, V7X_LANES), lambda h, i: (i, cb0 + h))]
    if rope_dim:
        args += list(tables)
        specs += [pl.BlockSpec((ts, V7X_LANES), lambda h, i: (i, 0))] * 2
    if head_major:
        out_spec = pl.BlockSpec((None, ts, V7X_LANES), lambda h, i: (h, i, 0))
        out_shape = jax.ShapeDtypeStruct((n_blocks, s, V7X_LANES), BF16)
    else:
        out_spec = pl.BlockSpec((ts, V7X_LANES), lambda h, i: (i, h))
        out_shape = jax.ShapeDtypeStruct((s, n_blocks * V7X_LANES), BF16)
    return pl.pallas_call(
        functools.partial(_prep_kernel, rope_dim=rope_dim, scale=scale),
        grid=(n_blocks, s // ts),
        in_specs=specs,
        out_specs=out_spec,
        out_shape=out_shape,
        compiler_params=_cparams(("parallel", "parallel"), 16 * ts * V7X_LANES * 4),
        name="prep_heads",
    )(*args)


def _ikw_kernel(x_ref, cos_ref, sin_ref, ik_ref, iw_ref):
    x = x_ref[...]
    y = x * cos_ref[...] + _rot_half(x, IDX_DIM) * sin_ref[...]
    lane = lax.broadcasted_iota(I32, x.shape, 1)
    ik_ref[...] = jnp.where(lane < IDX_DIM, y, pltpu.roll(y, IDX_DIM, 1)).astype(ik_ref.dtype)
    iw_ref[...] = pltpu.roll(x, V7X_LANES - IDX_DIM, 1)


def _prep_ikw(proj, col0, tables):
    s = proj.shape[0]
    ts = _largest_tile(s, V7X_SUBLANES, 1024)
    cb = col0 // V7X_LANES
    blk = pl.BlockSpec((ts, V7X_LANES), lambda i: (i, 0))
    return pl.pallas_call(
        _ikw_kernel,
        grid=(s // ts,),
        in_specs=[pl.BlockSpec((ts, V7X_LANES), lambda i: (i, cb)), blk, blk],
        out_specs=[blk, blk],
        out_shape=[jax.ShapeDtypeStruct((s, V7X_LANES), BF16),
                   jax.ShapeDtypeStruct((s, V7X_LANES), F32)],
        compiler_params=_cparams(("parallel",), 16 * ts * V7X_LANES * 4),
        name="prep_ikw",
    )(proj, *tables)


def _dsa_kernel(iq_ref, iwt_ref, ik_ref, q_ref, k_ref, vt_ref, o_ref,
                rhs_scr, key_scr, m_scr, l_scr, acc_scr, s_scr, ma_scr, *, topk, rep, idx_scale):
    tq, tk = DSA_TQ, DSA_TK
    nsub = tk // V7X_SUBLANES
    r_all = rep * tq
    i = pl.program_id(0)
    start = i * tq
    n_kt = (start + tq + tk - 1) >> int(math.log2(tk))

    lane = lax.broadcasted_iota(I32, (tq, V7X_LANES), 1)
    for p in range(IDX_HEADS // 2):
        pair = iq_ref[:, p * V7X_LANES:(p + 1) * V7X_LANES]
        zero = jnp.zeros_like(pair)
        rhs_scr[p, :tq] = jnp.where(lane < IDX_DIM, pair, zero)
        rhs_scr[p, tq:] = jnp.where(lane >= IDX_DIM, pair, zero)

    iwt = iwt_ref[...]
    key_row = lax.broadcasted_iota(I32, (tk, tq), 0)
    q_lane = lax.broadcasted_iota(I32, (tk, tq), 1)
    key_lim = start + ((q_lane >> int(math.log2(CHUNK))) + 1) * CHUNK

    def score_body(kt, carry):
        off = pl.multiple_of(kt * tk, tk)
        ik_t = ik_ref[pl.ds(off, tk), :]
        acc = jnp.zeros((tk, tq), F32)
        for p in range(IDX_HEADS // 2):
            d = lax.dot_general(ik_t, rhs_scr[p], (((1,), (1,)), ((), ())),
                                preferred_element_type=F32)
            for half in range(2):
                h = 2 * p + half
                acc = acc + iwt[h:h + 1, :] * jnp.maximum(d[:, half * tq:(half + 1) * tq], 0.0)
        bits = pltpu.bitcast(acc * idx_scale, I32)
        key = jnp.where(bits < 0, bits ^ 0x7FFFFFFF, bits)
        key_scr[kt] = jnp.where(key_row + off < key_lim, key, INT_MIN)
        return carry

    lax.fori_loop(0, n_kt, score_body, 0)

    n_part = 4

    def count_ge(cand):
        def body(kt, parts):
            hit = jnp.where(key_scr[kt] >= cand, 1, 0)
            parts = list(parts)
            for c in range(nsub):
                parts[c % n_part] = parts[c % n_part] + hit[c * V7X_SUBLANES:(c + 1) * V7X_SUBLANES]
            return tuple(parts)

        parts = lax.fori_loop(0, n_kt, body,
                              tuple(jnp.zeros((V7X_SUBLANES, tq), I32) for _ in range(n_part)))
        return jnp.sum(sum(parts[1:], parts[0]), axis=0, keepdims=True)

    zero = jnp.zeros((1, tq), I32)
    thr = jnp.where(count_ge(zero) >= topk, zero, zero + INT_MIN)

    def bisect(b, thr):
        cand = thr + lax.shift_left(jnp.int32(1), 30 - b)
        return jnp.where(count_ge(cand) >= topk, cand, thr)

    thr = lax.fori_loop(0, 31, bisect, thr)
    thr = jnp.maximum(thr, INT_MIN + 1)

    m_scr[...] = jnp.full(m_scr.shape, NEG_BIG, F32)
    l_scr[...] = jnp.zeros(l_scr.shape, F32)
    acc_scr[...] = jnp.zeros(acc_scr.shape, F32)

    def stage_a(kt):
        slot = kt & 1
        off = pl.multiple_of(kt * tk, tk)
        bias = jnp.where(key_scr[kt] >= thr, 0.0, NEG_BIG)
        bias = jnp.concatenate([bias] * rep, axis=1)
        for g in range(N_KV_A):
            qg = q_ref[g * rep:(g + 1) * rep].reshape(r_all, HEAD_DIM)
            s = lax.dot_general(k_ref[g, pl.ds(off, tk), :], qg, (((1,), (1,)), ((), ())),
                                preferred_element_type=F32) + bias
            m_old = m_scr[g]
            m_new = jnp.maximum(m_old, jnp.max(s, axis=0, keepdims=True))
            s_scr[slot, g] = s
            ma_scr[slot, g, 0:1] = m_new
            ma_scr[slot, g, 1:2] = jnp.exp2(m_old - m_new)
            m_scr[g] = m_new

    def stage_b(kt):
        slot = kt & 1
        for g in range(N_KV_A):
            m_new = ma_scr[slot, g, 0:1]
            alpha = ma_scr[slot, g, 1:2]
            p = jnp.exp2(s_scr[slot, g] - m_new)
            l_scr[g] = alpha * l_scr[g] + jnp.sum(p, axis=0, keepdims=True)
            acc_scr[g] = alpha * acc_scr[g] + jnp.dot(vt_ref[g, kt], p.astype(BF16),
                                                      preferred_element_type=F32)

    stage_a(0)

    def att_body(kt, carry):
        stage_b(kt - 1)
        stage_a(kt)
        return carry

    lax.fori_loop(1, n_kt, att_body, 0)
    stage_b(n_kt - 1)
    for g in range(N_KV_A):
        out_t = acc_scr[g] / l_scr[g]
        for r in range(rep):
            hh = g * rep + r
            o_ref[:, hh * HEAD_DIM:(hh + 1) * HEAD_DIM] = out_t[:, r * tq:(r + 1) * tq].T


def _dsa(iq, iwt, ik2, qa, ka, vat, topk):
    n_heads, s, _ = qa.shape
    rep = n_heads // N_KV_A
    tq, tk = DSA_TQ, DSA_TK
    assert s % tk == 0
    idx_scale = (IDX_HEADS ** -0.5) * (IDX_DIM ** -0.5)
    vmem = (4 * N_KV_A * s * HEAD_DIM * 2 * 2 + 2 * s * V7X_LANES * 2 + tq * s * 4
            + 32 * rep * tq * tk * 4)
    return pl.pallas_call(
        functools.partial(_dsa_kernel, topk=topk, rep=rep, idx_scale=idx_scale),
        grid=(s // tq,),
        in_specs=[
            pl.BlockSpec((tq, IDX_HEADS * IDX_DIM), lambda i: (i, 0)),
            pl.BlockSpec((IDX_HEADS, tq), lambda i: (0, i)),
            pl.BlockSpec((s, V7X_LANES), lambda i: (0, 0)),
            pl.BlockSpec((n_heads, tq, HEAD_DIM), lambda i: (0, i, 0)),
            pl.BlockSpec((N_KV_A, s, HEAD_DIM), lambda i: (0, 0, 0)),
            pl.BlockSpec((N_KV_A, s // tk, HEAD_DIM, tk), lambda i: (0, 0, 0, 0)),
        ],
        out_specs=pl.BlockSpec((tq, n_heads * HEAD_DIM), lambda i: (i, 0)),
        out_shape=jax.ShapeDtypeStruct((s, n_heads * HEAD_DIM), F32),
        scratch_shapes=[
            pltpu.VMEM((IDX_HEADS // 2, 2 * tq, V7X_LANES), BF16),
            pltpu.VMEM((s // tk, tk, tq), I32),
            pltpu.VMEM((N_KV_A, 1, rep * tq), F32),
            pltpu.VMEM((N_KV_A, 1, rep * tq), F32),
            pltpu.VMEM((N_KV_A, HEAD_DIM, rep * tq), F32),
            pltpu.VMEM((2, N_KV_A, tk, rep * tq), F32),
            pltpu.VMEM((2, N_KV_A, 2, rep * tq), F32),
        ],
        compiler_params=_cparams(("parallel",), vmem),
        name="dsa",
    )(iq, iwt, ik2, qa, ka, vat)


def _sb_kernel(q_ref, k_ref, vt_ref, o_ref):
    t = SB_T
    nrg = t // V7X_SUBLANES
    n_h = q_ref.shape[0]
    i = pl.program_id(1)
    o_ref[...] = jnp.zeros(o_ref.shape, F32)

    def tile(hh, j, carry, diagonal):
        z = lax.dot_general(k_ref[hh, j], q_ref[hh], (((1,), (1,)), ((), ())),
                            preferred_element_type=F32)
        e = jnp.exp(-jnp.abs(z))
        r = 1.0 / (1.0 + e)
        er = e * r
        pos = z >= 0.0
        beta = jnp.where(pos, r, er)
        keep = jnp.where(pos, er, r)
        if diagonal:
            rr = lax.broadcasted_iota(I32, (t, t), 0)
            tt = lax.broadcasted_iota(I32, (t, t), 1)
            key_pos = (rr & (V7X_SUBLANES - 1)) * nrg + (rr >> 3)
            causal = key_pos < tt
            beta = jnp.where(causal, beta, 0.0)
            keep = jnp.where(causal, keep, 1.0)
        run8 = jnp.ones((V7X_SUBLANES, t), F32)
        a_loc = [None] * nrg
        for g in range(nrg - 1, -1, -1):
            rows = slice(g * V7X_SUBLANES, (g + 1) * V7X_SUBLANES)
            a_loc[g] = beta[rows] * run8
            run8 = run8 * keep[rows]
        run = carry
        offs = [None] * V7X_SUBLANES
        for u in range(V7X_SUBLANES - 1, -1, -1):
            offs[u] = run
            run = run * run8[u:u + 1]
        off8 = jnp.concatenate(offs, axis=0)
        a = jnp.concatenate([al * off8 for al in a_loc], axis=0)
        o_ref[hh] += jnp.dot(vt_ref[hh, j], a.astype(BF16), preferred_element_type=F32)
        return run

    carries = tuple(tile(hh, i, jnp.ones((1, t), F32), True) for hh in range(n_h))

    def cond(state):
        j, cs = state
        alive = jnp.max(functools.reduce(jnp.maximum, cs)) > 0.0
        return jnp.logical_and(j >= 0, alive)

    def body(state):
        j, cs = state
        return j - 1, tuple(tile(hh, j, cs[hh], False) for hh in range(n_h))

    lax.while_loop(cond, body, (i - 1, carries))


def _sb(q, k_perm, v_t):
    n_heads, s, _ = q.shape
    t = SB_T
    nb = s // t
    n_h = SB_HEADS if n_heads % SB_HEADS == 0 else 1
    return pl.pallas_call(
        _sb_kernel,
        grid=(n_heads // n_h, nb),
        in_specs=[
            pl.BlockSpec((n_h, t, HEAD_DIM), lambda h, i: (h, i, 0)),
            pl.BlockSpec((n_h, nb, t, HEAD_DIM), lambda h, i: (h, 0, 0, 0)),
            pl.BlockSpec((n_h, nb, HEAD_DIM, t), lambda h, i: (h, 0, 0, 0)),
        ],
        out_specs=pl.BlockSpec((n_h, HEAD_DIM, t), lambda h, i: (h, 0, i)),
        out_shape=jax.ShapeDtypeStruct((n_heads, HEAD_DIM, s), F32),
        compiler_params=_cparams(("parallel", "parallel"),
                                 n_h * (8 * s * HEAD_DIM * 2 + 40 * t * t * 4)),
        name="stick_breaking",
    )(q, k_perm, v_t)


def _cb_kernel(q_ref, k_ref, v_ref, b_ref, o_ref, *, win, pad):
    tq = CB_TQ
    i = pl.program_id(1)
    off = pl.multiple_of(i * tq, tq)
    col = lax.broadcasted_iota(I32, (tq, win), 1)
    in_seq = col + off >= pad
    for hh in range(q_ref.shape[0]):
        k_w = k_ref[hh, pl.ds(off, win), :]
        v_w = v_ref[hh, pl.ds(off, win), :]
        s = lax.dot_general(q_ref[hh], k_w, (((1,), (1,)), ((), ())), preferred_element_type=F32)
        s = jnp.where(in_seq, s + b_ref[hh], NEG_BIG)
        m = jnp.max(s, axis=-1, keepdims=True)
        p = jnp.exp(s - m)
        l = jnp.sum(p, axis=-1, keepdims=True)
        o_ref[:, hh * HEAD_DIM:(hh + 1) * HEAD_DIM] = jnp.dot(
            p.astype(BF16), v_w, preferred_element_type=F32) / l


def _cb(q, k_pad, v_pad, bias, pad):
    n_heads, s, _ = q.shape
    tq = CB_TQ
    win = pad + tq
    sp = k_pad.shape[1]
    n_h = CB_HEADS if n_heads % CB_HEADS == 0 else 1
    return pl.pallas_call(
        functools.partial(_cb_kernel, win=win, pad=pad),
        grid=(n_heads // n_h, s // tq),
        in_specs=[
            pl.BlockSpec((n_h, tq, HEAD_DIM), lambda h, i: (h, i, 0)),
            pl.BlockSpec((n_h, sp, HEAD_DIM), lambda h, i: (h, 0, 0)),
            pl.BlockSpec((n_h, sp, HEAD_DIM), lambda h, i: (h, 0, 0)),
            pl.BlockSpec((n_h, tq, win), lambda h, i: (h, 0, 0)),
        ],
        out_specs=pl.BlockSpec((tq, n_h * HEAD_DIM), lambda h, i: (i, h)),
        out_shape=jax.ShapeDtypeStruct((s, n_heads * HEAD_DIM), F32),
        compiler_params=_cparams(("parallel", "parallel"),
                                 n_h * (8 * sp * HEAD_DIM * 2 + 16 * tq * win * 4)),
        name="chunk_band",
    )(q, k_pad, v_pad, bias)


def _cb_bias(rel_bias):
    tq = CB_TQ
    pad = N_LEFT_CHUNKS * CHUNK
    win = pad + tq
    n_heads = rel_bias.shape[0]
    qi = np.arange(tq)[:, None]
    kj = np.arange(win)[None, :]
    q_chunk = qi // CHUNK
    k_chunk = kj // CHUNK - N_LEFT_CHUNKS
    in_band = (k_chunk <= q_chunk) & (k_chunk >= q_chunk - N_LEFT_CHUNKS)
    p = win + tq
    n_clipped = pad + tq - 1 - REL_CLIP
    lo = 2 * REL_CLIP - (p - 1 - n_clipped)
    assert lo >= 0 and pad - (win - 1) >= -REL_CLIP
    u = jnp.concatenate([jnp.broadcast_to(rel_bias[:, 2 * REL_CLIP:], (n_heads, n_clipped)),
                         rel_bias[:, lo:2 * REL_CLIP + 1][:, ::-1]], axis=1).astype(F32)
    rows = jnp.tile(u, (1, tq + 1))[:, :tq * (p + 1)].reshape(n_heads, tq, p + 1)[:, :, :win]
    bias = rows[:, ::-1, :]
    return jnp.where(jnp.asarray(in_band)[None], bias, NEG_BIG)


def _headnorm_kernel(a_ref, b_ref, c_ref, g_ref, o_ref):
    col = 0
    for ref in (a_ref, b_ref, c_ref):
        for h in range(ref.shape[1] // HEAD_DIM):
            x = ref[:, h * HEAD_DIM:(h + 1) * HEAD_DIM]
            ms = jnp.mean(x * x, axis=-1, keepdims=True)
            y = x * lax.rsqrt(ms + EPS) * g_ref[:, col:col + HEAD_DIM]
            o_ref[:, col:col + HEAD_DIM] = y.astype(o_ref.dtype)
            col += HEAD_DIM


def _headnorm(o_a, o_b, o_c, g):
    s = o_a.shape[0]
    d = o_a.shape[1] + o_b.shape[1] + o_c.shape[1]
    ts = _largest_tile(s, V7X_SUBLANES, 256)
    return pl.pallas_call(
        _headnorm_kernel,
        grid=(s // ts,),
        in_specs=[pl.BlockSpec((ts, o_a.shape[1]), lambda i: (i, 0)),
                  pl.BlockSpec((ts, o_b.shape[1]), lambda i: (i, 0)),
                  pl.BlockSpec((ts, o_c.shape[1]), lambda i: (i, 0)),
                  pl.BlockSpec((1, d), lambda i: (0, 0))],
        out_specs=pl.BlockSpec((ts, d), lambda i: (i, 0)),
        out_shape=jax.ShapeDtypeStruct((s, d), BF16),
        compiler_params=_cparams(("parallel",), 6 * ts * d * 4),
        name="headnorm",
    )(o_a, o_b, o_c, g.reshape(1, d))


def _pack_kernel(x_ref, o_ref, *, runs):
    end = 0
    for src, dst, width in runs:
        if dst > end:
            o_ref[:, end:dst] = jnp.zeros((o_ref.shape[0], dst - end), o_ref.dtype)
        o_ref[:, dst:dst + width] = x_ref[:, src:src + width].astype(o_ref.dtype)
        end = dst + width
    if end < o_ref.shape[1]:
        o_ref[:, end:] = jnp.zeros((o_ref.shape[0], o_ref.shape[1] - end), o_ref.dtype)


def _pack_w_in(w, sizes):
    offsets, runs, src, dst = [], [], 0, 0
    for width in sizes:
        fill = (-dst) % V7X_LANES if width >= V7X_LANES else 0
        dst += fill
        offsets.append(dst)
        if runs and not fill:
            runs[-1] = (runs[-1][0], runs[-1][1], runs[-1][2] + width)
        else:
            runs.append((src, dst, width))
        src += width
        dst += width
    n_out = dst + (-dst) % V7X_LANES
    k, n_in = w.shape
    tr = _largest_tile(k, 16, 128)
    packed = pl.pallas_call(
        functools.partial(_pack_kernel, runs=tuple(runs)),
        grid=(k // tr,),
        in_specs=[pl.BlockSpec((tr, n_in), lambda i: (i, 0))],
        out_specs=pl.BlockSpec((tr, n_out), lambda i: (i, 0)),
        out_shape=jax.ShapeDtypeStruct((k, n_out), BF16),
        compiler_params=_cparams(("parallel",), 4 * tr * n_in * 4),
        name="pack_w_in",
    )(w)
    return packed, offsets


def kernel(x, c, w_ada, b_ada, norm_attn_g, w_in, rel_bias, head_norm_g, w_out, norm_ffn_g,
           w_gate_up, w_down, final_norm_g):
    b, s, d = x.shape
    assert b == 1, "single-sequence prefill only"
    depth = w_ada.shape[0]
    n_heads = d // HEAD_DIM
    ha = 3 * n_heads // 8
    hb = 3 * n_heads // 8
    hc = n_heads - ha - hb
    assert ha % N_KV_A == 0 and s % SB_T == 0
    sizes = (ha * HEAD_DIM, N_KV_A * HEAD_DIM, N_KV_A * HEAD_DIM, IDX_HEADS * IDX_DIM, IDX_DIM,
             IDX_HEADS, hb * HEAD_DIM, hb * HEAD_DIM, hb * HEAD_DIM,
             hc * HEAD_DIM, hc * HEAD_DIM, hc * HEAD_DIM)
    assert w_in.shape[2] == sum(sizes)
    topk = min(IDX_TOPK_MAX, s // 4)
    scale = HEAD_DIM ** -0.5
    pad = N_LEFT_CHUNKS * CHUNK

    rope128 = _rope_tables(s, HEAD_DIM, 1)
    rope64 = _rope_tables(s, IDX_DIM, V7X_LANES // IDX_DIM)

    xs = x.reshape(s, d)
    mod = _adaln(c, w_ada, b_ada)

    for l in range(depth):
        sh1, sc1, g1, sh2, sc2, g2 = [mod[l, :, n * d:(n + 1) * d] for n in range(6)]

        h = _norm(xs, norm_attn_g[l], sc1, sh1)
        w_packed, offs = _pack_w_in(w_in[l], sizes)
        o_qa, o_ka, o_va, o_iq, o_ik, o_iw, o_qb, o_kb, o_vb, o_qc, o_kc, o_vc = offs
        assert o_iw == o_ik + IDX_DIM
        proj = _matmul(h, w_packed, F32)

        qa = _prep_heads(proj, o_qa, ha, rope128, HEAD_DIM, scale * math.log2(math.e))
        ka = _prep_heads(proj, o_ka, N_KV_A, rope128, HEAD_DIM)
        va = _prep_heads(proj, o_va, N_KV_A)
        vat = va.reshape(N_KV_A, s // DSA_TK, DSA_TK, HEAD_DIM).swapaxes(2, 3)
        iq = _prep_heads(proj, o_iq, IDX_HEADS * IDX_DIM // V7X_LANES, rope64, IDX_DIM,
                         head_major=False)
        ik2, iw = _prep_ikw(proj, o_ik, rope64)
        o_a = _dsa(iq, iw[:, :IDX_HEADS].T, ik2, qa, ka, vat, topk)

        qb = _prep_heads(proj, o_qb, hb, scale=scale)
        kb = _prep_heads(proj, o_kb, hb)
        vb = _prep_heads(proj, o_vb, hb)
        nb, sub = s // SB_T, V7X_SUBLANES
        kb = kb.reshape(hb, nb, sub, SB_T // sub, HEAD_DIM).swapaxes(2, 3).reshape(hb, nb, SB_T, HEAD_DIM)
        vbt = vb.reshape(hb, nb, sub, SB_T // sub, HEAD_DIM).swapaxes(2, 3).reshape(hb, nb, SB_T, HEAD_DIM)
        vbt = vbt.swapaxes(2, 3)
        o_bt = _sb(qb, kb, vbt)
        o_b = jnp.transpose(o_bt, (2, 0, 1)).reshape(s, hb * HEAD_DIM)

        qc = _prep_heads(proj, o_qc, hc, scale=scale)
        kc = _prep_heads(proj, o_kc, hc)
        vc = _prep_heads(proj, o_vc, hc)
        kc = jnp.pad(kc, ((0, 0), (pad, 0), (0, 0)))
        vc = jnp.pad(vc, ((0, 0), (pad, 0), (0, 0)))
        o_c = _cb(qc, kc, vc, _cb_bias(rel_bias[l]), pad)

        mix = _headnorm(o_a, o_b, o_c, head_norm_g[l])
        xs = _matmul_resid(mix, w_out[l], xs, g1)

        h = _norm(xs, norm_ffn_g[l], sc2, sh2)
        act = _matmul_swiglu(h, w_gate_up[l])
        xs = _matmul_resid(act, w_down[l].astype(BF16), xs, g2)

    out = _norm(xs, final_norm_g, out_dtype=x.dtype)
    return out.reshape(b, s, d)
```

```python
import functools
import math

import numpy as np
import jax
import jax.numpy as jnp
from jax import lax
from jax.experimental import pallas as pl
from jax.experimental.pallas import tpu as pltpu

F32 = jnp.float32
BF16 = jnp.bfloat16
I32 = jnp.int32
I16 = jnp.int16

HEAD_DIM = 128
N_KV_A = 2
IDX_HEADS = 16
IDX_DIM = 64
IDX_TOPK_MAX = 256
CHUNK = 64
N_LEFT_CHUNKS = 8
REL_CLIP = 256
ROPE_THETA = 10000.0
EPS = 1e-6

V7X_LANES = 128
V7X_SUBLANES = 8
V7X_VMEM_BYTES = 64 * 1024 * 1024

NEG_BIG = -1e30
INT_MIN = -(2 ** 31)

DSA_TQ = 128
DSA_TK = 512
SB_T = 256
SB_HEADS = 2
CB_TQ = 128
CB_HEADS = 2


def _cparams(semantics, vmem_bytes, flags=None):
    limit = int(min(V7X_VMEM_BYTES * 7 // 8, max(vmem_bytes, 16 * 1024 * 1024)))
    return pltpu.CompilerParams(dimension_semantics=semantics, vmem_limit_bytes=limit, flags=flags)


def _largest_tile(n, unit, cap):
    best = None
    t = unit
    while t <= min(n, cap):
        if n % t == 0:
            best = t
        t += unit
    assert best is not None, (n, unit, cap)
    return best


def _adaln_kernel(c_ref, w_ref, b_ref, o_ref, acc_ref):
    k = pl.program_id(1)
    c = c_ref[...]
    cond = c * jax.nn.sigmoid(c)
    tr, n = w_ref.shape

    @pl.when(k == 0)
    def _():
        acc_ref[...] = jnp.zeros_like(acc_ref)

    for j in range(n // V7X_LANES):
        cols = slice(j * V7X_LANES, (j + 1) * V7X_LANES)
        prod = w_ref[:, cols] * cond
        part = prod[0:V7X_SUBLANES]
        for g in range(1, tr // V7X_SUBLANES):
            part = part + prod[g * V7X_SUBLANES:(g + 1) * V7X_SUBLANES]
        acc_ref[:, cols] += part

    @pl.when(k == pl.num_programs(1) - 1)
    def _():
        o_ref[...] = jnp.sum(acc_ref[...], axis=0, keepdims=True) + b_ref[...]


def _adaln(c, w_ada, b_ada):
    depth, d, n = w_ada.shape
    tr = _largest_tile(d, V7X_SUBLANES, 128)
    c_b = jnp.broadcast_to(c.reshape(d, 1), (d, V7X_LANES))
    out = pl.pallas_call(
        _adaln_kernel,
        grid=(depth, d // tr),
        in_specs=[
            pl.BlockSpec((tr, V7X_LANES), lambda l, k: (k, 0)),
            pl.BlockSpec((None, tr, n), lambda l, k: (l, k, 0)),
            pl.BlockSpec((None, 1, n), lambda l, k: (l, 0, 0)),
        ],
        out_specs=pl.BlockSpec((None, 1, n), lambda l, k: (l, 0, 0)),
        out_shape=jax.ShapeDtypeStruct((depth, 1, n), F32),
        scratch_shapes=[pltpu.VMEM((V7X_SUBLANES, n), F32)],
        compiler_params=_cparams(("parallel", "arbitrary"), 3 * tr * n * 4),
        name="adaln",
    )(c_b, w_ada, b_ada.reshape(depth, 1, n))
    return out


def _norm_kernel(x_ref, g_ref, *rest, modulate):
    o_ref = rest[-1]
    x = x_ref[...]
    ms = jnp.mean(x * x, axis=-1, keepdims=True)
    y = x * lax.rsqrt(ms + EPS) * g_ref[...]
    if modulate:
        sc_ref, sh_ref = rest[0], rest[1]
        y = y * (1.0 + sc_ref[...]) + sh_ref[...]
    o_ref[...] = y.astype(o_ref.dtype)


def _norm(x, g, scale=None, shift=None, out_dtype=BF16):
    s, d = x.shape
    ts = _largest_tile(s, V7X_SUBLANES, 256)
    modulate = scale is not None
    row = pl.BlockSpec((1, d), lambda i: (0, 0))
    args = [x, g.reshape(1, d)]
    specs = [pl.BlockSpec((ts, d), lambda i: (i, 0)), row]
    if modulate:
        args += [scale.reshape(1, d), shift.reshape(1, d)]
        specs += [row, row]
    return pl.pallas_call(
        functools.partial(_norm_kernel, modulate=modulate),
        grid=(s // ts,),
        in_specs=specs,
        out_specs=pl.BlockSpec((ts, d), lambda i: (i, 0)),
        out_shape=jax.ShapeDtypeStruct((s, d), out_dtype),
        compiler_params=_cparams(("parallel",), 6 * ts * d * 4),
        name="rmsnorm",
    )(*args)


def _dot(a, w):
    return jnp.dot(a, w.astype(BF16), preferred_element_type=F32)


def _mm_plain_kernel(a_ref, w_ref, o_ref):
    o_ref[...] = _dot(a_ref[...], w_ref[...]).astype(o_ref.dtype)


def _mm_swiglu_kernel(a_ref, wg_ref, wu_ref, o_ref):
    a = a_ref[...]
    gate = _dot(a, wg_ref[...])
    up = _dot(a, wu_ref[...])
    o_ref[...] = (gate * jax.nn.sigmoid(gate) * up).astype(o_ref.dtype)


def _mm_resid_kernel(a_ref, w_ref, r_ref, g_ref, o_ref, *acc, nk):
    if nk == 1:
        o_ref[...] = r_ref[...] + g_ref[...] * _dot(a_ref[...], w_ref[...])
        return
    acc_ref, = acc
    k = pl.program_id(2)

    @pl.when(k == 0)
    def _():
        acc_ref[...] = jnp.zeros_like(acc_ref)

    acc_ref[...] += _dot(a_ref[...], w_ref[...])

    @pl.when(k == nk - 1)
    def _():
        o_ref[...] = r_ref[...] + g_ref[...] * acc_ref[...]


def _mm_tiles(m, k, n, w_itemsize):
    tn = _largest_tile(n, V7X_LANES, 768 if w_itemsize == 2 else 256)
    if k <= 4096:
        return _largest_tile(m, V7X_SUBLANES, 1024), tn, k
    if w_itemsize == 2:
        return _largest_tile(m, V7X_SUBLANES, 512), _largest_tile(n, V7X_LANES, 256), k
    return _largest_tile(m, V7X_SUBLANES, 1024), tn, _largest_tile(k, V7X_LANES, 5632)


def _mm_vmem(tm, tn, tk, w_itemsize, n_w=1):
    return (2 * tm * tk * 2 + n_w * (2 * tk * tn * w_itemsize + tk * tn * 2)
            + (4 + 2 * n_w) * tm * tn * 4)


def _matmul(a, w, l, out_dtype):
    m, k = a.shape
    n = w.shape[2]
    tm, tn, tk = _mm_tiles(m, k, n, w.dtype.itemsize)
    assert tk == k
    return pl.pallas_call(
        _mm_plain_kernel,
        grid=(m // tm, n // tn),
        in_specs=[pl.BlockSpec((tm, k), lambda i, j: (i, 0)),
                  pl.BlockSpec((None, k, tn), lambda i, j: (l, 0, j))],
        out_specs=pl.BlockSpec((tm, tn), lambda i, j: (i, j)),
        out_shape=jax.ShapeDtypeStruct((m, n), out_dtype),
        compiler_params=_cparams(("parallel", "parallel"), _mm_vmem(tm, tn, tk, w.dtype.itemsize)),
        name="matmul",
    )(a, w)


def _matmul_swiglu(a, w_gate_up, l):
    m, k = a.shape
    n = w_gate_up.shape[2] // 2
    tm, tn, tk = _mm_tiles(m, k, n, w_gate_up.dtype.itemsize)
    assert tk == k
    nb = n // tn
    return pl.pallas_call(
        _mm_swiglu_kernel,
        grid=(m // tm, nb),
        in_specs=[pl.BlockSpec((tm, k), lambda i, j: (i, 0)),
                  pl.BlockSpec((None, k, tn), lambda i, j: (l, 0, j)),
                  pl.BlockSpec((None, k, tn), lambda i, j: (l, 0, j + nb))],
        out_specs=pl.BlockSpec((tm, tn), lambda i, j: (i, j)),
        out_shape=jax.ShapeDtypeStruct((m, n), BF16),
        compiler_params=_cparams(("parallel", "parallel"),
                                 _mm_vmem(tm, tn, tk, w_gate_up.dtype.itemsize, n_w=2)),
        name="matmul_swiglu",
    )(a, w_gate_up, w_gate_up)


def _matmul_resid(a, w, l, resid, gate):
    m, k = a.shape
    n = w.shape[2]
    tm, tn, tk = _mm_tiles(m, k, n, w.dtype.itemsize)
    nk = k // tk
    scratch = [pltpu.VMEM((tm, tn), F32)] if nk > 1 else []
    return pl.pallas_call(
        functools.partial(_mm_resid_kernel, nk=nk),
        grid=(m // tm, n // tn, nk),
        in_specs=[pl.BlockSpec((tm, tk), lambda i, j, kk: (i, kk)),
                  pl.BlockSpec((None, tk, tn), lambda i, j, kk: (l, kk, j)),
                  pl.BlockSpec((tm, tn), lambda i, j, kk: (i, j)),
                  pl.BlockSpec((1, tn), lambda i, j, kk: (0, j))],
        out_specs=pl.BlockSpec((tm, tn), lambda i, j, kk: (i, j)),
        out_shape=jax.ShapeDtypeStruct((m, n), F32),
        scratch_shapes=scratch,
        compiler_params=_cparams(("parallel", "parallel", "arbitrary"),
                                 _mm_vmem(tm, tn, tk, w.dtype.itemsize) + 3 * tm * tn * 4),
        name="matmul_resid",
    )(a, w, resid, gate.reshape(1, n))


def _rope_tables(s, d, lane_groups):
    pos = jnp.arange(s, dtype=jnp.int32).astype(F32)
    inv = ROPE_THETA ** (-jnp.arange(0, d, 2, dtype=F32) / d)
    ang = pos[:, None] * inv[None, :]
    cos, sin = jnp.cos(ang), jnp.sin(ang)
    cos_t = jnp.tile(jnp.concatenate([cos, cos], axis=1), (1, lane_groups))
    sin_t = jnp.tile(jnp.concatenate([-sin, sin], axis=1), (1, lane_groups))
    return cos_t, sin_t


def _rot_half(x, d):
    if d == V7X_LANES:
        return pltpu.roll(x, d // 2, 1)
    lane = lax.broadcasted_iota(I32, x.shape, 1)
    first = (lane & (d - 1)) < d // 2
    return jnp.where(first, pltpu.roll(x, V7X_LANES - d // 2, 1), pltpu.roll(x, d // 2, 1))


def _prep_kernel(x_ref, *rest, rope_dim, scale, head_major):
    o_ref = rest[-1]
    if rope_dim:
        cos, sin = rest[0][...], rest[1][...]
    for h in range(x_ref.shape[1] // V7X_LANES):
        cols = slice(h * V7X_LANES, (h + 1) * V7X_LANES)
        x = x_ref[:, cols]
        if rope_dim:
            x = x * cos + _rot_half(x, rope_dim) * sin
        if scale != 1.0:
            x = x * scale
        if head_major:
            o_ref[h] = x.astype(o_ref.dtype)
        else:
            o_ref[:, cols] = x.astype(o_ref.dtype)


def _prep_heads(proj, col0, n_blocks, tables=None, rope_dim=0, scale=1.0, head_major=True):
    s = proj.shape[0]
    width = n_blocks * V7X_LANES
    assert col0 % width == 0 or n_blocks == 1, "segment must start at a multiple of its width"
    ts = _largest_tile(s, V7X_SUBLANES, 512)
    args = [proj]
    specs = [pl.BlockSpec((ts, width), lambda i: (i, col0 // width))]
    if rope_dim:
        args += list(tables)
        specs += [pl.BlockSpec((ts, V7X_LANES), lambda i: (i, 0))] * 2
    if head_major:
        out_spec = pl.BlockSpec((n_blocks, ts, V7X_LANES), lambda i: (0, i, 0))
        out_shape = jax.ShapeDtypeStruct((n_blocks, s, V7X_LANES), BF16)
    else:
        out_spec = pl.BlockSpec((ts, width), lambda i: (i, 0))
        out_shape = jax.ShapeDtypeStruct((s, width), BF16)
    return pl.pallas_call(
        functools.partial(_prep_kernel, rope_dim=rope_dim, scale=scale, head_major=head_major),
        grid=(s // ts,),
        in_specs=specs,
        out_specs=out_spec,
        out_shape=out_shape,
        compiler_params=_cparams(("parallel",), 8 * ts * width * 4),
        name="prep_heads",
    )(*args)


def _ikw_kernel(x_ref, cos_ref, sin_ref, ik_ref, iw_ref):
    x = x_ref[...]
    y = x * cos_ref[...] + _rot_half(x, IDX_DIM) * sin_ref[...]
    lane = lax.broadcasted_iota(I32, x.shape, 1)
    ik_ref[...] = jnp.where(lane < IDX_DIM, y, pltpu.roll(y, IDX_DIM, 1)).astype(ik_ref.dtype)
    iw_ref[...] = pltpu.roll(x, V7X_LANES - IDX_DIM, 1)


def _prep_ikw(proj, col0, tables):
    s = proj.shape[0]
    ts = _largest_tile(s, V7X_SUBLANES, 1024)
    cb = col0 // V7X_LANES
    blk = pl.BlockSpec((ts, V7X_LANES), lambda i: (i, 0))
    return pl.pallas_call(
        _ikw_kernel,
        grid=(s // ts,),
        in_specs=[pl.BlockSpec((ts, V7X_LANES), lambda i: (i, cb)), blk, blk],
        out_specs=[blk, blk],
        out_shape=[jax.ShapeDtypeStruct((s, V7X_LANES), BF16),
                   jax.ShapeDtypeStruct((s, V7X_LANES), F32)],
        compiler_params=_cparams(("parallel",), 16 * ts * V7X_LANES * 4),
        name="prep_ikw",
    )(proj, *tables)


def _dsa_kernel(iq_ref, iwt_ref, ik_ref, q_ref, k_ref, vt_ref, o_ref,
                rhs_scr, key_scr, hi_scr, lo_scr, m_scr, l_scr, acc_scr, s_scr, ma_scr,
                *, topk, rep, idx_scale):
    tq, tk = DSA_TQ, DSA_TK
    nsub = tk // V7X_SUBLANES
    r_all = rep * tq
    i = pl.program_id(0)
    start = i * tq
    n_kt = (start + tq + tk - 1) >> int(math.log2(tk))

    lane = lax.broadcasted_iota(I32, (tq, V7X_LANES), 1)
    for p in range(IDX_HEADS // 2):
        pair = iq_ref[:, p * V7X_LANES:(p + 1) * V7X_LANES]
        zero = jnp.zeros_like(pair)
        rhs_scr[p, :tq] = jnp.where(lane < IDX_DIM, pair, zero)
        rhs_scr[p, tq:] = jnp.where(lane >= IDX_DIM, pair, zero)

    iwt = iwt_ref[...]
    key_row = lax.broadcasted_iota(I32, (tk, tq), 0)
    q_lane = lax.broadcasted_iota(I32, (tk, tq), 1)
    key_lim = start + ((q_lane >> int(math.log2(CHUNK))) + 1) * CHUNK

    def score_body(kt, carry):
        off = pl.multiple_of(kt * tk, tk)
        ik_t = ik_ref[pl.ds(off, tk), :]
        acc = jnp.zeros((tk, tq), F32)
        for p in range(IDX_HEADS // 2):
            d = lax.dot_general(ik_t, rhs_scr[p], (((1,), (1,)), ((), ())),
                                preferred_element_type=F32)
            for half in range(2):
                h = 2 * p + half
                acc = acc + iwt[h:h + 1, :] * jnp.maximum(d[:, half * tq:(half + 1) * tq], 0.0)
        bits = pltpu.bitcast(acc * idx_scale, I32)
        key = jnp.where(bits < 0, bits ^ 0x7FFFFFFF, bits)
        key = jnp.where(key_row + off < key_lim, key, INT_MIN)
        key_scr[kt] = key
        hi_scr[kt] = (key >> 16).astype(I16)
        lo_scr[kt] = ((key & 0xFFFF) - 2 ** 15).astype(I16)
        return carry

    lax.fori_loop(0, n_kt, score_body, 0)

    n_part = 4
    rows16 = 2 * V7X_SUBLANES
    i16_min = -(2 ** 15)

    def count16(plane_scr, test):
        def body(kt, parts):
            hit = jnp.where(test(plane_scr[kt]), jnp.int16(1), jnp.int16(0))
            parts = list(parts)
            for c in range(tk // rows16):
                parts[c % n_part] = parts[c % n_part] + hit[c * rows16:(c + 1) * rows16]
            return tuple(parts)

        parts = lax.fori_loop(0, n_kt, body,
                              tuple(jnp.zeros((rows16, tq), I16) for _ in range(n_part)))
        return jnp.sum(sum(parts[1:], parts[0]).astype(I32), axis=0, keepdims=True)

    def kth_largest16(plane_scr, need):
        def count_ge(cand):
            c16 = cand.astype(I16)
            return count16(plane_scr, lambda x: x >= c16)

        zero = jnp.zeros((1, tq), I32)
        t = jnp.where(count_ge(zero) >= need, zero, zero + i16_min)

        def bisect(b, t):
            cand = t + lax.shift_left(jnp.int32(1), 14 - b)
            return jnp.where(count_ge(cand) >= need, cand, t)

        return lax.fori_loop(0, 15, bisect, t)

    t_hi = kth_largest16(hi_scr, topk)
    t_hi16 = t_hi.astype(I16)
    n_above = count16(hi_scr, lambda x: x > t_hi16)

    def mask_lo(kt, carry):
        lo_scr[kt] = jnp.where(hi_scr[kt] == t_hi16, lo_scr[kt], jnp.int16(i16_min))
        return carry

    lax.fori_loop(0, n_kt, mask_lo, 0)
    t_lo = kth_largest16(lo_scr, topk - n_above)
    thr = jnp.left_shift(t_hi, 16) + (t_lo + 2 ** 15)
    thr = jnp.maximum(thr, INT_MIN + 1)

    m_scr[...] = jnp.full(m_scr.shape, NEG_BIG, F32)
    l_scr[...] = jnp.zeros(l_scr.shape, F32)
    acc_scr[...] = jnp.zeros(acc_scr.shape, F32)

    def stage_a(kt):
        slot = kt & 1
        off = pl.multiple_of(kt * tk, tk)
        bias = jnp.where(key_scr[kt] >= thr, 0.0, NEG_BIG)
        bias = jnp.concatenate([bias] * rep, axis=1)
        for g in range(N_KV_A):
            qg = q_ref[g * rep:(g + 1) * rep].reshape(r_all, HEAD_DIM)
            s = lax.dot_general(k_ref[g, pl.ds(off, tk), :], qg, (((1,), (1,)), ((), ())),
                                preferred_element_type=F32) + bias
            m_old = m_scr[g]
            m_new = jnp.maximum(m_old, jnp.max(s, axis=0, keepdims=True))
            s_scr[slot, g] = s
            ma_scr[slot, g, 0:1] = m_new
            ma_scr[slot, g, 1:2] = jnp.exp2(m_old - m_new)
            m_scr[g] = m_new

    def stage_b(kt):
        slot = kt & 1
        for g in range(N_KV_A):
            m_new = ma_scr[slot, g, 0:1]
            alpha = ma_scr[slot, g, 1:2]
            p = jnp.exp2(s_scr[slot, g] - m_new)
            l_scr[g] = alpha * l_scr[g] + jnp.sum(p, axis=0, keepdims=True)
            acc_scr[g] = alpha * acc_scr[g] + jnp.dot(vt_ref[g, kt], p.astype(BF16),
                                                      preferred_element_type=F32)

    stage_a(0)

    def att_body(kt, carry):
        stage_b(kt - 1)
        stage_a(kt)
        return carry

    lax.fori_loop(1, n_kt, att_body, 0)
    stage_b(n_kt - 1)
    for g in range(N_KV_A):
        out_t = acc_scr[g] / l_scr[g]
        for r in range(rep):
            hh = g * rep + r
            o_ref[:, hh * HEAD_DIM:(hh + 1) * HEAD_DIM] = out_t[:, r * tq:(r + 1) * tq].T


def _dsa(iq, iwt, ik2, qa, ka, vat, topk):
    n_heads, s, _ = qa.shape
    rep = n_heads // N_KV_A
    tq, tk = DSA_TQ, DSA_TK
    assert s % tk == 0
    idx_scale = (IDX_HEADS ** -0.5) * (IDX_DIM ** -0.5)
    vmem = (4 * N_KV_A * s * HEAD_DIM * 2 * 2 + 2 * s * V7X_LANES * 2 + tq * s * 4
            + 32 * rep * tq * tk * 4)
    return pl.pallas_call(
        functools.partial(_dsa_kernel, topk=topk, rep=rep, idx_scale=idx_scale),
        grid=(s // tq,),
        in_specs=[
            pl.BlockSpec((tq, IDX_HEADS * IDX_DIM), lambda i: (i, 0)),
            pl.BlockSpec((IDX_HEADS, tq), lambda i: (0, i)),
            pl.BlockSpec((s, V7X_LANES), lambda i: (0, 0)),
            pl.BlockSpec((n_heads, tq, HEAD_DIM), lambda i: (0, i, 0)),
            pl.BlockSpec((N_KV_A, s, HEAD_DIM), lambda i: (0, 0, 0)),
            pl.BlockSpec((N_KV_A, s // tk, HEAD_DIM, tk), lambda i: (0, 0, 0, 0)),
        ],
        out_specs=pl.BlockSpec((tq, n_heads * HEAD_DIM), lambda i: (i, 0)),
        out_shape=jax.ShapeDtypeStruct((s, n_heads * HEAD_DIM), F32),
        scratch_shapes=[
            pltpu.VMEM((IDX_HEADS // 2, 2 * tq, V7X_LANES), BF16),
            pltpu.VMEM((s // tk, tk, tq), I32),
            pltpu.VMEM((s // tk, tk, tq), I16),
            pltpu.VMEM((s // tk, tk, tq), I16),
            pltpu.VMEM((N_KV_A, 1, rep * tq), F32),
            pltpu.VMEM((N_KV_A, 1, rep * tq), F32),
            pltpu.VMEM((N_KV_A, HEAD_DIM, rep * tq), F32),
            pltpu.VMEM((2, N_KV_A, tk, rep * tq), F32),
            pltpu.VMEM((2, N_KV_A, 2, rep * tq), F32),
        ],
        compiler_params=_cparams(("parallel",), vmem),
        name="dsa",
    )(iq, iwt, ik2, qa, ka, vat)


def _sb_kernel(q_ref, k_ref, vt_ref, o_ref):
    t = SB_T
    nrg = t // V7X_SUBLANES
    n_h = q_ref.shape[0]
    i = pl.program_id(1)
    o_ref[...] = jnp.zeros(o_ref.shape, F32)

    def tile(hh, j, carry, diagonal):
        z = lax.dot_general(k_ref[hh, j], q_ref[hh], (((1,), (1,)), ((), ())),
                            preferred_element_type=F32)
        e = jnp.exp(-jnp.abs(z))
        r = 1.0 / (1.0 + e)
        er = e * r
        pos = z >= 0.0
        beta = jnp.where(pos, r, er)
        keep = jnp.where(pos, er, r)
        if diagonal:
            rr = lax.broadcasted_iota(I32, (t, t), 0)
            tt = lax.broadcasted_iota(I32, (t, t), 1)
            key_pos = (rr & (V7X_SUBLANES - 1)) * nrg + (rr >> 3)
            causal = key_pos < tt
            beta = jnp.where(causal, beta, 0.0)
            keep = jnp.where(causal, keep, 1.0)
        run8 = jnp.ones((V7X_SUBLANES, t), F32)
        a_loc = [None] * nrg
        for g in range(nrg - 1, -1, -1):
            rows = slice(g * V7X_SUBLANES, (g + 1) * V7X_SUBLANES)
            a_loc[g] = beta[rows] * run8
            run8 = run8 * keep[rows]
        run = carry
        offs = [None] * V7X_SUBLANES
        for u in range(V7X_SUBLANES - 1, -1, -1):
            offs[u] = run
            run = run * run8[u:u + 1]
        off8 = jnp.concatenate(offs, axis=0)
        a = jnp.concatenate([al * off8 for al in a_loc], axis=0)
        o_ref[hh] += jnp.dot(vt_ref[hh, j], a.astype(BF16), preferred_element_type=F32)
        return run

    carries = tuple(tile(hh, i, jnp.ones((1, t), F32), True) for hh in range(n_h))

    def cond(state):
        j, cs = state
        alive = jnp.max(functools.reduce(jnp.maximum, cs)) > 0.0
        return jnp.logical_and(j >= 0, alive)

    def body(state):
        j, cs = state
        return j - 1, tuple(tile(hh, j, cs[hh], False) for hh in range(n_h))

    lax.while_loop(cond, body, (i - 1, carries))


def _sb(q, k_perm, v_t):
    n_heads, s, _ = q.shape
    t = SB_T
    nb = s // t
    n_h = SB_HEADS if n_heads % SB_HEADS == 0 else 1
    return pl.pallas_call(
        _sb_kernel,
        grid=(n_heads // n_h, nb),
        in_specs=[
            pl.BlockSpec((n_h, t, HEAD_DIM), lambda h, i: (h, i, 0)),
            pl.BlockSpec((n_h, nb, t, HEAD_DIM), lambda h, i: (h, 0, 0, 0)),
            pl.BlockSpec((n_h, nb, HEAD_DIM, t), lambda h, i: (h, 0, 0, 0)),
        ],
        out_specs=pl.BlockSpec((n_h, HEAD_DIM, t), lambda h, i: (h, 0, i)),
        out_shape=jax.ShapeDtypeStruct((n_heads, HEAD_DIM, s), F32),
        compiler_params=_cparams(("parallel", "parallel"),
                                 n_h * (8 * s * HEAD_DIM * 2 + 40 * t * t * 4)),
        name="stick_breaking",
    )(q, k_perm, v_t)


def _cb_kernel(q_ref, k_ref, v_ref, b_ref, o_ref, *, win, pad):
    tq = CB_TQ
    i = pl.program_id(1)
    off = pl.multiple_of(i * tq, tq)
    col = lax.broadcasted_iota(I32, (tq, win), 1)
    in_seq = col + off >= pad
    for hh in range(q_ref.shape[0]):
        k_w = k_ref[hh, pl.ds(off, win), :]
        v_w = v_ref[hh, pl.ds(off, win), :]
        s = lax.dot_general(q_ref[hh], k_w, (((1,), (1,)), ((), ())), preferred_element_type=F32)
        s = jnp.where(in_seq, s + b_ref[hh], NEG_BIG)
        m = jnp.max(s, axis=-1, keepdims=True)
        p = jnp.exp(s - m)
        l = jnp.sum(p, axis=-1, keepdims=True)
        o_ref[:, hh * HEAD_DIM:(hh + 1) * HEAD_DIM] = jnp.dot(
            p.astype(BF16), v_w, preferred_element_type=F32) / l


def _cb(q, k_pad, v_pad, bias, pad):
    n_heads, s, _ = q.shape
    tq = CB_TQ
    win = pad + tq
    sp = k_pad.shape[1]
    n_h = CB_HEADS if n_heads % CB_HEADS == 0 else 1
    return pl.pallas_call(
        functools.partial(_cb_kernel, win=win, pad=pad),
        grid=(n_heads // n_h, s // tq),
        in_specs=[
            pl.BlockSpec((n_h, tq, HEAD_DIM), lambda h, i: (h, i, 0)),
            pl.BlockSpec((n_h, sp, HEAD_DIM), lambda h, i: (h, 0, 0)),
            pl.BlockSpec((n_h, sp, HEAD_DIM), lambda h, i: (h, 0, 0)),
            pl.BlockSpec((n_h, tq, win), lambda h, i: (h, 0, 0)),
        ],
        out_specs=pl.BlockSpec((tq, n_h * HEAD_DIM), lambda h, i: (i, h)),
        out_shape=jax.ShapeDtypeStruct((s, n_heads * HEAD_DIM), F32),
        compiler_params=_cparams(("parallel", "parallel"),
                                 n_h * (8 * sp * HEAD_DIM * 2 + 16 * tq * win * 4)),
        name="chunk_band",
    )(q, k_pad, v_pad, bias)


def _cb_bias(rel_bias):
    tq = CB_TQ
    pad = N_LEFT_CHUNKS * CHUNK
    win = pad + tq
    n_heads = rel_bias.shape[0]
    qi = np.arange(tq)[:, None]
    kj = np.arange(win)[None, :]
    q_chunk = qi // CHUNK
    k_chunk = kj // CHUNK - N_LEFT_CHUNKS
    in_band = (k_chunk <= q_chunk) & (k_chunk >= q_chunk - N_LEFT_CHUNKS)
    p = win + tq
    n_clipped = pad + tq - 1 - REL_CLIP
    lo = 2 * REL_CLIP - (p - 1 - n_clipped)
    assert lo >= 0 and pad - (win - 1) >= -REL_CLIP
    u = jnp.concatenate([jnp.broadcast_to(rel_bias[:, 2 * REL_CLIP:], (n_heads, n_clipped)),
                         rel_bias[:, lo:2 * REL_CLIP + 1][:, ::-1]], axis=1).astype(F32)
    rows = jnp.tile(u, (1, tq + 1))[:, :tq * (p + 1)].reshape(n_heads, tq, p + 1)[:, :, :win]
    bias = rows[:, ::-1, :]
    return jnp.where(jnp.asarray(in_band)[None], bias, NEG_BIG)


def _headnorm_kernel(a_ref, b_ref, c_ref, g_ref, o_ref):
    col = 0
    for ref in (a_ref, b_ref, c_ref):
        for h in range(ref.shape[1] // HEAD_DIM):
            x = ref[:, h * HEAD_DIM:(h + 1) * HEAD_DIM]
            ms = jnp.mean(x * x, axis=-1, keepdims=True)
            y = x * lax.rsqrt(ms + EPS) * g_ref[:, col:col + HEAD_DIM]
            o_ref[:, col:col + HEAD_DIM] = y.astype(o_ref.dtype)
            col += HEAD_DIM


def _headnorm(o_a, o_b, o_c, g):
    s = o_a.shape[0]
    d = o_a.shape[1] + o_b.shape[1] + o_c.shape[1]
    ts = _largest_tile(s, V7X_SUBLANES, 256)
    return pl.pallas_call(
        _headnorm_kernel,
        grid=(s // ts,),
        in_specs=[pl.BlockSpec((ts, o_a.shape[1]), lambda i: (i, 0)),
                  pl.BlockSpec((ts, o_b.shape[1]), lambda i: (i, 0)),
                  pl.BlockSpec((ts, o_c.shape[1]), lambda i: (i, 0)),
                  pl.BlockSpec((1, d), lambda i: (0, 0))],
        out_specs=pl.BlockSpec((ts, d), lambda i: (i, 0)),
        out_shape=jax.ShapeDtypeStruct((s, d), BF16),
        compiler_params=_cparams(("parallel",), 6 * ts * d * 4),
        name="headnorm",
    )(o_a, o_b, o_c, g.reshape(1, d))


def _pack_kernel(x_ref, o_ref, *, runs):
    end = 0
    for src, dst, width in runs:
        if dst > end:
            o_ref[:, end:dst] = jnp.zeros((o_ref.shape[0], dst - end), o_ref.dtype)
        o_ref[:, dst:dst + width] = x_ref[:, src:src + width].astype(o_ref.dtype)
        end = dst + width
    if end < o_ref.shape[1]:
        o_ref[:, end:] = jnp.zeros((o_ref.shape[0], o_ref.shape[1] - end), o_ref.dtype)


def _packed_layout(sizes):
    units, src = [], 0
    for idx, width in enumerate(sizes):
        if width >= V7X_LANES or not units or units[-1][2] >= V7X_LANES:
            units.append([[idx], src, width])
        else:
            units[-1][0].append(idx)
            units[-1][2] += width
        src += width
    srcs = np.concatenate([[0], np.cumsum(sizes)[:-1]])
    offsets, runs, dst = [0] * len(sizes), [], 0
    for members, usrc, uwidth in sorted(units, key=lambda u: -u[2]):
        padded = -(-uwidth // V7X_LANES) * V7X_LANES
        dst = -(-dst // padded) * padded
        for idx in members:
            offsets[idx] = dst + int(srcs[idx]) - usrc
        runs.append((usrc, dst, uwidth))
        dst += padded
    return offsets, tuple(runs), dst


def _pack_w_in(w, l, sizes):
    offsets, runs, n_out = _packed_layout(sizes)
    _, k, n_in = w.shape
    tr = _largest_tile(k, 16, 128)
    packed = pl.pallas_call(
        functools.partial(_pack_kernel, runs=runs),
        grid=(k // tr,),
        in_specs=[pl.BlockSpec((None, tr, n_in), lambda i: (l, i, 0))],
        out_specs=pl.BlockSpec((tr, n_out), lambda i: (i, 0)),
        out_shape=jax.ShapeDtypeStruct((k, n_out), BF16),
        compiler_params=_cparams(("parallel",), 4 * tr * n_in * 4),
        name="pack_w_in",
    )(w)
    return packed.reshape(1, k, n_out), offsets


def kernel(x, c, w_ada, b_ada, norm_attn_g, w_in, rel_bias, head_norm_g, w_out, norm_ffn_g,
           w_gate_up, w_down, final_norm_g):
    b, s, d = x.shape
    assert b == 1, "single-sequence prefill only"
    depth = w_ada.shape[0]
    n_heads = d // HEAD_DIM
    ha = 3 * n_heads // 8
    hb = 3 * n_heads // 8
    hc = n_heads - ha - hb
    assert ha % N_KV_A == 0 and s % SB_T == 0
    sizes = (ha * HEAD_DIM, N_KV_A * HEAD_DIM, N_KV_A * HEAD_DIM, IDX_HEADS * IDX_DIM, IDX_DIM,
             IDX_HEADS, hb * HEAD_DIM, hb * HEAD_DIM, hb * HEAD_DIM,
             hc * HEAD_DIM, hc * HEAD_DIM, hc * HEAD_DIM)
    assert w_in.shape[2] == sum(sizes)
    topk = min(IDX_TOPK_MAX, s // 4)
    scale = HEAD_DIM ** -0.5
    pad = N_LEFT_CHUNKS * CHUNK

    rope128 = _rope_tables(s, HEAD_DIM, 1)
    rope64 = _rope_tables(s, IDX_DIM, V7X_LANES // IDX_DIM)

    xs = x.reshape(s, d)
    mod = _adaln(c, w_ada, b_ada)
    w_down_bf16 = w_down.astype(BF16)

    for l in range(depth):
        sh1, sc1, g1, sh2, sc2, g2 = [mod[l, :, n * d:(n + 1) * d] for n in range(6)]

        h = _norm(xs, norm_attn_g[l], sc1, sh1)
        w_packed, offs = _pack_w_in(w_in, l, sizes)
        o_qa, o_ka, o_va, o_iq, o_ik, o_iw, o_qb, o_kb, o_vb, o_qc, o_kc, o_vc = offs
        assert o_iw == o_ik + IDX_DIM and o_ik % V7X_LANES == 0
        proj = _matmul(h, w_packed, 0, F32)

        qa = _prep_heads(proj, o_qa, ha, rope128, HEAD_DIM, scale * math.log2(math.e))
        ka = _prep_heads(proj, o_ka, N_KV_A, rope128, HEAD_DIM)
        va = _prep_heads(proj, o_va, N_KV_A)
        vat = va.reshape(N_KV_A, s // DSA_TK, DSA_TK, HEAD_DIM).swapaxes(2, 3)
        iq = _prep_heads(proj, o_iq, IDX_HEADS * IDX_DIM // V7X_LANES, rope64, IDX_DIM,
                         head_major=False)
        ik2, iw = _prep_ikw(proj, o_ik, rope64)
        o_a = _dsa(iq, iw[:, :IDX_HEADS].T, ik2, qa, ka, vat, topk)

        qb = _prep_heads(proj, o_qb, hb, scale=scale)
        kb = _prep_heads(proj, o_kb, hb)
        vb = _prep_heads(proj, o_vb, hb)
        nb, sub = s // SB_T, V7X_SUBLANES
        kb = kb.reshape(hb, nb, sub, SB_T // sub, HEAD_DIM).swapaxes(2, 3).reshape(hb, nb, SB_T, HEAD_DIM)
        vbt = vb.reshape(hb, nb, sub, SB_T // sub, HEAD_DIM).swapaxes(2, 3).reshape(hb, nb, SB_T, HEAD_DIM)
        vbt = vbt.swapaxes(2, 3)
        o_bt = _sb(qb, kb, vbt)
        o_b = jnp.transpose(o_bt, (2, 0, 1)).reshape(s, hb * HEAD_DIM)

        qc = _prep_heads(proj, o_qc, hc, scale=scale)
        kc = _prep_heads(proj, o_kc, hc)
        vc = _prep_heads(proj, o_vc, hc)
        kc = jnp.pad(kc, ((0, 0), (pad, 0), (0, 0)))
        vc = jnp.pad(vc, ((0, 0), (pad, 0), (0, 0)))
        o_c = _cb(qc, kc, vc, _cb_bias(rel_bias[l]), pad)

        mix = _headnorm(o_a, o_b, o_c, head_norm_g[l])
        xs = _matmul_resid(mix, w_out, l, xs, g1)

        h = _norm(xs, norm_ffn_g[l], sc2, sh2)
        act = _matmul_swiglu(h, w_gate_up, l)
        xs = _matmul_resid(act, w_down_bf16, l, xs, g2)

    out = _norm(xs, final_norm_g, out_dtype=x.dtype)
    return out.reshape(b, s, d)
```

```python
import functools
import math

import numpy as np
import jax
import jax.numpy as jnp
from jax import lax
from jax.experimental import pallas as pl
from jax.experimental.pallas import tpu as pltpu

F32 = jnp.float32
BF16 = jnp.bfloat16
I32 = jnp.int32

HEAD_DIM = 128
N_KV_A = 2
IDX_HEADS = 16
IDX_DIM = 64
IDX_TOPK_MAX = 256
CHUNK = 64
N_LEFT_CHUNKS = 8
REL_CLIP = 256
ROPE_THETA = 10000.0
EPS = 1e-6

V7X_LANES = 128
V7X_SUBLANES = 8
V7X_VMEM_BYTES = 64 * 1024 * 1024
V7X_MXU_COLS = 256

PROJ_N_ALIGN = 2 * V7X_MXU_COLS

NEG_BIG = -1e30
INT_MIN = -(2 ** 31)

DSA_TQ = 128
DSA_TK = 512
SB_T = 256
SB_HEADS = 2
CB_TQ = 128
CB_HEADS = 2


def _cparams(semantics, vmem_bytes, flags=None):
    limit = int(min(V7X_VMEM_BYTES * 7 // 8, max(vmem_bytes, 16 * 1024 * 1024)))
    return pltpu.CompilerParams(dimension_semantics=semantics, vmem_limit_bytes=limit, flags=flags)


def _largest_tile(n, unit, cap):
    best = None
    t = unit
    while t <= min(n, cap):
        if n % t == 0:
            best = t
        t += unit
    assert best is not None, (n, unit, cap)
    return best


def _adaln_kernel(c_ref, w_ref, b_ref, o_ref, acc_ref):
    k = pl.program_id(1)
    c = c_ref[...]
    cond = c * jax.nn.sigmoid(c)
    tr, n = w_ref.shape

    @pl.when(k == 0)
    def _():
        acc_ref[...] = jnp.zeros_like(acc_ref)

    for j in range(n // V7X_LANES):
        cols = slice(j * V7X_LANES, (j + 1) * V7X_LANES)
        prod = w_ref[:, cols] * cond
        part = prod[0:V7X_SUBLANES]
        for g in range(1, tr // V7X_SUBLANES):
            part = part + prod[g * V7X_SUBLANES:(g + 1) * V7X_SUBLANES]
        acc_ref[:, cols] += part

    @pl.when(k == pl.num_programs(1) - 1)
    def _():
        o_ref[...] = jnp.sum(acc_ref[...], axis=0, keepdims=True) + b_ref[...]


def _adaln(c, w_ada, b_ada):
    depth, d, n = w_ada.shape
    tr = _largest_tile(d, V7X_SUBLANES, 128)
    c_b = jnp.broadcast_to(c.reshape(d, 1), (d, V7X_LANES))
    out = pl.pallas_call(
        _adaln_kernel,
        grid=(depth, d // tr),
        in_specs=[
            pl.BlockSpec((tr, V7X_LANES), lambda l, k: (k, 0)),
            pl.BlockSpec((None, tr, n), lambda l, k: (l, k, 0)),
            pl.BlockSpec((None, 1, n), lambda l, k: (l, 0, 0)),
        ],
        out_specs=pl.BlockSpec((None, 1, n), lambda l, k: (l, 0, 0)),
        out_shape=jax.ShapeDtypeStruct((depth, 1, n), F32),
        scratch_shapes=[pltpu.VMEM((V7X_SUBLANES, n), F32)],
        compiler_params=_cparams(("parallel", "arbitrary"), 3 * tr * n * 4),
        name="adaln",
    )(c_b, w_ada, b_ada.reshape(depth, 1, n))
    return out


def _norm_kernel(x_ref, g_ref, *rest, modulate):
    o_ref = rest[-1]
    x = x_ref[...]
    ms = jnp.mean(x * x, axis=-1, keepdims=True)
    y = x * lax.rsqrt(ms + EPS) * g_ref[...]
    if modulate:
        sc_ref, sh_ref = rest[0], rest[1]
        y = y * (1.0 + sc_ref[...]) + sh_ref[...]
    o_ref[...] = y.astype(o_ref.dtype)


def _norm(x, g, scale=None, shift=None, out_dtype=BF16):
    s, d = x.shape
    ts = _largest_tile(s, V7X_SUBLANES, 256)
    modulate = scale is not None
    row = pl.BlockSpec((1, d), lambda i: (0, 0))
    args = [x, g.reshape(1, d)]
    specs = [pl.BlockSpec((ts, d), lambda i: (i, 0)), row]
    if modulate:
        args += [scale.reshape(1, d), shift.reshape(1, d)]
        specs += [row, row]
    return pl.pallas_call(
        functools.partial(_norm_kernel, modulate=modulate),
        grid=(s // ts,),
        in_specs=specs,
        out_specs=pl.BlockSpec((ts, d), lambda i: (i, 0)),
        out_shape=jax.ShapeDtypeStruct((s, d), out_dtype),
        compiler_params=_cparams(("parallel",), 6 * ts * d * 4),
        name="rmsnorm",
    )(*args)


def _dot(a, w):
    return jnp.dot(a, w.astype(BF16), preferred_element_type=F32)


def _mm_plain_kernel(a_ref, w_ref, o_ref):
    o_ref[...] = _dot(a_ref[...], w_ref[...]).astype(o_ref.dtype)


def _mm_swiglu_kernel(a_ref, wg_ref, wu_ref, o_ref):
    a = a_ref[...]
    gate = _dot(a, wg_ref[...])
    up = _dot(a, wu_ref[...])
    o_ref[...] = (gate * jax.nn.sigmoid(gate) * up).astype(o_ref.dtype)


def _mm_resid_kernel(a_ref, w_ref, r_ref, g_ref, o_ref, *acc, nk):
    if nk == 1:
        o_ref[...] = r_ref[...] + g_ref[...] * _dot(a_ref[...], w_ref[...])
        return
    acc_ref, = acc
    k = pl.program_id(2)

    @pl.when(k == 0)
    def _():
        acc_ref[...] = jnp.zeros_like(acc_ref)

    acc_ref[...] += _dot(a_ref[...], w_ref[...])

    @pl.when(k == nk - 1)
    def _():
        o_ref[...] = r_ref[...] + g_ref[...] * acc_ref[...]


def _mm_tiles(m, k, n, w_itemsize):
    tn = _largest_tile(n, V7X_LANES, 768 if w_itemsize == 2 else 256)
    if k <= 4096:
        return _largest_tile(m, V7X_SUBLANES, 1024), tn, k
    if w_itemsize == 2:
        return _largest_tile(m, V7X_SUBLANES, 512), _largest_tile(n, V7X_LANES, 256), k
    return _largest_tile(m, V7X_SUBLANES, 1024), tn, _largest_tile(k, V7X_LANES, 5632)


def _mm_vmem(tm, tn, tk, w_itemsize, n_w=1):
    return (2 * tm * tk * 2 + n_w * (2 * tk * tn * w_itemsize + tk * tn * 2)
            + (4 + 2 * n_w) * tm * tn * 4)


def _matmul(a, w, l, out_dtype):
    m, k = a.shape
    n = w.shape[2]
    tm, tn, tk = _mm_tiles(m, k, n, w.dtype.itemsize)
    assert tk == k
    return pl.pallas_call(
        _mm_plain_kernel,
        grid=(m // tm, n // tn),
        in_specs=[pl.BlockSpec((tm, k), lambda i, j: (i, 0)),
                  pl.BlockSpec((None, k, tn), lambda i, j: (l, 0, j))],
        out_specs=pl.BlockSpec((tm, tn), lambda i, j: (i, j)),
        out_shape=jax.ShapeDtypeStruct((m, n), out_dtype),
        compiler_params=_cparams(("parallel", "parallel"), _mm_vmem(tm, tn, tk, w.dtype.itemsize)),
        name="matmul",
    )(a, w)


def _matmul_swiglu(a, w_gate_up, l):
    m, k = a.shape
    n = w_gate_up.shape[2] // 2
    tm, tn, tk = _mm_tiles(m, k, n, w_gate_up.dtype.itemsize)
    assert tk == k
    nb = n // tn
    return pl.pallas_call(
        _mm_swiglu_kernel,
        grid=(m // tm, nb),
        in_specs=[pl.BlockSpec((tm, k), lambda i, j: (i, 0)),
                  pl.BlockSpec((None, k, tn), lambda i, j: (l, 0, j)),
                  pl.BlockSpec((None, k, tn), lambda i, j: (l, 0, j + nb))],
        out_specs=pl.BlockSpec((tm, tn), lambda i, j: (i, j)),
        out_shape=jax.ShapeDtypeStruct((m, n), BF16),
        compiler_params=_cparams(("parallel", "parallel"),
                                 _mm_vmem(tm, tn, tk, w_gate_up.dtype.itemsize, n_w=2)),
        name="matmul_swiglu",
    )(a, w_gate_up, w_gate_up)


def _matmul_resid(a, w, l, resid, gate):
    m, k = a.shape
    n = w.shape[2]
    tm, tn, tk = _mm_tiles(m, k, n, w.dtype.itemsize)
    nk = k // tk
    scratch = [pltpu.VMEM((tm, tn), F32)] if nk > 1 else []
    return pl.pallas_call(
        functools.partial(_mm_resid_kernel, nk=nk),
        grid=(m // tm, n // tn, nk),
        in_specs=[pl.BlockSpec((tm, tk), lambda i, j, kk: (i, kk)),
                  pl.BlockSpec((None, tk, tn), lambda i, j, kk: (l, kk, j)),
                  pl.BlockSpec((tm, tn), lambda i, j, kk: (i, j)),
                  pl.BlockSpec((1, tn), lambda i, j, kk: (0, j))],
        out_specs=pl.BlockSpec((tm, tn), lambda i, j, kk: (i, j)),
        out_shape=jax.ShapeDtypeStruct((m, n), F32),
        scratch_shapes=scratch,
        compiler_params=_cparams(("parallel", "parallel", "arbitrary"),
                                 _mm_vmem(tm, tn, tk, w.dtype.itemsize) + 3 * tm * tn * 4),
        name="matmul_resid",
    )(a, w, resid, gate.reshape(1, n))


def _rope_tables(s, d, lane_groups):
    pos = jnp.arange(s, dtype=jnp.int32).astype(F32)
    inv = ROPE_THETA ** (-jnp.arange(0, d, 2, dtype=F32) / d)
    ang = pos[:, None] * inv[None, :]
    cos, sin = jnp.cos(ang), jnp.sin(ang)
    cos_t = jnp.tile(jnp.concatenate([cos, cos], axis=1), (1, lane_groups))
    sin_t = jnp.tile(jnp.concatenate([-sin, sin], axis=1), (1, lane_groups))
    return cos_t, sin_t


def _rot_half(x, d):
    if d == V7X_LANES:
        return pltpu.roll(x, d // 2, 1)
    lane = lax.broadcasted_iota(I32, x.shape, 1)
    first = (lane & (d - 1)) < d // 2
    return jnp.where(first, pltpu.roll(x, V7X_LANES - d // 2, 1), pltpu.roll(x, d // 2, 1))


def _prep_kernel(x_ref, *rest, rope_dim, scale, head_major):
    o_ref = rest[-1]
    if rope_dim:
        cos, sin = rest[0][...], rest[1][...]
    for h in range(x_ref.shape[1] // V7X_LANES):
        cols = slice(h * V7X_LANES, (h + 1) * V7X_LANES)
        x = x_ref[:, cols]
        if rope_dim:
            x = x * cos + _rot_half(x, rope_dim) * sin
        if scale != 1.0:
            x = x * scale
        if head_major:
            o_ref[h] = x.astype(o_ref.dtype)
        else:
            o_ref[:, cols] = x.astype(o_ref.dtype)


def _prep_heads(proj, col0, n_blocks, tables=None, rope_dim=0, scale=1.0, head_major=True):
    s = proj.shape[0]
    width = n_blocks * V7X_LANES
    assert col0 % width == 0 or n_blocks == 1, "segment must start at a multiple of its width"
    ts = _largest_tile(s, V7X_SUBLANES, 512)
    args = [proj]
    specs = [pl.BlockSpec((ts, width), lambda i: (i, col0 // width))]
    if rope_dim:
        args += list(tables)
        specs += [pl.BlockSpec((ts, V7X_LANES), lambda i: (i, 0))] * 2
    if head_major:
        out_spec = pl.BlockSpec((n_blocks, ts, V7X_LANES), lambda i: (0, i, 0))
        out_shape = jax.ShapeDtypeStruct((n_blocks, s, V7X_LANES), BF16)
    else:
        out_spec = pl.BlockSpec((ts, width), lambda i: (i, 0))
        out_shape = jax.ShapeDtypeStruct((s, width), BF16)
    return pl.pallas_call(
        functools.partial(_prep_kernel, rope_dim=rope_dim, scale=scale, head_major=head_major),
        grid=(s // ts,),
        in_specs=specs,
        out_specs=out_spec,
        out_shape=out_shape,
        compiler_params=_cparams(("parallel",), 8 * ts * width * 4),
        name="prep_heads",
    )(*args)


def _ikw_kernel(x_ref, cos_ref, sin_ref, ik_ref, iw_ref):
    x = x_ref[...]
    y = x * cos_ref[...] + _rot_half(x, IDX_DIM) * sin_ref[...]
    lane = lax.broadcasted_iota(I32, x.shape, 1)
    ik_ref[...] = jnp.where(lane < IDX_DIM, y, pltpu.roll(y, IDX_DIM, 1)).astype(ik_ref.dtype)
    iw_ref[...] = pltpu.roll(x, V7X_LANES - IDX_DIM, 1)


def _prep_ikw(proj, col0, tables):
    s = proj.shape[0]
    ts = _largest_tile(s, V7X_SUBLANES, 1024)
    cb = col0 // V7X_LANES
    blk = pl.BlockSpec((ts, V7X_LANES), lambda i: (i, 0))
    return pl.pallas_call(
        _ikw_kernel,
        grid=(s // ts,),
        in_specs=[pl.BlockSpec((ts, V7X_LANES), lambda i: (i, cb)), blk, blk],
        out_specs=[blk, blk],
        out_shape=[jax.ShapeDtypeStruct((s, V7X_LANES), BF16),
                   jax.ShapeDtypeStruct((s, V7X_LANES), F32)],
        compiler_params=_cparams(("parallel",), 16 * ts * V7X_LANES * 4),
        name="prep_ikw",
    )(proj, *tables)


def _dsa_kernel(iq_ref, iwt_ref, ik_ref, q_ref, k_ref, vt_ref, o_ref,
                rhs_scr, key_scr, gmax_scr, m_scr, l_scr, acc_scr, s_scr, ma_scr,
                *, topk, rep, idx_scale):
    tq, tk = DSA_TQ, DSA_TK
    nsub = tk // V7X_SUBLANES
    r_all = rep * tq
    i = pl.program_id(0)
    start = i * tq
    n_kt = (start + tq + tk - 1) >> int(math.log2(tk))

    lane = lax.broadcasted_iota(I32, (tq, V7X_LANES), 1)
    for p in range(IDX_HEADS // 2):
        pair = iq_ref[:, p * V7X_LANES:(p + 1) * V7X_LANES]
        zero = jnp.zeros_like(pair)
        rhs_scr[p, :tq] = jnp.where(lane < IDX_DIM, pair, zero)
        rhs_scr[p, tq:] = jnp.where(lane >= IDX_DIM, pair, zero)

    iwt = iwt_ref[...]
    key_row = lax.broadcasted_iota(I32, (tk, tq), 0)
    q_lane = lax.broadcasted_iota(I32, (tk, tq), 1)
    key_lim = start + ((q_lane >> int(math.log2(CHUNK))) + 1) * CHUNK

    def score_body(kt, carry):
        off = pl.multiple_of(kt * tk, tk)
        ik_t = ik_ref[pl.ds(off, tk), :]
        acc = jnp.zeros((tk, tq), F32)
        for p in range(IDX_HEADS // 2):
            d = lax.dot_general(ik_t, rhs_scr[p], (((1,), (1,)), ((), ())),
                                preferred_element_type=F32)
            for half in range(2):
                h = 2 * p + half
                acc = acc + iwt[h:h + 1, :] * jnp.maximum(d[:, half * tq:(half + 1) * tq], 0.0)
        bits = pltpu.bitcast(acc * idx_scale, I32)
        key = jnp.where(bits < 0, bits ^ 0x7FFFFFFF, bits)
        key = jnp.where(key_row + off < key_lim, key, INT_MIN)
        key_scr[kt] = key
        gmax = gmax_scr[...]
        rows = [gmax[r * V7X_SUBLANES:(r + 1) * V7X_SUBLANES] for r in range(n_grp // V7X_SUBLANES)]
        for c in range(nsub):
            r = c % len(rows)
            rows[r] = jnp.maximum(rows[r], key[c * V7X_SUBLANES:(c + 1) * V7X_SUBLANES])
        gmax_scr[...] = jnp.concatenate(rows, axis=0)
        return carry

    n_grp = IDX_TOPK_MAX
    gmax_scr[...] = jnp.full(gmax_scr.shape, INT_MIN, I32)
    lax.fori_loop(0, n_kt, score_body, 0)

    n_part = 4

    def count_ge(cand):
        def body(kt, parts):
            hit = jnp.where(key_scr[kt] >= cand, 1, 0)
            parts = list(parts)
            for c in range(nsub):
                parts[c % n_part] = parts[c % n_part] + hit[c * V7X_SUBLANES:(c + 1) * V7X_SUBLANES]
            return tuple(parts)

        parts = lax.fori_loop(0, n_kt, body,
                              tuple(jnp.zeros((V7X_SUBLANES, tq), I32) for _ in range(n_part)))
        return jnp.sum(sum(parts[1:], parts[0]), axis=0, keepdims=True)

    gmax = gmax_scr[...]
    lower = jnp.min(gmax, axis=0, keepdims=True)
    upper = jnp.max(gmax, axis=0, keepdims=True)
    n_low = jnp.max(32 - lax.clz(lower ^ upper))

    def sign_base():
        zero = jnp.zeros((1, tq), I32)
        return jnp.where(count_ge(zero) >= topk, zero, zero + INT_MIN)

    def prefix_base():
        return jnp.left_shift(jnp.right_shift(upper, n_low), n_low)

    thr = lax.cond(n_low >= 32, sign_base, prefix_base)
    n_bits = jnp.minimum(n_low, 31)

    def bisect(b, thr):
        cand = thr + lax.shift_left(jnp.int32(1), n_bits - 1 - b)
        return jnp.where(count_ge(cand) >= topk, cand, thr)

    thr = lax.fori_loop(0, n_bits, bisect, thr)
    thr = jnp.maximum(thr, INT_MIN + 1)

    m_scr[...] = jnp.full(m_scr.shape, NEG_BIG, F32)
    l_scr[...] = jnp.zeros(l_scr.shape, F32)
    acc_scr[...] = jnp.zeros(acc_scr.shape, F32)

    def stage_a(kt):
        slot = kt & 1
        off = pl.multiple_of(kt * tk, tk)
        bias = jnp.where(key_scr[kt] >= thr, 0.0, NEG_BIG)
        bias = jnp.concatenate([bias] * rep, axis=1)
        for g in range(N_KV_A):
            qg = q_ref[g * rep:(g + 1) * rep].reshape(r_all, HEAD_DIM)
            s = lax.dot_general(k_ref[g, pl.ds(off, tk), :], qg, (((1,), (1,)), ((), ())),
                                preferred_element_type=F32) + bias
            m_old = m_scr[g]
            m_new = jnp.maximum(m_old, jnp.max(s, axis=0, keepdims=True))
            s_scr[slot, g] = s
            ma_scr[slot, g, 0:1] = m_new
            ma_scr[slot, g, 1:2] = jnp.exp2(m_old - m_new)
            m_scr[g] = m_new

    def stage_b(kt):
        slot = kt & 1
        for g in range(N_KV_A):
            m_new = ma_scr[slot, g, 0:1]
            alpha = ma_scr[slot, g, 1:2]
            p = jnp.exp2(s_scr[slot, g] - m_new)
            l_scr[g] = alpha * l_scr[g] + jnp.sum(p, axis=0, keepdims=True)
            acc_scr[g] = alpha * acc_scr[g] + jnp.dot(vt_ref[g, kt], p.astype(BF16),
                                                      preferred_element_type=F32)

    stage_a(0)

    def att_body(kt, carry):
        stage_b(kt - 1)
        stage_a(kt)
        return carry

    lax.fori_loop(1, n_kt, att_body, 0)
    stage_b(n_kt - 1)
    for g in range(N_KV_A):
        out_t = acc_scr[g] / l_scr[g]
        for r in range(rep):
            hh = g * rep + r
            o_ref[:, hh * HEAD_DIM:(hh + 1) * HEAD_DIM] = out_t[:, r * tq:(r + 1) * tq].T


def _dsa(iq, iwt, ik2, qa, ka, vat, topk):
    n_heads, s, _ = qa.shape
    rep = n_heads // N_KV_A
    tq, tk = DSA_TQ, DSA_TK
    assert s % tk == 0
    idx_scale = (IDX_HEADS ** -0.5) * (IDX_DIM ** -0.5)
    vmem = (4 * N_KV_A * s * HEAD_DIM * 2 * 2 + 2 * s * V7X_LANES * 2 + tq * s * 4
            + 32 * rep * tq * tk * 4)
    return pl.pallas_call(
        functools.partial(_dsa_kernel, topk=topk, rep=rep, idx_scale=idx_scale),
        grid=(s // tq,),
        in_specs=[
            pl.BlockSpec((tq, IDX_HEADS * IDX_DIM), lambda i: (i, 0)),
            pl.BlockSpec((IDX_HEADS, tq), lambda i: (0, i)),
            pl.BlockSpec((s, V7X_LANES), lambda i: (0, 0)),
            pl.BlockSpec((n_heads, tq, HEAD_DIM), lambda i: (0, i, 0)),
            pl.BlockSpec((N_KV_A, s, HEAD_DIM), lambda i: (0, 0, 0)),
            pl.BlockSpec((N_KV_A, s // tk, HEAD_DIM, tk), lambda i: (0, 0, 0, 0)),
        ],
        out_specs=pl.BlockSpec((tq, n_heads * HEAD_DIM), lambda i: (i, 0)),
        out_shape=jax.ShapeDtypeStruct((s, n_heads * HEAD_DIM), F32),
        scratch_shapes=[
            pltpu.VMEM((IDX_HEADS // 2, 2 * tq, V7X_LANES), BF16),
            pltpu.VMEM((s // tk, tk, tq), I32),
            pltpu.VMEM((IDX_TOPK_MAX, tq), I32),
            pltpu.VMEM((N_KV_A, 1, rep * tq), F32),
            pltpu.VMEM((N_KV_A, 1, rep * tq), F32),
            pltpu.VMEM((N_KV_A, HEAD_DIM, rep * tq), F32),
            pltpu.VMEM((2, N_KV_A, tk, rep * tq), F32),
            pltpu.VMEM((2, N_KV_A, 2, rep * tq), F32),
        ],
        compiler_params=_cparams(("parallel",), vmem),
        name="dsa",
    )(iq, iwt, ik2, qa, ka, vat)


def _sb_kernel(q_ref, k_ref, vt_ref, o_ref):
    t = SB_T
    nrg = t // V7X_SUBLANES
    n_h = q_ref.shape[0]
    i = pl.program_id(1)
    o_ref[...] = jnp.zeros(o_ref.shape, F32)

    def local_scan(hh, j, diagonal):
        z = lax.dot_general(k_ref[hh, j], q_ref[hh], (((1,), (1,)), ((), ())),
                            preferred_element_type=F32)
        e = jnp.exp(-jnp.abs(z))
        r = 1.0 / (1.0 + e)
        er = e * r
        pos = z >= 0.0
        beta = jnp.where(pos, r, er)
        keep = jnp.where(pos, er, r)
        if diagonal:
            rr = lax.broadcasted_iota(I32, (t, t), 0)
            tt = lax.broadcasted_iota(I32, (t, t), 1)
            key_pos = (rr & (V7X_SUBLANES - 1)) * nrg + (rr >> 3)
            causal = key_pos < tt
            beta = jnp.where(causal, beta, 0.0)
            keep = jnp.where(causal, keep, 1.0)
        run8 = jnp.ones((V7X_SUBLANES, t), F32)
        a_loc = [None] * nrg
        for g in range(nrg - 1, -1, -1):
            rows = slice(g * V7X_SUBLANES, (g + 1) * V7X_SUBLANES)
            a_loc[g] = beta[rows] * run8
            run8 = run8 * keep[rows]
        return a_loc, run8

    def finish(hh, j, scan, carry):
        a_loc, run8 = scan
        run = carry
        offs = [None] * V7X_SUBLANES
        for u in range(V7X_SUBLANES - 1, -1, -1):
            offs[u] = run
            run = run * run8[u:u + 1]
        off8 = jnp.concatenate(offs, axis=0)
        a = jnp.concatenate([al * off8 for al in a_loc], axis=0)
        o_ref[hh] += jnp.dot(vt_ref[hh, j], a.astype(BF16), preferred_element_type=F32)
        return run

    j_prev = jnp.maximum(i - 1, 0)
    has_prev = (i > 0).astype(F32)
    carries = []
    for hh in range(n_h):
        scan_diag = local_scan(hh, i, True)
        scan_prev = local_scan(hh, j_prev, False)
        carry = finish(hh, i, scan_diag, jnp.ones((1, t), F32)) * has_prev
        carries.append(finish(hh, j_prev, scan_prev, carry))

    def cond(state):
        j, cs = state
        alive = jnp.max(functools.reduce(jnp.maximum, cs)) > 0.0
        return jnp.logical_and(j >= 0, alive)

    def body(state):
        j, cs = state
        return j - 1, tuple(finish(hh, j, local_scan(hh, j, False), cs[hh]) for hh in range(n_h))

    lax.while_loop(cond, body, (i - 2, tuple(carries)))


def _sb(q, k_perm, v_t):
    n_heads, s, _ = q.shape
    t = SB_T
    nb = s // t
    n_h = SB_HEADS if n_heads % SB_HEADS == 0 else 1
    return pl.pallas_call(
        _sb_kernel,
        grid=(n_heads // n_h, nb),
        in_specs=[
            pl.BlockSpec((n_h, t, HEAD_DIM), lambda h, i: (h, i, 0)),
            pl.BlockSpec((n_h, nb, t, HEAD_DIM), lambda h, i: (h, 0, 0, 0)),
            pl.BlockSpec((n_h, nb, HEAD_DIM, t), lambda h, i: (h, 0, 0, 0)),
        ],
        out_specs=pl.BlockSpec((n_h, HEAD_DIM, t), lambda h, i: (h, 0, i)),
        out_shape=jax.ShapeDtypeStruct((n_heads, HEAD_DIM, s), F32),
        compiler_params=_cparams(("parallel", "parallel"),
                                 n_h * (8 * s * HEAD_DIM * 2 + 40 * t * t * 4)),
        name="stick_breaking",
    )(q, k_perm, v_t)


def _cb_kernel(q_ref, k_ref, v_ref, b_ref, o_ref, *, win, pad):
    tq = CB_TQ
    i = pl.program_id(1)
    off = pl.multiple_of(i * tq, tq)
    col = lax.broadcasted_iota(I32, (tq, win), 1)
    in_seq = col + off >= pad
    for hh in range(q_ref.shape[0]):
        k_w = k_ref[hh, pl.ds(off, win), :]
        v_w = v_ref[hh, pl.ds(off, win), :]
        s = lax.dot_general(q_ref[hh], k_w, (((1,), (1,)), ((), ())), preferred_element_type=F32)
        s = jnp.where(in_seq, s + b_ref[hh], NEG_BIG)
        m = jnp.max(s, axis=-1, keepdims=True)
        p = jnp.exp(s - m)
        l = jnp.sum(p, axis=-1, keepdims=True)
        o_ref[:, hh * HEAD_DIM:(hh + 1) * HEAD_DIM] = jnp.dot(
            p.astype(BF16), v_w, preferred_element_type=F32) / l


def _cb(q, k_pad, v_pad, bias, pad):
    n_heads, s, _ = q.shape
    tq = CB_TQ
    win = pad + tq
    sp = k_pad.shape[1]
    n_h = CB_HEADS if n_heads % CB_HEADS == 0 else 1
    return pl.pallas_call(
        functools.partial(_cb_kernel, win=win, pad=pad),
        grid=(n_heads // n_h, s // tq),
        in_specs=[
            pl.BlockSpec((n_h, tq, HEAD_DIM), lambda h, i: (h, i, 0)),
            pl.BlockSpec((n_h, sp, HEAD_DIM), lambda h, i: (h, 0, 0)),
            pl.BlockSpec((n_h, sp, HEAD_DIM), lambda h, i: (h, 0, 0)),
            pl.BlockSpec((n_h, tq, win), lambda h, i: (h, 0, 0)),
        ],
        out_specs=pl.BlockSpec((tq, n_h * HEAD_DIM), lambda h, i: (i, h)),
        out_shape=jax.ShapeDtypeStruct((s, n_heads * HEAD_DIM), F32),
        compiler_params=_cparams(("parallel", "parallel"),
                                 n_h * (8 * sp * HEAD_DIM * 2 + 16 * tq * win * 4)),
        name="chunk_band",
    )(q, k_pad, v_pad, bias)


def _cb_bias(rel_bias):
    tq = CB_TQ
    pad = N_LEFT_CHUNKS * CHUNK
    win = pad + tq
    n_heads = rel_bias.shape[0]
    qi = np.arange(tq)[:, None]
    kj = np.arange(win)[None, :]
    q_chunk = qi // CHUNK
    k_chunk = kj // CHUNK - N_LEFT_CHUNKS
    in_band = (k_chunk <= q_chunk) & (k_chunk >= q_chunk - N_LEFT_CHUNKS)
    p = win + tq
    n_clipped = pad + tq - 1 - REL_CLIP
    lo = 2 * REL_CLIP - (p - 1 - n_clipped)
    assert lo >= 0 and pad - (win - 1) >= -REL_CLIP
    u = jnp.concatenate([jnp.broadcast_to(rel_bias[:, 2 * REL_CLIP:], (n_heads, n_clipped)),
                         rel_bias[:, lo:2 * REL_CLIP + 1][:, ::-1]], axis=1).astype(F32)
    rows = jnp.tile(u, (1, tq + 1))[:, :tq * (p + 1)].reshape(n_heads, tq, p + 1)[:, :, :win]
    bias = rows[:, ::-1, :]
    return jnp.where(jnp.asarray(in_band)[None], bias, NEG_BIG)


def _headnorm_kernel(a_ref, b_ref, c_ref, g_ref, o_ref):
    col = 0
    for ref in (a_ref, b_ref, c_ref):
        for h in range(ref.shape[1] // HEAD_DIM):
            x = ref[:, h * HEAD_DIM:(h + 1) * HEAD_DIM]
            ms = jnp.mean(x * x, axis=-1, keepdims=True)
            y = x * lax.rsqrt(ms + EPS) * g_ref[:, col:col + HEAD_DIM]
            o_ref[:, col:col + HEAD_DIM] = y.astype(o_ref.dtype)
            col += HEAD_DIM


def _headnorm(o_a, o_b, o_c, g):
    s = o_a.shape[0]
    d = o_a.shape[1] + o_b.shape[1] + o_c.shape[1]
    ts = _largest_tile(s, V7X_SUBLANES, 256)
    return pl.pallas_call(
        _headnorm_kernel,
        grid=(s // ts,),
        in_specs=[pl.BlockSpec((ts, o_a.shape[1]), lambda i: (i, 0)),
                  pl.BlockSpec((ts, o_b.shape[1]), lambda i: (i, 0)),
                  pl.BlockSpec((ts, o_c.shape[1]), lambda i: (i, 0)),
                  pl.BlockSpec((1, d), lambda i: (0, 0))],
        out_specs=pl.BlockSpec((ts, d), lambda i: (i, 0)),
        out_shape=jax.ShapeDtypeStruct((s, d), BF16),
        compiler_params=_cparams(("parallel",), 6 * ts * d * 4),
        name="headnorm",
    )(o_a, o_b, o_c, g.reshape(1, d))


def _pack_kernel(x_ref, o_ref, *, runs):
    end = 0
    for src, dst, width in runs:
        if dst > end:
            o_ref[:, end:dst] = jnp.zeros((o_ref.shape[0], dst - end), o_ref.dtype)
        o_ref[:, dst:dst + width] = x_ref[:, src:src + width].astype(o_ref.dtype)
        end = dst + width
    if end < o_ref.shape[1]:
        o_ref[:, end:] = jnp.zeros((o_ref.shape[0], o_ref.shape[1] - end), o_ref.dtype)


def _cast_kernel(x_ref, o_ref):
    o_ref[...] = x_ref[...].astype(o_ref.dtype)


def _cast_layer(w, l, dtype):
    _, k, n = w.shape
    tr = _largest_tile(k, 16, 256)
    out = pl.pallas_call(
        _cast_kernel,
        grid=(k // tr,),
        in_specs=[pl.BlockSpec((None, tr, n), lambda i: (l, i, 0))],
        out_specs=pl.BlockSpec((tr, n), lambda i: (i, 0)),
        out_shape=jax.ShapeDtypeStruct((k, n), dtype),
        compiler_params=_cparams(("parallel",), 4 * tr * n * 4),
        name="cast_layer",
    )(w)
    return out.reshape(1, k, n)


def _packed_layout(sizes):
    units, src = [], 0
    for idx, width in enumerate(sizes):
        if width >= V7X_LANES or not units or units[-1][2] >= V7X_LANES:
            units.append([[idx], src, width])
        else:
            units[-1][0].append(idx)
            units[-1][2] += width
        src += width
    srcs = np.concatenate([[0], np.cumsum(sizes)[:-1]])
    offsets, runs, dst = [0] * len(sizes), [], 0
    for members, usrc, uwidth in sorted(units, key=lambda u: -u[2]):
        padded = -(-uwidth // V7X_LANES) * V7X_LANES
        dst = -(-dst // padded) * padded
        for idx in members:
            offsets[idx] = dst + int(srcs[idx]) - usrc
        runs.append((usrc, dst, uwidth))
        dst += padded
    return offsets, tuple(runs), -(-dst // PROJ_N_ALIGN) * PROJ_N_ALIGN


def _pack_w_in(w, l, sizes):
    offsets, runs, n_out = _packed_layout(sizes)
    _, k, n_in = w.shape
    tr = _largest_tile(k, 16, 128)
    packed = pl.pallas_call(
        functools.partial(_pack_kernel, runs=runs),
        grid=(k // tr,),
        in_specs=[pl.BlockSpec((None, tr, n_in), lambda i: (l, i, 0))],
        out_specs=pl.BlockSpec((tr, n_out), lambda i: (i, 0)),
        out_shape=jax.ShapeDtypeStruct((k, n_out), BF16),
        compiler_params=_cparams(("parallel",), 4 * tr * n_in * 4),
        name="pack_w_in",
    )(w)
    return packed.reshape(1, k, n_out), offsets


def kernel(x, c, w_ada, b_ada, norm_attn_g, w_in, rel_bias, head_norm_g, w_out, norm_ffn_g,
           w_gate_up, w_down, final_norm_g):
    b, s, d = x.shape
    assert b == 1, "single-sequence prefill only"
    depth = w_ada.shape[0]
    n_heads = d // HEAD_DIM
    ha = 3 * n_heads // 8
    hb = 3 * n_heads // 8
    hc = n_heads - ha - hb
    assert ha % N_KV_A == 0 and s % SB_T == 0
    sizes = (ha * HEAD_DIM, N_KV_A * HEAD_DIM, N_KV_A * HEAD_DIM, IDX_HEADS * IDX_DIM, IDX_DIM,
             IDX_HEADS, hb * HEAD_DIM, hb * HEAD_DIM, hb * HEAD_DIM,
             hc * HEAD_DIM, hc * HEAD_DIM, hc * HEAD_DIM)
    assert w_in.shape[2] == sum(sizes)
    topk = min(IDX_TOPK_MAX, s // 4)
    scale = HEAD_DIM ** -0.5
    pad = N_LEFT_CHUNKS * CHUNK

    rope128 = _rope_tables(s, HEAD_DIM, 1)
    rope64 = _rope_tables(s, IDX_DIM, V7X_LANES // IDX_DIM)

    xs = x.reshape(s, d)
    mod = _adaln(c, w_ada, b_ada)

    for l in range(depth):
        sh1, sc1, g1, sh2, sc2, g2 = [mod[l, :, n * d:(n + 1) * d] for n in range(6)]

        h = _norm(xs, norm_attn_g[l], sc1, sh1)
        w_packed, offs = _pack_w_in(w_in, l, sizes)
        o_qa, o_ka, o_va, o_iq, o_ik, o_iw, o_qb, o_kb, o_vb, o_qc, o_kc, o_vc = offs
        assert o_iw == o_ik + IDX_DIM and o_ik % V7X_LANES == 0
        proj = _matmul(h, w_packed, 0, F32)

        qa = _prep_heads(proj, o_qa, ha, rope128, HEAD_DIM, scale * math.log2(math.e))
        ka = _prep_heads(proj, o_ka, N_KV_A, rope128, HEAD_DIM)
        va = _prep_heads(proj, o_va, N_KV_A)
        vat = va.reshape(N_KV_A, s // DSA_TK, DSA_TK, HEAD_DIM).swapaxes(2, 3)
        iq = _prep_heads(proj, o_iq, IDX_HEADS * IDX_DIM // V7X_LANES, rope64, IDX_DIM,
                         head_major=False)
        ik2, iw = _prep_ikw(proj, o_ik, rope64)
        o_a = _dsa(iq, iw[:, :IDX_HEADS].T, ik2, qa, ka, vat, topk)

        qb = _prep_heads(proj, o_qb, hb, scale=scale)
        kb = _prep_heads(proj, o_kb, hb)
        vb = _prep_heads(proj, o_vb, hb)
        nb, sub = s // SB_T, V7X_SUBLANES
        kb = kb.reshape(hb, nb, sub, SB_T // sub, HEAD_DIM).swapaxes(2, 3).reshape(hb, nb, SB_T, HEAD_DIM)
        vbt = vb.reshape(hb, nb, sub, SB_T // sub, HEAD_DIM).swapaxes(2, 3).reshape(hb, nb, SB_T, HEAD_DIM)
        vbt = vbt.swapaxes(2, 3)
        o_bt = _sb(qb, kb, vbt)
        o_b = jnp.transpose(o_bt, (2, 0, 1)).reshape(s, hb * HEAD_DIM)

        qc = _prep_heads(proj, o_qc, hc, scale=scale)
        kc = _prep_heads(proj, o_kc, hc)
        vc = _prep_heads(proj, o_vc, hc)
        kc = jnp.pad(kc, ((0, 0), (pad, 0), (0, 0)))
        vc = jnp.pad(vc, ((0, 0), (pad, 0), (0, 0)))
        o_c = _cb(qc, kc, vc, _cb_bias(rel_bias[l]), pad)

        mix = _headnorm(o_a, o_b, o_c, head_norm_g[l])
        xs = _matmul_resid(mix, w_out, l, xs, g1)

        h = _norm(xs, norm_ffn_g[l], sc2, sh2)
        act = _matmul_swiglu(h, w_gate_up, l)
        xs = _matmul_resid(act, _cast_layer(w_down, l, BF16), 0, xs, g2)

    out = _norm(xs, final_norm_g, out_dtype=x.dtype)
    return out.reshape(b, s, d)
```

```python
import functools
import math

import numpy as np
import jax
import jax.numpy as jnp
from jax import lax
from jax.experimental import pallas as pl
from jax.experimental.pallas import tpu as pltpu

F32 = jnp.float32
BF16 = jnp.bfloat16
I32 = jnp.int32

HEAD_DIM = 128
N_KV_A = 2
IDX_HEADS = 16
IDX_DIM = 64
IDX_TOPK_MAX = 256
CHUNK = 64
N_LEFT_CHUNKS = 8
REL_CLIP = 256
ROPE_THETA = 10000.0
EPS = 1e-6

V7X_LANES = 128
V7X_SUBLANES = 8
V7X_VMEM_BYTES = 64 * 1024 * 1024
V7X_MXU_COLS = 256

PROJ_N_ALIGN = 2 * V7X_MXU_COLS

NEG_BIG = -1e30
INT_MIN = -(2 ** 31)

DSA_TQ = 128
DSA_TK = 512
SB_T = 256
SB_HEADS = 2
CB_TQ = 128
CB_HEADS = 4


def _cparams(semantics, vmem_bytes, flags=None):
    limit = int(min(V7X_VMEM_BYTES * 7 // 8, max(vmem_bytes, 16 * 1024 * 1024)))
    return pltpu.CompilerParams(dimension_semantics=semantics, vmem_limit_bytes=limit, flags=flags)


def _largest_tile(n, unit, cap):
    best = None
    t = unit
    while t <= min(n, cap):
        if n % t == 0:
            best = t
        t += unit
    assert best is not None, (n, unit, cap)
    return best


def _adaln_kernel(c_ref, w_ref, b_ref, o_ref, acc_ref):
    k = pl.program_id(1)
    c = c_ref[...]
    cond = c * jax.nn.sigmoid(c)
    tr, n = w_ref.shape

    @pl.when(k == 0)
    def _():
        acc_ref[...] = jnp.zeros_like(acc_ref)

    for j in range(n // V7X_LANES):
        cols = slice(j * V7X_LANES, (j + 1) * V7X_LANES)
        prod = w_ref[:, cols] * cond
        part = prod[0:V7X_SUBLANES]
        for g in range(1, tr // V7X_SUBLANES):
            part = part + prod[g * V7X_SUBLANES:(g + 1) * V7X_SUBLANES]
        acc_ref[:, cols] += part

    @pl.when(k == pl.num_programs(1) - 1)
    def _():
        o_ref[...] = jnp.sum(acc_ref[...], axis=0, keepdims=True) + b_ref[...]


def _adaln(c, w_ada, b_ada):
    depth, d, n = w_ada.shape
    tr = _largest_tile(d, V7X_SUBLANES, 128)
    c_b = jnp.broadcast_to(c.reshape(d, 1), (d, V7X_LANES))
    out = pl.pallas_call(
        _adaln_kernel,
        grid=(depth, d // tr),
        in_specs=[
            pl.BlockSpec((tr, V7X_LANES), lambda l, k: (k, 0)),
            pl.BlockSpec((None, tr, n), lambda l, k: (l, k, 0)),
            pl.BlockSpec((None, 1, n), lambda l, k: (l, 0, 0)),
        ],
        out_specs=pl.BlockSpec((None, 1, n), lambda l, k: (l, 0, 0)),
        out_shape=jax.ShapeDtypeStruct((depth, 1, n), F32),
        scratch_shapes=[pltpu.VMEM((V7X_SUBLANES, n), F32)],
        compiler_params=_cparams(("parallel", "arbitrary"), 3 * tr * n * 4),
        name="adaln",
    )(c_b, w_ada, b_ada.reshape(depth, 1, n))
    return out


def _norm_kernel(x_ref, g_ref, *rest, modulate):
    o_ref = rest[-1]
    x = x_ref[...]
    ms = jnp.mean(x * x, axis=-1, keepdims=True)
    y = x * lax.rsqrt(ms + EPS) * g_ref[...]
    if modulate:
        sc_ref, sh_ref = rest[0], rest[1]
        y = y * (1.0 + sc_ref[...]) + sh_ref[...]
    o_ref[...] = y.astype(o_ref.dtype)


def _norm(x, g, scale=None, shift=None, out_dtype=BF16):
    s, d = x.shape
    ts = _largest_tile(s, V7X_SUBLANES, 256)
    modulate = scale is not None
    row = pl.BlockSpec((1, d), lambda i: (0, 0))
    args = [x, g.reshape(1, d)]
    specs = [pl.BlockSpec((ts, d), lambda i: (i, 0)), row]
    if modulate:
        args += [scale.reshape(1, d), shift.reshape(1, d)]
        specs += [row, row]
    return pl.pallas_call(
        functools.partial(_norm_kernel, modulate=modulate),
        grid=(s // ts,),
        in_specs=specs,
        out_specs=pl.BlockSpec((ts, d), lambda i: (i, 0)),
        out_shape=jax.ShapeDtypeStruct((s, d), out_dtype),
        compiler_params=_cparams(("parallel",), 6 * ts * d * 4),
        name="rmsnorm",
    )(*args)


def _dot(a, w):
    return jnp.dot(a, w.astype(BF16), preferred_element_type=F32)


def _mm_plain_kernel(a_ref, wt_ref, o_ref):
    o_ref[...] = lax.dot_general(a_ref[...], wt_ref[...].astype(BF16), (((1,), (1,)), ((), ())),
                                 preferred_element_type=F32).astype(o_ref.dtype)


def _mm_swiglu_kernel(a_ref, wg_ref, wu_ref, o_ref):
    a = a_ref[...]
    gate = _dot(a, wg_ref[...])
    up = _dot(a, wu_ref[...])
    o_ref[...] = (gate * jax.nn.sigmoid(gate) * up).astype(o_ref.dtype)


def _mm_resid_kernel(a_ref, w_ref, r_ref, g_ref, o_ref, *acc, nk):
    if nk == 1:
        o_ref[...] = r_ref[...] + g_ref[...] * _dot(a_ref[...], w_ref[...])
        return
    acc_ref, = acc
    k = pl.program_id(2)

    @pl.when(k == 0)
    def _():
        acc_ref[...] = jnp.zeros_like(acc_ref)

    acc_ref[...] += _dot(a_ref[...], w_ref[...])

    @pl.when(k == nk - 1)
    def _():
        o_ref[...] = r_ref[...] + g_ref[...] * acc_ref[...]


def _mm_tiles(m, k, n, w_itemsize):
    tn = _largest_tile(n, V7X_LANES, 768 if w_itemsize == 2 else 256)
    if k <= 4096:
        return _largest_tile(m, V7X_SUBLANES, 1024), tn, k
    if w_itemsize == 2:
        return _largest_tile(m, V7X_SUBLANES, 512), _largest_tile(n, V7X_LANES, 256), k
    return _largest_tile(m, V7X_SUBLANES, 1024), tn, _largest_tile(k, V7X_LANES, 5632)


def _mm_vmem(tm, tn, tk, w_itemsize, n_w=1):
    return (2 * tm * tk * 2 + n_w * (2 * tk * tn * w_itemsize + tk * tn * 2)
            + (4 + 2 * n_w) * tm * tn * 4)


def _matmul_nt(a, w_t, l, out_dtype):
    m, k = a.shape
    n = w_t.shape[1]
    tm, tn, tk = _mm_tiles(m, k, n, w_t.dtype.itemsize)
    assert tk == k
    return pl.pallas_call(
        _mm_plain_kernel,
        grid=(m // tm, n // tn),
        in_specs=[pl.BlockSpec((tm, k), lambda i, j: (i, 0)),
                  pl.BlockSpec((None, tn, k), lambda i, j: (l, j, 0))],
        out_specs=pl.BlockSpec((tm, tn), lambda i, j: (i, j)),
        out_shape=jax.ShapeDtypeStruct((m, n), out_dtype),
        compiler_params=_cparams(("parallel", "parallel"), _mm_vmem(tm, tn, tk, w_t.dtype.itemsize)),
        name="matmul",
    )(a, w_t)


def _matmul_swiglu(a, w_gate_up, l):
    m, k = a.shape
    n = w_gate_up.shape[2] // 2
    tm, tn, tk = _mm_tiles(m, k, n, w_gate_up.dtype.itemsize)
    assert tk == k
    nb = n // tn
    return pl.pallas_call(
        _mm_swiglu_kernel,
        grid=(m // tm, nb),
        in_specs=[pl.BlockSpec((tm, k), lambda i, j: (i, 0)),
                  pl.BlockSpec((None, k, tn), lambda i, j: (l, 0, j)),
                  pl.BlockSpec((None, k, tn), lambda i, j: (l, 0, j + nb))],
        out_specs=pl.BlockSpec((tm, tn), lambda i, j: (i, j)),
        out_shape=jax.ShapeDtypeStruct((m, n), BF16),
        compiler_params=_cparams(("parallel", "parallel"),
                                 _mm_vmem(tm, tn, tk, w_gate_up.dtype.itemsize, n_w=2)),
        name="matmul_swiglu",
    )(a, w_gate_up, w_gate_up)


def _matmul_resid(a, w, l, resid, gate):
    m, k = a.shape
    n = w.shape[2]
    tm, tn, tk = _mm_tiles(m, k, n, w.dtype.itemsize)
    nk = k // tk
    scratch = [pltpu.VMEM((tm, tn), F32)] if nk > 1 else []
    return pl.pallas_call(
        functools.partial(_mm_resid_kernel, nk=nk),
        grid=(m // tm, n // tn, nk),
        in_specs=[pl.BlockSpec((tm, tk), lambda i, j, kk: (i, kk)),
                  pl.BlockSpec((None, tk, tn), lambda i, j, kk: (l, kk, j)),
                  pl.BlockSpec((tm, tn), lambda i, j, kk: (i, j)),
                  pl.BlockSpec((1, tn), lambda i, j, kk: (0, j))],
        out_specs=pl.BlockSpec((tm, tn), lambda i, j, kk: (i, j)),
        out_shape=jax.ShapeDtypeStruct((m, n), F32),
        scratch_shapes=scratch,
        compiler_params=_cparams(("parallel", "parallel", "arbitrary"),
                                 _mm_vmem(tm, tn, tk, w.dtype.itemsize) + 3 * tm * tn * 4),
        name="matmul_resid",
    )(a, w, resid, gate.reshape(1, n))


def _rope_tables(s, d, lane_groups):
    pos = jnp.arange(s, dtype=jnp.int32).astype(F32)
    inv = ROPE_THETA ** (-jnp.arange(0, d, 2, dtype=F32) / d)
    ang = pos[:, None] * inv[None, :]
    cos, sin = jnp.cos(ang), jnp.sin(ang)
    cos_t = jnp.tile(jnp.concatenate([cos, cos], axis=1), (1, lane_groups))
    sin_t = jnp.tile(jnp.concatenate([-sin, sin], axis=1), (1, lane_groups))
    return cos_t, sin_t


def _rot_half(x, d):
    if d == V7X_LANES:
        return pltpu.roll(x, d // 2, 1)
    lane = lax.broadcasted_iota(I32, x.shape, 1)
    first = (lane & (d - 1)) < d // 2
    return jnp.where(first, pltpu.roll(x, V7X_LANES - d // 2, 1), pltpu.roll(x, d // 2, 1))


def _prep_kernel(x_ref, *rest, rope_dim, scale, head_major):
    o_ref = rest[-1]
    if rope_dim:
        cos, sin = rest[0][...], rest[1][...]
    for h in range(x_ref.shape[1] // V7X_LANES):
        cols = slice(h * V7X_LANES, (h + 1) * V7X_LANES)
        x = x_ref[:, cols]
        if rope_dim:
            x = x * cos + _rot_half(x, rope_dim) * sin
        if scale != 1.0:
            x = x * scale
        if head_major:
            o_ref[h] = x.astype(o_ref.dtype)
        else:
            o_ref[:, cols] = x.astype(o_ref.dtype)


def _prep_heads(proj, col0, n_blocks, tables=None, rope_dim=0, scale=1.0, head_major=True):
    s = proj.shape[0]
    width = n_blocks * V7X_LANES
    assert col0 % width == 0 or n_blocks == 1, "segment must start at a multiple of its width"
    ts = _largest_tile(s, V7X_SUBLANES, 512)
    args = [proj]
    specs = [pl.BlockSpec((ts, width), lambda i: (i, col0 // width))]
    if rope_dim:
        args += list(tables)
        specs += [pl.BlockSpec((ts, V7X_LANES), lambda i: (i, 0))] * 2
    if head_major:
        out_spec = pl.BlockSpec((n_blocks, ts, V7X_LANES), lambda i: (0, i, 0))
        out_shape = jax.ShapeDtypeStruct((n_blocks, s, V7X_LANES), BF16)
    else:
        out_spec = pl.BlockSpec((ts, width), lambda i: (i, 0))
        out_shape = jax.ShapeDtypeStruct((s, width), BF16)
    return pl.pallas_call(
        functools.partial(_prep_kernel, rope_dim=rope_dim, scale=scale, head_major=head_major),
        grid=(s // ts,),
        in_specs=specs,
        out_specs=out_spec,
        out_shape=out_shape,
        compiler_params=_cparams(("parallel",), 8 * ts * width * 4),
        name="prep_heads",
    )(*args)


def _ikw_kernel(x_ref, cos_ref, sin_ref, ik_ref, iw_ref):
    x = x_ref[...]
    y = x * cos_ref[...] + _rot_half(x, IDX_DIM) * sin_ref[...]
    lane = lax.broadcasted_iota(I32, x.shape, 1)
    ik_ref[...] = jnp.where(lane < IDX_DIM, y, pltpu.roll(y, IDX_DIM, 1)).astype(ik_ref.dtype)
    iw_ref[...] = pltpu.roll(x, V7X_LANES - IDX_DIM, 1)


def _prep_ikw(proj, col0, tables):
    s = proj.shape[0]
    ts = _largest_tile(s, V7X_SUBLANES, 1024)
    cb = col0 // V7X_LANES
    blk = pl.BlockSpec((ts, V7X_LANES), lambda i: (i, 0))
    return pl.pallas_call(
        _ikw_kernel,
        grid=(s // ts,),
        in_specs=[pl.BlockSpec((ts, V7X_LANES), lambda i: (i, cb)), blk, blk],
        out_specs=[blk, blk],
        out_shape=[jax.ShapeDtypeStruct((s, V7X_LANES), BF16),
                   jax.ShapeDtypeStruct((s, V7X_LANES), F32)],
        compiler_params=_cparams(("parallel",), 16 * ts * V7X_LANES * 4),
        name="prep_ikw",
    )(proj, *tables)


def _head_rms_t(x_t, g_t):
    ms = jnp.mean(x_t * x_t, axis=0, keepdims=True)
    y = x_t * lax.rsqrt(ms + EPS)
    reps = x_t.shape[1] // g_t.shape[1]
    return y * (g_t if reps == 1 else jnp.concatenate([g_t] * reps, axis=1))


def _dsa_kernel(iq_ref, iwt_ref, ik_ref, q_ref, k_ref, vt_ref, gt_ref, o_ref,
                rhs_scr, key_scr, m_scr, l_scr, acc_scr, s_scr, ma_scr,
                *, topk, rep, idx_scale, n_total):
    tq, tk = DSA_TQ, DSA_TK
    nsub = tk // V7X_SUBLANES
    r_all = rep * tq
    i = pl.program_id(0)
    start = i * tq
    n_kt = (start + tq + tk - 1) >> int(math.log2(tk))

    lane = lax.broadcasted_iota(I32, (tq, V7X_LANES), 1)
    for p in range(IDX_HEADS // 2):
        pair = iq_ref[:, p * V7X_LANES:(p + 1) * V7X_LANES]
        zero = jnp.zeros_like(pair)
        rhs_scr[p, :tq] = jnp.where(lane < IDX_DIM, pair, zero)
        rhs_scr[p, tq:] = jnp.where(lane >= IDX_DIM, pair, zero)

    iwt = iwt_ref[...]
    key_row = lax.broadcasted_iota(I32, (tk, tq), 0)
    q_lane = lax.broadcasted_iota(I32, (tk, tq), 1)
    key_lim = start + ((q_lane >> int(math.log2(CHUNK))) + 1) * CHUNK

    def score_body(kt, carry):
        off = pl.multiple_of(kt * tk, tk)
        ik_t = ik_ref[pl.ds(off, tk), :]
        acc = jnp.zeros((tk, tq), F32)
        for p in range(IDX_HEADS // 2):
            d = lax.dot_general(ik_t, rhs_scr[p], (((1,), (1,)), ((), ())),
                                preferred_element_type=F32)
            for half in range(2):
                h = 2 * p + half
                acc = acc + iwt[h:h + 1, :] * jnp.maximum(d[:, half * tq:(half + 1) * tq], 0.0)
        bits = pltpu.bitcast(acc * idx_scale, I32)
        key = jnp.where(bits < 0, bits ^ 0x7FFFFFFF, bits)
        key_scr[kt] = jnp.where(key_row + off < key_lim, key, INT_MIN)
        return carry

    lax.fori_loop(0, n_kt, score_body, 0)

    n_part = 4

    def count_if(pred):
        def body(kt, parts):
            hit = jnp.where(pred(key_scr[kt], kt), 1, 0)
            parts = list(parts)
            for c in range(nsub):
                parts[c % n_part] = parts[c % n_part] + hit[c * V7X_SUBLANES:(c + 1) * V7X_SUBLANES]
            return tuple(parts)

        parts = lax.fori_loop(0, n_kt, body,
                              tuple(jnp.zeros((V7X_SUBLANES, tq), I32) for _ in range(n_part)))
        return jnp.sum(sum(parts[1:], parts[0]), axis=0, keepdims=True)

    zero = jnp.zeros((1, tq), I32)
    n_pos = count_if(lambda keys, kt: keys >= zero)
    thr = jnp.where(n_pos >= topk, zero, zero + INT_MIN)

    def bisect(b, state):
        thr, n_ge = state
        cand = thr + lax.shift_left(jnp.int32(1), 30 - b)
        n_cand = count_if(lambda keys, kt: keys >= cand)
        ok = n_cand >= topk
        return jnp.where(ok, cand, thr), jnp.where(ok, n_cand, n_ge)

    thr, n_ge = lax.fori_loop(0, 31, bisect, (thr, n_pos))

    tied = jnp.logical_and(n_ge > topk, thr > INT_MIN)

    @pl.when(jnp.max(jnp.where(tied, 1, 0)) > 0)
    def _():
        n_idx_bits = max(1, (n_total - 1).bit_length())
        need = topk - count_if(lambda keys, kt: keys > thr)

        def n_tied_before(limit):
            return count_if(lambda keys, kt: jnp.logical_and(keys == thr, key_row + kt * tk < limit))

        def idx_bisect(b, last):
            cand = last + lax.shift_left(jnp.int32(1), n_idx_bits - 1 - b)
            return jnp.where(n_tied_before(cand) < need, cand, last)

        last = lax.fori_loop(0, n_idx_bits, idx_bisect, zero)

        def demote(kt, carry):
            keys = key_scr[kt]
            drop = jnp.logical_and(jnp.logical_and(tied, keys == thr), key_row + kt * tk > last)
            key_scr[kt] = jnp.where(drop, keys - 1, keys)
            return carry

        lax.fori_loop(0, n_kt, demote, 0)

    thr = jnp.maximum(thr, INT_MIN + 1)

    m_scr[...] = jnp.full(m_scr.shape, NEG_BIG, F32)
    l_scr[...] = jnp.zeros(l_scr.shape, F32)
    acc_scr[...] = jnp.zeros(acc_scr.shape, F32)

    def stage_a(kt):
        slot = kt & 1
        off = pl.multiple_of(kt * tk, tk)
        bias = jnp.where(key_scr[kt] >= thr, 0.0, NEG_BIG)
        bias = jnp.concatenate([bias] * rep, axis=1)
        for g in range(N_KV_A):
            qg = q_ref[g * rep:(g + 1) * rep].reshape(r_all, HEAD_DIM)
            s = lax.dot_general(k_ref[g, pl.ds(off, tk), :], qg, (((1,), (1,)), ((), ())),
                                preferred_element_type=F32) + bias
            m_old = m_scr[g]
            m_new = jnp.maximum(m_old, jnp.max(s, axis=0, keepdims=True))
            s_scr[slot, g] = s
            ma_scr[slot, g, 0:1] = m_new
            ma_scr[slot, g, 1:2] = jnp.exp2(m_old - m_new)
            m_scr[g] = m_new

    def stage_b(kt):
        slot = kt & 1
        for g in range(N_KV_A):
            m_new = ma_scr[slot, g, 0:1]
            alpha = ma_scr[slot, g, 1:2]
            p = jnp.exp2(s_scr[slot, g] - m_new)
            l_scr[g] = alpha * l_scr[g] + jnp.sum(p, axis=0, keepdims=True)
            acc_scr[g] = alpha * acc_scr[g] + jnp.dot(vt_ref[g, kt], p.astype(BF16),
                                                      preferred_element_type=F32)

    stage_a(0)

    def att_body(kt, carry):
        stage_b(kt - 1)
        stage_a(kt)
        return carry

    lax.fori_loop(1, n_kt, att_body, 0)
    stage_b(n_kt - 1)
    for g in range(N_KV_A):
        out_t = acc_scr[g] / l_scr[g]
        for r in range(rep):
            hh = g * rep + r
            head = _head_rms_t(out_t[:, r * tq:(r + 1) * tq], gt_ref[hh])
            o_ref[:, hh * HEAD_DIM:(hh + 1) * HEAD_DIM] = head.T.astype(o_ref.dtype)


def _dsa(iq, iwt, ik2, qa, ka, vat, g_t, topk):
    n_heads, s, _ = qa.shape
    rep = n_heads // N_KV_A
    tq, tk = DSA_TQ, DSA_TK
    assert s % tk == 0
    idx_scale = (IDX_HEADS ** -0.5) * (IDX_DIM ** -0.5)
    vmem = (4 * N_KV_A * s * HEAD_DIM * 2 * 2 + 2 * s * V7X_LANES * 2 + tq * s * 4
            + 32 * rep * tq * tk * 4)
    return pl.pallas_call(
        functools.partial(_dsa_kernel, topk=topk, rep=rep, idx_scale=idx_scale, n_total=s),
        grid=(s // tq,),
        in_specs=[
            pl.BlockSpec((tq, IDX_HEADS * IDX_DIM), lambda i: (i, 0)),
            pl.BlockSpec((IDX_HEADS, tq), lambda i: (0, i)),
            pl.BlockSpec((s, V7X_LANES), lambda i: (0, 0)),
            pl.BlockSpec((n_heads, tq, HEAD_DIM), lambda i: (0, i, 0)),
            pl.BlockSpec((N_KV_A, s, HEAD_DIM), lambda i: (0, 0, 0)),
            pl.BlockSpec((N_KV_A, s // tk, HEAD_DIM, tk), lambda i: (0, 0, 0, 0)),
            pl.BlockSpec((n_heads, HEAD_DIM, V7X_LANES), lambda i: (0, 0, 0)),
        ],
        out_specs=pl.BlockSpec((tq, n_heads * HEAD_DIM), lambda i: (i, 0)),
        out_shape=jax.ShapeDtypeStruct((s, n_heads * HEAD_DIM), BF16),
        scratch_shapes=[
            pltpu.VMEM((IDX_HEADS // 2, 2 * tq, V7X_LANES), BF16),
            pltpu.VMEM((s // tk, tk, tq), I32),
            pltpu.VMEM((N_KV_A, 1, rep * tq), F32),
            pltpu.VMEM((N_KV_A, 1, rep * tq), F32),
            pltpu.VMEM((N_KV_A, HEAD_DIM, rep * tq), F32),
            pltpu.VMEM((2, N_KV_A, tk, rep * tq), F32),
            pltpu.VMEM((2, N_KV_A, 2, rep * tq), F32),
        ],
        compiler_params=_cparams(("parallel",), vmem),
        name="dsa",
    )(iq, iwt, ik2, qa, ka, vat, g_t)


def _sb_kernel(q_ref, k_ref, vt_ref, gt_ref, o_ref, acc_ref):
    t = SB_T
    nrg = t // V7X_SUBLANES
    n_h = q_ref.shape[0]
    i = pl.program_id(1)
    acc_ref[...] = jnp.zeros(acc_ref.shape, F32)

    def local_scan(hh, j, diagonal):
        z = lax.dot_general(k_ref[hh, j], q_ref[hh], (((1,), (1,)), ((), ())),
                            preferred_element_type=F32)
        e = jnp.exp(-jnp.abs(z))
        r = 1.0 / (1.0 + e)
        er = e * r
        pos = z >= 0.0
        beta = jnp.where(pos, r, er)
        keep = jnp.where(pos, er, r)
        if diagonal:
            rr = lax.broadcasted_iota(I32, (t, t), 0)
            tt = lax.broadcasted_iota(I32, (t, t), 1)
            key_pos = (rr & (V7X_SUBLANES - 1)) * nrg + (rr >> 3)
            causal = key_pos < tt
            beta = jnp.where(causal, beta, 0.0)
            keep = jnp.where(causal, keep, 1.0)
        run8 = jnp.ones((V7X_SUBLANES, t), F32)
        a_loc = [None] * nrg
        for g in range(nrg - 1, -1, -1):
            rows = slice(g * V7X_SUBLANES, (g + 1) * V7X_SUBLANES)
            a_loc[g] = beta[rows] * run8
            run8 = run8 * keep[rows]
        return a_loc, run8

    def finish(hh, j, scan, carry):
        a_loc, run8 = scan
        run = carry
        offs = [None] * V7X_SUBLANES
        for u in range(V7X_SUBLANES - 1, -1, -1):
            offs[u] = run
            run = run * run8[u:u + 1]
        off8 = jnp.concatenate(offs, axis=0)
        a = jnp.concatenate([al * off8 for al in a_loc], axis=0)
        acc_ref[hh] += jnp.dot(vt_ref[hh, j], a.astype(BF16), preferred_element_type=F32)
        return run

    j_prev = jnp.maximum(i - 1, 0)
    has_prev = (i > 0).astype(F32)
    carries = []
    for hh in range(n_h):
        scan_diag = local_scan(hh, i, True)
        scan_prev = local_scan(hh, j_prev, False)
        carry = finish(hh, i, scan_diag, jnp.ones((1, t), F32)) * has_prev
        carries.append(finish(hh, j_prev, scan_prev, carry))

    def cond(state):
        j, cs = state
        alive = jnp.max(functools.reduce(jnp.maximum, cs)) > 0.0
        return jnp.logical_and(j >= 0, alive)

    def body(state):
        j, cs = state
        return j - 1, tuple(finish(hh, j, local_scan(hh, j, False), cs[hh]) for hh in range(n_h))

    lax.while_loop(cond, body, (i - 2, tuple(carries)))
    for hh in range(n_h):
        head = _head_rms_t(acc_ref[hh], gt_ref[hh])
        o_ref[:, hh * HEAD_DIM:(hh + 1) * HEAD_DIM] = head.T.astype(o_ref.dtype)


def _sb(q, k_perm, v_t, g_t):
    n_heads, s, _ = q.shape
    t = SB_T
    nb = s // t
    n_h = SB_HEADS if n_heads % SB_HEADS == 0 else 1
    return pl.pallas_call(
        _sb_kernel,
        grid=(n_heads // n_h, nb),
        in_specs=[
            pl.BlockSpec((n_h, t, HEAD_DIM), lambda h, i: (h, i, 0)),
            pl.BlockSpec((n_h, nb, t, HEAD_DIM), lambda h, i: (h, 0, 0, 0)),
            pl.BlockSpec((n_h, nb, HEAD_DIM, t), lambda h, i: (h, 0, 0, 0)),
            pl.BlockSpec((n_h, HEAD_DIM, V7X_LANES), lambda h, i: (h, 0, 0)),
        ],
        out_specs=pl.BlockSpec((t, n_h * HEAD_DIM), lambda h, i: (i, h)),
        out_shape=jax.ShapeDtypeStruct((s, n_heads * HEAD_DIM), BF16),
        scratch_shapes=[pltpu.VMEM((n_h, HEAD_DIM, t), F32)],
        compiler_params=_cparams(("parallel", "parallel"),
                                 n_h * (8 * s * HEAD_DIM * 2 + 40 * t * t * 4)),
        name="stick_breaking",
    )(q, k_perm, v_t, g_t)


def _cb_kernel(q_ref, k_ref, v_ref, b_ref, g_ref, o_ref, *, win, pad):
    tq = CB_TQ
    i = pl.program_id(1)
    off = pl.multiple_of(i * tq, tq)
    col = lax.broadcasted_iota(I32, (tq, win), 1)
    in_seq = col + off >= pad
    for hh in range(q_ref.shape[0]):
        k_w = k_ref[hh, pl.ds(off, win), :]
        v_w = v_ref[hh, pl.ds(off, win), :]
        s = lax.dot_general(q_ref[hh], k_w, (((1,), (1,)), ((), ())), preferred_element_type=F32)
        s = jnp.where(in_seq, s + b_ref[hh], NEG_BIG)
        m = jnp.max(s, axis=-1, keepdims=True)
        p = jnp.exp(s - m)
        l = jnp.sum(p, axis=-1, keepdims=True)
        o = jnp.dot(p.astype(BF16), v_w, preferred_element_type=F32) / l
        cols = slice(hh * HEAD_DIM, (hh + 1) * HEAD_DIM)
        ms = jnp.mean(o * o, axis=-1, keepdims=True)
        o_ref[:, cols] = (o * lax.rsqrt(ms + EPS) * g_ref[:, cols]).astype(o_ref.dtype)


def _cb(q, k_pad, v_pad, bias, gain, pad):
    n_heads, s, _ = q.shape
    tq = CB_TQ
    win = pad + tq
    sp = k_pad.shape[1]
    n_h = CB_HEADS if n_heads % CB_HEADS == 0 else 1
    return pl.pallas_call(
        functools.partial(_cb_kernel, win=win, pad=pad),
        grid=(n_heads // n_h, s // tq),
        in_specs=[
            pl.BlockSpec((n_h, tq, HEAD_DIM), lambda h, i: (h, i, 0)),
            pl.BlockSpec((n_h, sp, HEAD_DIM), lambda h, i: (h, 0, 0)),
            pl.BlockSpec((n_h, sp, HEAD_DIM), lambda h, i: (h, 0, 0)),
            pl.BlockSpec((n_h, tq, win), lambda h, i: (h, 0, 0)),
            pl.BlockSpec((1, n_h * HEAD_DIM), lambda h, i: (0, h)),
        ],
        out_specs=pl.BlockSpec((tq, n_h * HEAD_DIM), lambda h, i: (i, h)),
        out_shape=jax.ShapeDtypeStruct((s, n_heads * HEAD_DIM), BF16),
        compiler_params=_cparams(("parallel", "parallel"),
                                 n_h * (8 * sp * HEAD_DIM * 2 + 16 * tq * win * 4)),
        name="chunk_band",
    )(q, k_pad, v_pad, bias, gain.reshape(1, n_heads * HEAD_DIM))


def _cb_bias(rel_bias):
    tq = CB_TQ
    pad = N_LEFT_CHUNKS * CHUNK
    win = pad + tq
    n_heads = rel_bias.shape[0]
    qi = np.arange(tq)[:, None]
    kj = np.arange(win)[None, :]
    q_chunk = qi // CHUNK
    k_chunk = kj // CHUNK - N_LEFT_CHUNKS
    in_band = (k_chunk <= q_chunk) & (k_chunk >= q_chunk - N_LEFT_CHUNKS)
    p = win + tq
    n_clipped = pad + tq - 1 - REL_CLIP
    lo = 2 * REL_CLIP - (p - 1 - n_clipped)
    assert lo >= 0 and pad - (win - 1) >= -REL_CLIP
    u = jnp.concatenate([jnp.broadcast_to(rel_bias[:, 2 * REL_CLIP:], (n_heads, n_clipped)),
                         rel_bias[:, lo:2 * REL_CLIP + 1][:, ::-1]], axis=1).astype(F32)
    rows = jnp.tile(u, (1, tq + 1))[:, :tq * (p + 1)].reshape(n_heads, tq, p + 1)[:, :, :win]
    bias = rows[:, ::-1, :]
    return jnp.where(jnp.asarray(in_band)[None], bias, NEG_BIG)


def _out_proj_kernel(a_ref, b_ref, c_ref, w_ref, r_ref, g_ref, o_ref):
    acc, row = None, 0
    for x_ref in (a_ref, b_ref, c_ref):
        part = _dot(x_ref[...], w_ref[row:row + x_ref.shape[1], :])
        acc = part if acc is None else acc + part
        row += x_ref.shape[1]
    o_ref[...] = r_ref[...] + g_ref[...] * acc


def _out_proj(o_a, o_b, o_c, w, l, resid, gate):
    m = o_a.shape[0]
    k, n = w.shape[1], w.shape[2]
    assert o_a.shape[1] + o_b.shape[1] + o_c.shape[1] == k
    tm, tn, tk = _mm_tiles(m, k, n, w.dtype.itemsize)
    assert tk == k
    lhs = lambda x: pl.BlockSpec((tm, x.shape[1]), lambda i, j: (i, 0))
    return pl.pallas_call(
        _out_proj_kernel,
        grid=(m // tm, n // tn),
        in_specs=[lhs(o_a), lhs(o_b), lhs(o_c),
                  pl.BlockSpec((None, k, tn), lambda i, j: (l, 0, j)),
                  pl.BlockSpec((tm, tn), lambda i, j: (i, j)),
                  pl.BlockSpec((1, tn), lambda i, j: (0, j))],
        out_specs=pl.BlockSpec((tm, tn), lambda i, j: (i, j)),
        out_shape=jax.ShapeDtypeStruct((m, n), F32),
        compiler_params=_cparams(("parallel", "parallel"),
                                 _mm_vmem(tm, tn, tk, w.dtype.itemsize) + 3 * tm * tn * 4),
        name="out_proj",
    )(o_a, o_b, o_c, w, resid, gate.reshape(1, n))


def _pack_kernel(x_ref, o_ref, *, runs):
    end = 0
    for src, dst, height in runs:
        if dst > end:
            o_ref[end:dst, :] = jnp.zeros((dst - end, o_ref.shape[1]), o_ref.dtype)
        o_ref[dst:dst + height, :] = x_ref[src:src + height, :].astype(o_ref.dtype)
        end = dst + height
    if end < o_ref.shape[0]:
        o_ref[end:, :] = jnp.zeros((o_ref.shape[0] - end, o_ref.shape[1]), o_ref.dtype)


def _cast_kernel(x_ref, o_ref):
    o_ref[...] = x_ref[...].astype(o_ref.dtype)


def _cast_layer(w, l, dtype):
    _, k, n = w.shape
    tr = _largest_tile(k, 16, 256)
    out = pl.pallas_call(
        _cast_kernel,
        grid=(k // tr,),
        in_specs=[pl.BlockSpec((None, tr, n), lambda i: (l, i, 0))],
        out_specs=pl.BlockSpec((tr, n), lambda i: (i, 0)),
        out_shape=jax.ShapeDtypeStruct((k, n), dtype),
        compiler_params=_cparams(("parallel",), 4 * tr * n * 4),
        name="cast_layer",
    )(w)
    return out.reshape(1, k, n)


def _packed_layout(sizes):
    units, src = [], 0
    for idx, width in enumerate(sizes):
        if width >= V7X_LANES or not units or units[-1][2] >= V7X_LANES:
            units.append([[idx], src, width])
        else:
            units[-1][0].append(idx)
            units[-1][2] += width
        src += width
    srcs = np.concatenate([[0], np.cumsum(sizes)[:-1]])
    offsets, runs, dst = [0] * len(sizes), [], 0
    for members, usrc, uwidth in sorted(units, key=lambda u: -u[2]):
        padded = -(-uwidth // V7X_LANES) * V7X_LANES
        dst = -(-dst // padded) * padded
        for idx in members:
            offsets[idx] = dst + int(srcs[idx]) - usrc
        runs.append((usrc, dst, uwidth))
        dst += padded
    return offsets, tuple(runs), -(-dst // PROJ_N_ALIGN) * PROJ_N_ALIGN


def _pack_w_in(w_t, l, sizes):
    offsets, runs, n_out = _packed_layout(sizes)
    assert all(v % 16 == 0 for run in runs for v in run), "segments must be bf16-tile aligned"
    _, n_in, k = w_t.shape
    tc = _largest_tile(k, V7X_LANES, 256)
    packed = pl.pallas_call(
        functools.partial(_pack_kernel, runs=runs),
        grid=(k // tc,),
        in_specs=[pl.BlockSpec((None, n_in, tc), lambda i: (l, 0, i))],
        out_specs=pl.BlockSpec((n_out, tc), lambda i: (0, i)),
        out_shape=jax.ShapeDtypeStruct((n_out, k), BF16),
        compiler_params=_cparams(("parallel",), 3 * n_in * tc * 4 + 2 * n_out * tc * 2),
        name="pack_w_in",
    )(w_t)
    return packed.reshape(1, n_out, k), offsets


def kernel(x, c, w_ada, b_ada, norm_attn_g, w_in, rel_bias, head_norm_g, w_out, norm_ffn_g,
           w_gate_up, w_down, final_norm_g):
    b, s, d = x.shape
    assert b == 1, "single-sequence prefill only"
    depth = w_ada.shape[0]
    n_heads = d // HEAD_DIM
    ha = 3 * n_heads // 8
    hb = 3 * n_heads // 8
    hc = n_heads - ha - hb
    assert ha % N_KV_A == 0 and s % SB_T == 0
    sizes = (ha * HEAD_DIM, N_KV_A * HEAD_DIM, N_KV_A * HEAD_DIM, IDX_HEADS * IDX_DIM, IDX_DIM,
             IDX_HEADS, hb * HEAD_DIM, hb * HEAD_DIM, hb * HEAD_DIM,
             hc * HEAD_DIM, hc * HEAD_DIM, hc * HEAD_DIM)
    assert w_in.shape[2] == sum(sizes)
    topk = min(IDX_TOPK_MAX, s // 4)
    scale = HEAD_DIM ** -0.5
    pad = N_LEFT_CHUNKS * CHUNK

    rope128 = _rope_tables(s, HEAD_DIM, 1)
    rope64 = _rope_tables(s, IDX_DIM, V7X_LANES // IDX_DIM)

    xs = x.reshape(s, d)
    mod = _adaln(c, w_ada, b_ada)
    w_in_t = jnp.swapaxes(w_in, 1, 2)

    for l in range(depth):
        sh1, sc1, g1, sh2, sc2, g2 = [mod[l, :, n * d:(n + 1) * d] for n in range(6)]

        h = _norm(xs, norm_attn_g[l], sc1, sh1)
        w_packed, offs = _pack_w_in(w_in_t, l, sizes)
        o_qa, o_ka, o_va, o_iq, o_ik, o_iw, o_qb, o_kb, o_vb, o_qc, o_kc, o_vc = offs
        assert o_iw == o_ik + IDX_DIM and o_ik % V7X_LANES == 0
        proj = _matmul_nt(h, w_packed, 0, F32)

        qa = _prep_heads(proj, o_qa, ha, rope128, HEAD_DIM, scale * math.log2(math.e))
        ka = _prep_heads(proj, o_ka, N_KV_A, rope128, HEAD_DIM)
        va = _prep_heads(proj, o_va, N_KV_A)
        vat = va.reshape(N_KV_A, s // DSA_TK, DSA_TK, HEAD_DIM).swapaxes(2, 3)
        iq = _prep_heads(proj, o_iq, IDX_HEADS * IDX_DIM // V7X_LANES, rope64, IDX_DIM,
                         head_major=False)
        ik2, iw = _prep_ikw(proj, o_ik, rope64)
        gain = head_norm_g[l]
        gain_t = jnp.broadcast_to(gain.reshape(n_heads, HEAD_DIM, 1), (n_heads, HEAD_DIM, V7X_LANES))
        o_a = _dsa(iq, iw[:, :IDX_HEADS].T, ik2, qa, ka, vat, gain_t[:ha], topk)

        qb = _prep_heads(proj, o_qb, hb, scale=scale)
        kb = _prep_heads(proj, o_kb, hb)
        vb = _prep_heads(proj, o_vb, hb)
        nb, sub = s // SB_T, V7X_SUBLANES
        kb = kb.reshape(hb, nb, sub, SB_T // sub, HEAD_DIM).swapaxes(2, 3).reshape(hb, nb, SB_T, HEAD_DIM)
        vbt = vb.reshape(hb, nb, sub, SB_T // sub, HEAD_DIM).swapaxes(2, 3).reshape(hb, nb, SB_T, HEAD_DIM)
        vbt = vbt.swapaxes(2, 3)
        o_b = _sb(qb, kb, vbt, gain_t[ha:ha + hb])

        qc = _prep_heads(proj, o_qc, hc, scale=scale)
        kc = _prep_heads(proj, o_kc, hc)
        vc = _prep_heads(proj, o_vc, hc)
        kc = jnp.pad(kc, ((0, 0), (pad, 0), (0, 0)))
        vc = jnp.pad(vc, ((0, 0), (pad, 0), (0, 0)))
        o_c = _cb(qc, kc, vc, _cb_bias(rel_bias[l]), gain[(ha + hb) * HEAD_DIM:], pad)

        xs = _out_proj(o_a, o_b, o_c, w_out, l, xs, g1)

        h = _norm(xs, norm_ffn_g[l], sc2, sh2)
        act = _matmul_swiglu(h, w_gate_up, l)
        xs = _matmul_resid(act, _cast_layer(w_down, l, BF16), 0, xs, g2)

    out = _norm(xs, final_norm_g, out_dtype=x.dtype)
    return out.reshape(b, s, d)
```

```python
import functools
import math

import numpy as np
import jax
import jax.numpy as jnp
from jax import lax
from jax.experimental import pallas as pl
from jax.experimental.pallas import tpu as pltpu

F32 = jnp.float32
BF16 = jnp.bfloat16
I32 = jnp.int32

HEAD_DIM = 128
N_KV_A = 2
IDX_HEADS = 16
IDX_DIM = 64
IDX_TOPK_MAX = 256
CHUNK = 64
N_LEFT_CHUNKS = 8
REL_CLIP = 256
ROPE_THETA = 10000.0
EPS = 1e-6

V7X_LANES = 128
V7X_SUBLANES = 8
V7X_VMEM_BYTES = 64 * 1024 * 1024
V7X_MXU_COLS = 256

PROJ_N_ALIGN = 2 * V7X_MXU_COLS

NEG_BIG = -1e30
INT_MIN = -(2 ** 31)

DSA_TQ = 256
DSA_TK = 512
SB_T = 256
SB_HEADS = 4
CB_TQ = 128
CB_HEADS = 4


def _cparams(semantics, vmem_bytes, flags=None):
    limit = int(min(V7X_VMEM_BYTES * 7 // 8, max(vmem_bytes, 16 * 1024 * 1024)))
    return pltpu.CompilerParams(dimension_semantics=semantics, vmem_limit_bytes=limit, flags=flags)


def _largest_tile(n, unit, cap):
    best = None
    t = unit
    while t <= min(n, cap):
        if n % t == 0:
            best = t
        t += unit
    assert best is not None, (n, unit, cap)
    return best


def _adaln_kernel(c_ref, w_ref, b_ref, o_ref, acc_ref):
    k = pl.program_id(1)
    c = c_ref[...]
    cond = c * jax.nn.sigmoid(c)
    tr, n = w_ref.shape

    @pl.when(k == 0)
    def _():
        acc_ref[...] = jnp.zeros_like(acc_ref)

    for j in range(n // V7X_LANES):
        cols = slice(j * V7X_LANES, (j + 1) * V7X_LANES)
        prod = w_ref[:, cols] * cond
        part = prod[0:V7X_SUBLANES]
        for g in range(1, tr // V7X_SUBLANES):
            part = part + prod[g * V7X_SUBLANES:(g + 1) * V7X_SUBLANES]
        acc_ref[:, cols] += part

    @pl.when(k == pl.num_programs(1) - 1)
    def _():
        o_ref[...] = jnp.sum(acc_ref[...], axis=0, keepdims=True) + b_ref[...]


def _adaln(c, w_ada, b_ada):
    depth, d, n = w_ada.shape
    tr = _largest_tile(d, V7X_SUBLANES, 128)
    c_b = jnp.broadcast_to(c.reshape(d, 1), (d, V7X_LANES))
    out = pl.pallas_call(
        _adaln_kernel,
        grid=(depth, d // tr),
        in_specs=[
            pl.BlockSpec((tr, V7X_LANES), lambda l, k: (k, 0)),
            pl.BlockSpec((None, tr, n), lambda l, k: (l, k, 0)),
            pl.BlockSpec((None, 1, n), lambda l, k: (l, 0, 0)),
        ],
        out_specs=pl.BlockSpec((None, 1, n), lambda l, k: (l, 0, 0)),
        out_shape=jax.ShapeDtypeStruct((depth, 1, n), F32),
        scratch_shapes=[pltpu.VMEM((V7X_SUBLANES, n), F32)],
        compiler_params=_cparams(("parallel", "arbitrary"), 3 * tr * n * 4),
        name="adaln",
    )(c_b, w_ada, b_ada.reshape(depth, 1, n))
    return out


def _norm_kernel(x_ref, g_ref, *rest, modulate):
    o_ref = rest[-1]
    x = x_ref[...]
    ms = jnp.mean(x * x, axis=-1, keepdims=True)
    y = x * lax.rsqrt(ms + EPS) * g_ref[...]
    if modulate:
        sc_ref, sh_ref = rest[0], rest[1]
        y = y * (1.0 + sc_ref[...]) + sh_ref[...]
    o_ref[...] = y.astype(o_ref.dtype)


def _norm(x, g, scale=None, shift=None, out_dtype=BF16):
    s, d = x.shape
    ts = _largest_tile(s, V7X_SUBLANES, 256)
    modulate = scale is not None
    row = pl.BlockSpec((1, d), lambda i: (0, 0))
    args = [x, g.reshape(1, d)]
    specs = [pl.BlockSpec((ts, d), lambda i: (i, 0)), row]
    if modulate:
        args += [scale.reshape(1, d), shift.reshape(1, d)]
        specs += [row, row]
    return pl.pallas_call(
        functools.partial(_norm_kernel, modulate=modulate),
        grid=(s // ts,),
        in_specs=specs,
        out_specs=pl.BlockSpec((ts, d), lambda i: (i, 0)),
        out_shape=jax.ShapeDtypeStruct((s, d), out_dtype),
        compiler_params=_cparams(("parallel",), 6 * ts * d * 4),
        name="rmsnorm",
    )(*args)


def _dot(a, w):
    return jnp.dot(a, w.astype(BF16), preferred_element_type=F32)


def _mm_plain_kernel(a_ref, wt_ref, o_ref):
    o_ref[...] = lax.dot_general(a_ref[...], wt_ref[...].astype(BF16), (((1,), (1,)), ((), ())),
                                 preferred_element_type=F32).astype(o_ref.dtype)


def _mm_swiglu_kernel(a_ref, wg_ref, wu_ref, o_ref):
    a = a_ref[...]
    gate = _dot(a, wg_ref[...])
    up = _dot(a, wu_ref[...])
    o_ref[...] = (gate * jax.nn.sigmoid(gate) * up).astype(o_ref.dtype)


def _mm_resid_kernel(a_ref, w_ref, r_ref, g_ref, o_ref, *acc, nk):
    if nk == 1:
        o_ref[...] = r_ref[...] + g_ref[...] * _dot(a_ref[...], w_ref[...])
        return
    acc_ref, = acc
    k = pl.program_id(2)

    @pl.when(k == 0)
    def _():
        acc_ref[...] = jnp.zeros_like(acc_ref)

    acc_ref[...] += _dot(a_ref[...], w_ref[...])

    @pl.when(k == nk - 1)
    def _():
        o_ref[...] = r_ref[...] + g_ref[...] * acc_ref[...]


def _mm_tiles(m, k, n, w_itemsize):
    tn = _largest_tile(n, V7X_LANES, 768 if w_itemsize == 2 else 256)
    if k <= 4096:
        return _largest_tile(m, V7X_SUBLANES, 1024), tn, k
    if w_itemsize == 2:
        return _largest_tile(m, V7X_SUBLANES, 512), _largest_tile(n, V7X_LANES, 256), k
    return _largest_tile(m, V7X_SUBLANES, 1024), tn, _largest_tile(k, V7X_LANES, 5632)


def _mm_vmem(tm, tn, tk, w_itemsize, n_w=1):
    return (2 * tm * tk * 2 + n_w * (2 * tk * tn * w_itemsize + tk * tn * 2)
            + (4 + 2 * n_w) * tm * tn * 4)


def _matmul_nt(a, w_t, l, out_dtype):
    m, k = a.shape
    n = w_t.shape[1]
    tm, tn, tk = _mm_tiles(m, k, n, w_t.dtype.itemsize)
    assert tk == k
    return pl.pallas_call(
        _mm_plain_kernel,
        grid=(m // tm, n // tn),
        in_specs=[pl.BlockSpec((tm, k), lambda i, j: (i, 0)),
                  pl.BlockSpec((None, tn, k), lambda i, j: (l, j, 0))],
        out_specs=pl.BlockSpec((tm, tn), lambda i, j: (i, j)),
        out_shape=jax.ShapeDtypeStruct((m, n), out_dtype),
        compiler_params=_cparams(("parallel", "parallel"), _mm_vmem(tm, tn, tk, w_t.dtype.itemsize)),
        name="matmul",
    )(a, w_t)


def _matmul_swiglu(a, w_gate_up, l):
    m, k = a.shape
    n = w_gate_up.shape[2] // 2
    tm, tn, tk = _mm_tiles(m, k, n, w_gate_up.dtype.itemsize)
    assert tk == k
    nb = n // tn
    return pl.pallas_call(
        _mm_swiglu_kernel,
        grid=(m // tm, nb),
        in_specs=[pl.BlockSpec((tm, k), lambda i, j: (i, 0)),
                  pl.BlockSpec((None, k, tn), lambda i, j: (l, 0, j)),
                  pl.BlockSpec((None, k, tn), lambda i, j: (l, 0, j + nb))],
        out_specs=pl.BlockSpec((tm, tn), lambda i, j: (i, j)),
        out_shape=jax.ShapeDtypeStruct((m, n), BF16),
        compiler_params=_cparams(("parallel", "parallel"),
                                 _mm_vmem(tm, tn, tk, w_gate_up.dtype.itemsize, n_w=2)),
        name="matmul_swiglu",
    )(a, w_gate_up, w_gate_up)


def _matmul_resid(a, w, l, resid, gate):
    m, k = a.shape
    n = w.shape[2]
    tm, tn, tk = _mm_tiles(m, k, n, w.dtype.itemsize)
    nk = k // tk
    scratch = [pltpu.VMEM((tm, tn), F32)] if nk > 1 else []
    return pl.pallas_call(
        functools.partial(_mm_resid_kernel, nk=nk),
        grid=(m // tm, n // tn, nk),
        in_specs=[pl.BlockSpec((tm, tk), lambda i, j, kk: (i, kk)),
                  pl.BlockSpec((None, tk, tn), lambda i, j, kk: (l, kk, j)),
                  pl.BlockSpec((tm, tn), lambda i, j, kk: (i, j)),
                  pl.BlockSpec((1, tn), lambda i, j, kk: (0, j))],
        out_specs=pl.BlockSpec((tm, tn), lambda i, j, kk: (i, j)),
        out_shape=jax.ShapeDtypeStruct((m, n), F32),
        scratch_shapes=scratch,
        compiler_params=_cparams(("parallel", "parallel", "arbitrary"),
                                 _mm_vmem(tm, tn, tk, w.dtype.itemsize) + 3 * tm * tn * 4),
        name="matmul_resid",
    )(a, w, resid, gate.reshape(1, n))


def _rope_tables(s, d, lane_groups):
    pos = jnp.arange(s, dtype=jnp.int32).astype(F32)
    inv = ROPE_THETA ** (-jnp.arange(0, d, 2, dtype=F32) / d)
    ang = pos[:, None] * inv[None, :]
    cos, sin = jnp.cos(ang), jnp.sin(ang)
    cos_t = jnp.tile(jnp.concatenate([cos, cos], axis=1), (1, lane_groups))
    sin_t = jnp.tile(jnp.concatenate([-sin, sin], axis=1), (1, lane_groups))
    return cos_t, sin_t


def _rot_half(x, d):
    if d == V7X_LANES:
        return pltpu.roll(x, d // 2, 1)
    lane = lax.broadcasted_iota(I32, x.shape, 1)
    first = (lane & (d - 1)) < d // 2
    return jnp.where(first, pltpu.roll(x, V7X_LANES - d // 2, 1), pltpu.roll(x, d // 2, 1))


def _prep_kernel(x_ref, *rest, rope_dim, scale, head_major):
    o_ref = rest[-1]
    if rope_dim:
        cos, sin = rest[0][...], rest[1][...]
    for h in range(x_ref.shape[1] // V7X_LANES):
        cols = slice(h * V7X_LANES, (h + 1) * V7X_LANES)
        x = x_ref[:, cols]
        if rope_dim:
            x = x * cos + _rot_half(x, rope_dim) * sin
        if scale != 1.0:
            x = x * scale
        if head_major:
            o_ref[h] = x.astype(o_ref.dtype)
        else:
            o_ref[:, cols] = x.astype(o_ref.dtype)


def _prep_heads(proj, col0, n_blocks, tables=None, rope_dim=0, scale=1.0, head_major=True):
    s = proj.shape[0]
    width = n_blocks * V7X_LANES
    assert col0 % width == 0 or n_blocks == 1, "segment must start at a multiple of its width"
    ts = _largest_tile(s, V7X_SUBLANES, 512)
    args = [proj]
    specs = [pl.BlockSpec((ts, width), lambda i: (i, col0 // width))]
    if rope_dim:
        args += list(tables)
        specs += [pl.BlockSpec((ts, V7X_LANES), lambda i: (i, 0))] * 2
    if head_major:
        out_spec = pl.BlockSpec((n_blocks, ts, V7X_LANES), lambda i: (0, i, 0))
        out_shape = jax.ShapeDtypeStruct((n_blocks, s, V7X_LANES), BF16)
    else:
        out_spec = pl.BlockSpec((ts, width), lambda i: (i, 0))
        out_shape = jax.ShapeDtypeStruct((s, width), BF16)
    return pl.pallas_call(
        functools.partial(_prep_kernel, rope_dim=rope_dim, scale=scale, head_major=head_major),
        grid=(s // ts,),
        in_specs=specs,
        out_specs=out_spec,
        out_shape=out_shape,
        compiler_params=_cparams(("parallel",), 8 * ts * width * 4),
        name="prep_heads",
    )(*args)


def _ikw_kernel(x_ref, cos_ref, sin_ref, ik_ref, iw_ref):
    x = x_ref[...]
    y = x * cos_ref[...] + _rot_half(x, IDX_DIM) * sin_ref[...]
    lane = lax.broadcasted_iota(I32, x.shape, 1)
    ik_ref[...] = jnp.where(lane < IDX_DIM, y, pltpu.roll(y, IDX_DIM, 1)).astype(ik_ref.dtype)
    iw_ref[...] = pltpu.roll(x, V7X_LANES - IDX_DIM, 1)


def _prep_ikw(proj, col0, tables):
    s = proj.shape[0]
    ts = _largest_tile(s, V7X_SUBLANES, 1024)
    cb = col0 // V7X_LANES
    blk = pl.BlockSpec((ts, V7X_LANES), lambda i: (i, 0))
    return pl.pallas_call(
        _ikw_kernel,
        grid=(s // ts,),
        in_specs=[pl.BlockSpec((ts, V7X_LANES), lambda i: (i, cb)), blk, blk],
        out_specs=[blk, blk],
        out_shape=[jax.ShapeDtypeStruct((s, V7X_LANES), BF16),
                   jax.ShapeDtypeStruct((s, V7X_LANES), F32)],
        compiler_params=_cparams(("parallel",), 16 * ts * V7X_LANES * 4),
        name="prep_ikw",
    )(proj, *tables)


def _head_rms_t(x_t, g_t):
    ms = jnp.mean(x_t * x_t, axis=0, keepdims=True)
    y = x_t * lax.rsqrt(ms + EPS)
    reps = x_t.shape[1] // g_t.shape[1]
    return y * (g_t if reps == 1 else jnp.concatenate([g_t] * reps, axis=1))


def _dsa_kernel(iq_ref, iwt_ref, ik_ref, q_ref, k_ref, vt_ref, gt_ref, o_ref,
                rhs_scr, key_scr, m_scr, l_scr, acc_scr, s_scr, ma_scr,
                *, topk, rep, idx_scale, n_total):
    tq, tk = DSA_TQ, DSA_TK
    nsub = tk // V7X_SUBLANES
    r_all = rep * tq
    i = pl.program_id(0)
    start = i * tq
    n_kt = (start + tq + tk - 1) >> int(math.log2(tk))

    lane = lax.broadcasted_iota(I32, (tq, V7X_LANES), 1)
    for p in range(IDX_HEADS // 2):
        pair = iq_ref[:, p * V7X_LANES:(p + 1) * V7X_LANES]
        zero = jnp.zeros_like(pair)
        rhs_scr[p, :tq] = jnp.where(lane < IDX_DIM, pair, zero)
        rhs_scr[p, tq:] = jnp.where(lane >= IDX_DIM, pair, zero)

    iwt = iwt_ref[...]
    key_row = lax.broadcasted_iota(I32, (tk, tq), 0)
    q_lane = lax.broadcasted_iota(I32, (tk, tq), 1)
    key_lim = start + ((q_lane >> int(math.log2(CHUNK))) + 1) * CHUNK

    def score_body(kt, carry):
        off = pl.multiple_of(kt * tk, tk)
        ik_t = ik_ref[pl.ds(off, tk), :]
        acc = jnp.zeros((tk, tq), F32)
        for p in range(IDX_HEADS // 2):
            d = lax.dot_general(ik_t, rhs_scr[p], (((1,), (1,)), ((), ())),
                                preferred_element_type=F32)
            for half in range(2):
                h = 2 * p + half
                acc = acc + iwt[h:h + 1, :] * jnp.maximum(d[:, half * tq:(half + 1) * tq], 0.0)
        bits = pltpu.bitcast(acc * idx_scale, I32)
        key = jnp.where(bits < 0, bits ^ 0x7FFFFFFF, bits)
        key_scr[kt] = jnp.where(key_row + off < key_lim, key, INT_MIN)
        return carry

    lax.fori_loop(0, n_kt, score_body, 0)

    n_part = 4

    def count_if(pred):
        def body(kt, parts):
            hit = jnp.where(pred(key_scr[kt], kt), 1, 0)
            parts = list(parts)
            for c in range(nsub):
                parts[c % n_part] = parts[c % n_part] + hit[c * V7X_SUBLANES:(c + 1) * V7X_SUBLANES]
            return tuple(parts)

        parts = lax.fori_loop(0, n_kt, body,
                              tuple(jnp.zeros((V7X_SUBLANES, tq), I32) for _ in range(n_part)))
        return jnp.sum(sum(parts[1:], parts[0]), axis=0, keepdims=True)

    zero = jnp.zeros((1, tq), I32)
    n_pos = count_if(lambda keys, kt: keys >= zero)
    thr = jnp.where(n_pos >= topk, zero, zero + INT_MIN)

    def bisect(b, state):
        thr, n_ge = state
        cand = thr + lax.shift_left(jnp.int32(1), 30 - b)
        n_cand = count_if(lambda keys, kt: keys >= cand)
        ok = n_cand >= topk
        return jnp.where(ok, cand, thr), jnp.where(ok, n_cand, n_ge)

    thr, n_ge = lax.fori_loop(0, 31, bisect, (thr, n_pos))

    tied = jnp.logical_and(n_ge > topk, thr > INT_MIN)

    @pl.when(jnp.max(jnp.where(tied, 1, 0)) > 0)
    def _():
        n_idx_bits = max(1, (n_total - 1).bit_length())
        need = topk - count_if(lambda keys, kt: keys > thr)

        def n_tied_before(limit):
            return count_if(lambda keys, kt: jnp.logical_and(keys == thr, key_row + kt * tk < limit))

        def idx_bisect(b, last):
            cand = last + lax.shift_left(jnp.int32(1), n_idx_bits - 1 - b)
            return jnp.where(n_tied_before(cand) < need, cand, last)

        last = lax.fori_loop(0, n_idx_bits, idx_bisect, zero)

        def demote(kt, carry):
            keys = key_scr[kt]
            drop = jnp.logical_and(jnp.logical_and(tied, keys == thr), key_row + kt * tk > last)
            key_scr[kt] = jnp.where(drop, keys - 1, keys)
            return carry

        lax.fori_loop(0, n_kt, demote, 0)

    thr = jnp.maximum(thr, INT_MIN + 1)

    m_scr[...] = jnp.full(m_scr.shape, NEG_BIG, F32)
    l_scr[...] = jnp.zeros(l_scr.shape, F32)
    acc_scr[...] = jnp.zeros(acc_scr.shape, F32)

    def stage_a(kt):
        slot = kt & 1
        off = pl.multiple_of(kt * tk, tk)
        bias = jnp.where(key_scr[kt] >= thr, 0.0, NEG_BIG)
        bias = jnp.concatenate([bias] * rep, axis=1)
        for g in range(N_KV_A):
            qg = q_ref[g * rep:(g + 1) * rep].reshape(r_all, HEAD_DIM)
            s = lax.dot_general(k_ref[g, pl.ds(off, tk), :], qg, (((1,), (1,)), ((), ())),
                                preferred_element_type=F32) + bias
            m_old = m_scr[g]
            m_new = jnp.maximum(m_old, jnp.max(s, axis=0, keepdims=True))
            s_scr[slot, g] = s
            ma_scr[slot, g, 0:1] = m_new
            ma_scr[slot, g, 1:2] = jnp.exp2(m_old - m_new)
            m_scr[g] = m_new

    def stage_b(kt):
        slot = kt & 1
        for g in range(N_KV_A):
            m_new = ma_scr[slot, g, 0:1]
            alpha = ma_scr[slot, g, 1:2]
            p = jnp.exp2(s_scr[slot, g] - m_new)
            l_scr[g] = alpha * l_scr[g] + jnp.sum(p, axis=0, keepdims=True)
            acc_scr[g] = alpha * acc_scr[g] + jnp.dot(vt_ref[g, kt], p.astype(BF16),
                                                      preferred_element_type=F32)

    stage_a(0)

    def att_body(kt, carry):
        stage_b(kt - 1)
        stage_a(kt)
        return carry

    lax.fori_loop(1, n_kt, att_body, 0)
    stage_b(n_kt - 1)
    for g in range(N_KV_A):
        out_t = acc_scr[g] / l_scr[g]
        for r in range(rep):
            hh = g * rep + r
            head = _head_rms_t(out_t[:, r * tq:(r + 1) * tq], gt_ref[hh])
            o_ref[:, hh * HEAD_DIM:(hh + 1) * HEAD_DIM] = head.T.astype(o_ref.dtype)


def _dsa(iq, iwt, ik2, qa, ka, vat, g_t, topk):
    n_heads, s, _ = qa.shape
    rep = n_heads // N_KV_A
    tq, tk = DSA_TQ, DSA_TK
    assert s % tk == 0
    idx_scale = (IDX_HEADS ** -0.5) * (IDX_DIM ** -0.5)
    vmem = (4 * N_KV_A * s * HEAD_DIM * 2 * 2 + 2 * s * V7X_LANES * 2 + tq * s * 4
            + 32 * rep * tq * tk * 4)
    return pl.pallas_call(
        functools.partial(_dsa_kernel, topk=topk, rep=rep, idx_scale=idx_scale, n_total=s),
        grid=(s // tq,),
        in_specs=[
            pl.BlockSpec((tq, IDX_HEADS * IDX_DIM), lambda i: (i, 0)),
            pl.BlockSpec((IDX_HEADS, tq), lambda i: (0, i)),
            pl.BlockSpec((s, V7X_LANES), lambda i: (0, 0)),
            pl.BlockSpec((n_heads, tq, HEAD_DIM), lambda i: (0, i, 0)),
            pl.BlockSpec((N_KV_A, s, HEAD_DIM), lambda i: (0, 0, 0)),
            pl.BlockSpec((N_KV_A, s // tk, HEAD_DIM, tk), lambda i: (0, 0, 0, 0)),
            pl.BlockSpec((n_heads, HEAD_DIM, V7X_LANES), lambda i: (0, 0, 0)),
        ],
        out_specs=pl.BlockSpec((tq, n_heads * HEAD_DIM), lambda i: (i, 0)),
        out_shape=jax.ShapeDtypeStruct((s, n_heads * HEAD_DIM), BF16),
        scratch_shapes=[
            pltpu.VMEM((IDX_HEADS // 2, 2 * tq, V7X_LANES), BF16),
            pltpu.VMEM((s // tk, tk, tq), I32),
            pltpu.VMEM((N_KV_A, 1, rep * tq), F32),
            pltpu.VMEM((N_KV_A, 1, rep * tq), F32),
            pltpu.VMEM((N_KV_A, HEAD_DIM, rep * tq), F32),
            pltpu.VMEM((2, N_KV_A, tk, rep * tq), F32),
            pltpu.VMEM((2, N_KV_A, 2, rep * tq), F32),
        ],
        compiler_params=_cparams(("parallel",), vmem),
        name="dsa",
    )(iq, iwt, ik2, qa, ka, vat, g_t)


def _sb_kernel(q_ref, k_ref, vt_ref, gt_ref, o_ref, acc_ref):
    t = SB_T
    nrg = t // V7X_SUBLANES
    n_h = q_ref.shape[0]
    i = pl.program_id(1)
    acc_ref[...] = jnp.zeros(acc_ref.shape, F32)

    def local_scan(hh, j, diagonal):
        z = lax.dot_general(k_ref[hh, j], q_ref[hh], (((1,), (1,)), ((), ())),
                            preferred_element_type=F32)
        e = jnp.exp(-jnp.abs(z))
        r = 1.0 / (1.0 + e)
        er = e * r
        pos = z >= 0.0
        beta = jnp.where(pos, r, er)
        keep = jnp.where(pos, er, r)
        if diagonal:
            rr = lax.broadcasted_iota(I32, (t, t), 0)
            tt = lax.broadcasted_iota(I32, (t, t), 1)
            key_pos = (rr & (V7X_SUBLANES - 1)) * nrg + (rr >> 3)
            causal = key_pos < tt
            beta = jnp.where(causal, beta, 0.0)
            keep = jnp.where(causal, keep, 1.0)
        run8 = jnp.ones((V7X_SUBLANES, t), F32)
        a_loc = [None] * nrg
        for g in range(nrg - 1, -1, -1):
            rows = slice(g * V7X_SUBLANES, (g + 1) * V7X_SUBLANES)
            a_loc[g] = beta[rows] * run8
            run8 = run8 * keep[rows]
        return a_loc, run8

    def finish(hh, j, scan, carry):
        a_loc, run8 = scan
        run = carry
        offs = [None] * V7X_SUBLANES
        for u in range(V7X_SUBLANES - 1, -1, -1):
            offs[u] = run
            run = run * run8[u:u + 1]
        off8 = jnp.concatenate(offs, axis=0)
        a = jnp.concatenate([al * off8 for al in a_loc], axis=0)
        acc_ref[hh] += jnp.dot(vt_ref[hh, j], a.astype(BF16), preferred_element_type=F32)
        return run

    j_prev = jnp.maximum(i - 1, 0)
    has_prev = (i > 0).astype(F32)
    carries = []
    for hh in range(n_h):
        scan_diag = local_scan(hh, i, True)
        scan_prev = local_scan(hh, j_prev, False)
        carry = finish(hh, i, scan_diag, jnp.ones((1, t), F32)) * has_prev
        carries.append(finish(hh, j_prev, scan_prev, carry))

    def cond(state):
        j, cs = state
        alive = jnp.max(functools.reduce(jnp.maximum, cs)) > 0.0
        return jnp.logical_and(j >= 0, alive)

    def body(state):
        j, cs = state
        return j - 1, tuple(finish(hh, j, local_scan(hh, j, False), cs[hh]) for hh in range(n_h))

    lax.while_loop(cond, body, (i - 2, tuple(carries)))
    for hh in range(n_h):
        head = _head_rms_t(acc_ref[hh], gt_ref[hh])
        o_ref[:, hh * HEAD_DIM:(hh + 1) * HEAD_DIM] = head.T.astype(o_ref.dtype)


def _sb(q, k_perm, v_t, g_t):
    n_heads, s, _ = q.shape
    t = SB_T
    nb = s // t
    n_h = SB_HEADS if n_heads % SB_HEADS == 0 else 1
    return pl.pallas_call(
        _sb_kernel,
        grid=(n_heads // n_h, nb),
        in_specs=[
            pl.BlockSpec((n_h, t, HEAD_DIM), lambda h, i: (h, i, 0)),
            pl.BlockSpec((n_h, nb, t, HEAD_DIM), lambda h, i: (h, 0, 0, 0)),
            pl.BlockSpec((n_h, nb, HEAD_DIM, t), lambda h, i: (h, 0, 0, 0)),
            pl.BlockSpec((n_h, HEAD_DIM, V7X_LANES), lambda h, i: (h, 0, 0)),
        ],
        out_specs=pl.BlockSpec((t, n_h * HEAD_DIM), lambda h, i: (i, h)),
        out_shape=jax.ShapeDtypeStruct((s, n_heads * HEAD_DIM), BF16),
        scratch_shapes=[pltpu.VMEM((n_h, HEAD_DIM, t), F32)],
        compiler_params=_cparams(("parallel", "parallel"),
                                 n_h * (8 * s * HEAD_DIM * 2 + 40 * t * t * 4)),
        name="stick_breaking",
    )(q, k_perm, v_t, g_t)


def _cb_kernel(q_ref, k_ref, v_ref, b_ref, g_ref, o_ref, *, win, pad):
    tq = CB_TQ
    i = pl.program_id(1)
    off = pl.multiple_of(i * tq, tq)
    col = lax.broadcasted_iota(I32, (tq, win), 1)
    in_seq = col + off >= pad
    for hh in range(q_ref.shape[0]):
        k_w = k_ref[hh, pl.ds(off, win), :]
        v_w = v_ref[hh, pl.ds(off, win), :]
        s = lax.dot_general(q_ref[hh], k_w, (((1,), (1,)), ((), ())), preferred_element_type=F32)
        s = jnp.where(in_seq, s + b_ref[hh], NEG_BIG)
        m = jnp.max(s, axis=-1, keepdims=True)
        p = jnp.exp(s - m)
        l = jnp.sum(p, axis=-1, keepdims=True)
        o = jnp.dot(p.astype(BF16), v_w, preferred_element_type=F32) / l
        cols = slice(hh * HEAD_DIM, (hh + 1) * HEAD_DIM)
        ms = jnp.mean(o * o, axis=-1, keepdims=True)
        o_ref[:, cols] = (o * lax.rsqrt(ms + EPS) * g_ref[:, cols]).astype(o_ref.dtype)


def _cb(q, k_pad, v_pad, bias, gain, pad):
    n_heads, s, _ = q.shape
    tq = CB_TQ
    win = pad + tq
    sp = k_pad.shape[1]
    n_h = CB_HEADS if n_heads % CB_HEADS == 0 else 1
    return pl.pallas_call(
        functools.partial(_cb_kernel, win=win, pad=pad),
        grid=(n_heads // n_h, s // tq),
        in_specs=[
            pl.BlockSpec((n_h, tq, HEAD_DIM), lambda h, i: (h, i, 0)),
            pl.BlockSpec((n_h, sp, HEAD_DIM), lambda h, i: (h, 0, 0)),
            pl.BlockSpec((n_h, sp, HEAD_DIM), lambda h, i: (h, 0, 0)),
            pl.BlockSpec((n_h, tq, win), lambda h, i: (h, 0, 0)),
            pl.BlockSpec((1, n_h * HEAD_DIM), lambda h, i: (0, h)),
        ],
        out_specs=pl.BlockSpec((tq, n_h * HEAD_DIM), lambda h, i: (i, h)),
        out_shape=jax.ShapeDtypeStruct((s, n_heads * HEAD_DIM), BF16),
        compiler_params=_cparams(("parallel", "parallel"),
                                 n_h * (8 * sp * HEAD_DIM * 2 + 16 * tq * win * 4)),
        name="chunk_band",
    )(q, k_pad, v_pad, bias, gain.reshape(1, n_heads * HEAD_DIM))


def _cb_bias(rel_bias):
    tq = CB_TQ
    pad = N_LEFT_CHUNKS * CHUNK
    win = pad + tq
    n_heads = rel_bias.shape[0]
    qi = np.arange(tq)[:, None]
    kj = np.arange(win)[None, :]
    q_chunk = qi // CHUNK
    k_chunk = kj // CHUNK - N_LEFT_CHUNKS
    in_band = (k_chunk <= q_chunk) & (k_chunk >= q_chunk - N_LEFT_CHUNKS)
    p = win + tq
    n_clipped = pad + tq - 1 - REL_CLIP
    lo = 2 * REL_CLIP - (p - 1 - n_clipped)
    assert lo >= 0 and pad - (win - 1) >= -REL_CLIP
    u = jnp.concatenate([jnp.broadcast_to(rel_bias[:, 2 * REL_CLIP:], (n_heads, n_clipped)),
                         rel_bias[:, lo:2 * REL_CLIP + 1][:, ::-1]], axis=1).astype(F32)
    rows = jnp.tile(u, (1, tq + 1))[:, :tq * (p + 1)].reshape(n_heads, tq, p + 1)[:, :, :win]
    bias = rows[:, ::-1, :]
    return jnp.where(jnp.asarray(in_band)[None], bias, NEG_BIG)


def _out_proj_kernel(a_ref, b_ref, c_ref, w_ref, r_ref, g_ref, o_ref):
    acc, row = None, 0
    for x_ref in (a_ref, b_ref, c_ref):
        part = _dot(x_ref[...], w_ref[row:row + x_ref.shape[1], :])
        acc = part if acc is None else acc + part
        row += x_ref.shape[1]
    o_ref[...] = r_ref[...] + g_ref[...] * acc


def _out_proj(o_a, o_b, o_c, w, l, resid, gate):
    m = o_a.shape[0]
    k, n = w.shape[1], w.shape[2]
    assert o_a.shape[1] + o_b.shape[1] + o_c.shape[1] == k
    tm, tn, tk = _mm_tiles(m, k, n, w.dtype.itemsize)
    assert tk == k
    lhs = lambda x: pl.BlockSpec((tm, x.shape[1]), lambda i, j: (i, 0))
    return pl.pallas_call(
        _out_proj_kernel,
        grid=(m // tm, n // tn),
        in_specs=[lhs(o_a), lhs(o_b), lhs(o_c),
                  pl.BlockSpec((None, k, tn), lambda i, j: (l, 0, j)),
                  pl.BlockSpec((tm, tn), lambda i, j: (i, j)),
                  pl.BlockSpec((1, tn), lambda i, j: (0, j))],
        out_specs=pl.BlockSpec((tm, tn), lambda i, j: (i, j)),
        out_shape=jax.ShapeDtypeStruct((m, n), F32),
        compiler_params=_cparams(("parallel", "parallel"),
                                 _mm_vmem(tm, tn, tk, w.dtype.itemsize) + 3 * tm * tn * 4),
        name="out_proj",
    )(o_a, o_b, o_c, w, resid, gate.reshape(1, n))


def _pack_kernel(x_ref, o_ref, *, runs):
    end = 0
    for src, dst, height in runs:
        if dst > end:
            o_ref[end:dst, :] = jnp.zeros((dst - end, o_ref.shape[1]), o_ref.dtype)
        o_ref[dst:dst + height, :] = x_ref[src:src + height, :].astype(o_ref.dtype)
        end = dst + height
    if end < o_ref.shape[0]:
        o_ref[end:, :] = jnp.zeros((o_ref.shape[0] - end, o_ref.shape[1]), o_ref.dtype)


def _cast_kernel(x_ref, o_ref):
    o_ref[...] = x_ref[...].astype(o_ref.dtype)


def _cast_layer(w, l, dtype):
    _, k, n = w.shape
    tr = _largest_tile(k, 16, 256)
    out = pl.pallas_call(
        _cast_kernel,
        grid=(k // tr,),
        in_specs=[pl.BlockSpec((None, tr, n), lambda i: (l, i, 0))],
        out_specs=pl.BlockSpec((tr, n), lambda i: (i, 0)),
        out_shape=jax.ShapeDtypeStruct((k, n), dtype),
        compiler_params=_cparams(("parallel",), 4 * tr * n * 4),
        name="cast_layer",
    )(w)
    return out.reshape(1, k, n)


def _packed_layout(sizes):
    units, src = [], 0
    for idx, width in enumerate(sizes):
        if width >= V7X_LANES or not units or units[-1][2] >= V7X_LANES:
            units.append([[idx], src, width])
        else:
            units[-1][0].append(idx)
            units[-1][2] += width
        src += width
    srcs = np.concatenate([[0], np.cumsum(sizes)[:-1]])
    offsets, runs, dst = [0] * len(sizes), [], 0
    for members, usrc, uwidth in sorted(units, key=lambda u: -u[2]):
        padded = -(-uwidth // V7X_LANES) * V7X_LANES
        dst = -(-dst // padded) * padded
        for idx in members:
            offsets[idx] = dst + int(srcs[idx]) - usrc
        runs.append((usrc, dst, uwidth))
        dst += padded
    return offsets, tuple(runs), -(-dst // PROJ_N_ALIGN) * PROJ_N_ALIGN


def _pack_w_in(w_t, l, sizes):
    offsets, runs, n_out = _packed_layout(sizes)
    assert all(v % 16 == 0 for run in runs for v in run), "segments must be bf16-tile aligned"
    _, n_in, k = w_t.shape
    tc = _largest_tile(k, V7X_LANES, 256)
    packed = pl.pallas_call(
        functools.partial(_pack_kernel, runs=runs),
        grid=(k // tc,),
        in_specs=[pl.BlockSpec((None, n_in, tc), lambda i: (l, 0, i))],
        out_specs=pl.BlockSpec((n_out, tc), lambda i: (0, i)),
        out_shape=jax.ShapeDtypeStruct((n_out, k), BF16),
        compiler_params=_cparams(("parallel",), 3 * n_in * tc * 4 + 2 * n_out * tc * 2),
        name="pack_w_in",
    )(w_t)
    return packed.reshape(1, n_out, k), offsets


def kernel(x, c, w_ada, b_ada, norm_attn_g, w_in, rel_bias, head_norm_g, w_out, norm_ffn_g,
           w_gate_up, w_down, final_norm_g):
    b, s, d = x.shape
    assert b == 1, "single-sequence prefill only"
    depth = w_ada.shape[0]
    n_heads = d // HEAD_DIM
    ha = 3 * n_heads // 8
    hb = 3 * n_heads // 8
    hc = n_heads - ha - hb
    assert ha % N_KV_A == 0 and s % SB_T == 0
    sizes = (ha * HEAD_DIM, N_KV_A * HEAD_DIM, N_KV_A * HEAD_DIM, IDX_HEADS * IDX_DIM, IDX_DIM,
             IDX_HEADS, hb * HEAD_DIM, hb * HEAD_DIM, hb * HEAD_DIM,
             hc * HEAD_DIM, hc * HEAD_DIM, hc * HEAD_DIM)
    assert w_in.shape[2] == sum(sizes)
    topk = min(IDX_TOPK_MAX, s // 4)
    scale = HEAD_DIM ** -0.5
    pad = N_LEFT_CHUNKS * CHUNK

    rope128 = _rope_tables(s, HEAD_DIM, 1)
    rope64 = _rope_tables(s, IDX_DIM, V7X_LANES // IDX_DIM)

    xs = x.reshape(s, d)
    mod = _adaln(c, w_ada, b_ada)
    w_in_t = jnp.swapaxes(w_in, 1, 2)

    for l in range(depth):
        sh1, sc1, g1, sh2, sc2, g2 = [mod[l, :, n * d:(n + 1) * d] for n in range(6)]

        h = _norm(xs, norm_attn_g[l], sc1, sh1)
        w_packed, offs = _pack_w_in(w_in_t, l, sizes)
        o_qa, o_ka, o_va, o_iq, o_ik, o_iw, o_qb, o_kb, o_vb, o_qc, o_kc, o_vc = offs
        assert o_iw == o_ik + IDX_DIM and o_ik % V7X_LANES == 0
        proj = _matmul_nt(h, w_packed, 0, F32)

        qa = _prep_heads(proj, o_qa, ha, rope128, HEAD_DIM, scale * math.log2(math.e))
        ka = _prep_heads(proj, o_ka, N_KV_A, rope128, HEAD_DIM)
        va = _prep_heads(proj, o_va, N_KV_A)
        vat = va.reshape(N_KV_A, s // DSA_TK, DSA_TK, HEAD_DIM).swapaxes(2, 3)
        iq = _prep_heads(proj, o_iq, IDX_HEADS * IDX_DIM // V7X_LANES, rope64, IDX_DIM,
                         head_major=False)
        ik2, iw = _prep_ikw(proj, o_ik, rope64)
        gain = head_norm_g[l]
        gain_t = jnp.broadcast_to(gain.reshape(n_heads, HEAD_DIM, 1), (n_heads, HEAD_DIM, V7X_LANES))
        o_a = _dsa(iq, iw[:, :IDX_HEADS].T, ik2, qa, ka, vat, gain_t[:ha], topk)

        qb = _prep_heads(proj, o_qb, hb, scale=scale)
        kb = _prep_heads(proj, o_kb, hb)
        vb = _prep_heads(proj, o_vb, hb)
        nb, sub = s // SB_T, V7X_SUBLANES
        kb = kb.reshape(hb, nb, sub, SB_T // sub, HEAD_DIM).swapaxes(2, 3).reshape(hb, nb, SB_T, HEAD_DIM)
        vbt = vb.reshape(hb, nb, sub, SB_T // sub, HEAD_DIM).swapaxes(2, 3).reshape(hb, nb, SB_T, HEAD_DIM)
        vbt = vbt.swapaxes(2, 3)
        o_b = _sb(qb, kb, vbt, gain_t[ha:ha + hb])

        qc = _prep_heads(proj, o_qc, hc, scale=scale)
        kc = _prep_heads(proj, o_kc, hc)
        vc = _prep_heads(proj, o_vc, hc)
        kc = jnp.pad(kc, ((0, 0), (pad, 0), (0, 0)))
        vc = jnp.pad(vc, ((0, 0), (pad, 0), (0, 0)))
        o_c = _cb(qc, kc, vc, _cb_bias(rel_bias[l]), gain[(ha + hb) * HEAD_DIM:], pad)

        xs = _out_proj(o_a, o_b, o_c, w_out, l, xs, g1)

        h = _norm(xs, norm_ffn_g[l], sc2, sh2)
        act = _matmul_swiglu(h, w_gate_up, l)
        xs = _matmul_resid(act, _cast_layer(w_down, l, BF16), 0, xs, g2)

    out = _norm(xs, final_norm_g, out_dtype=x.dtype)
    return out.reshape(b, s, d)
```

```python
import functools
import math

import numpy as np
import jax
import jax.numpy as jnp
from jax import lax
from jax.experimental import pallas as pl
from jax.experimental.pallas import tpu as pltpu

F32 = jnp.float32
BF16 = jnp.bfloat16
I32 = jnp.int32

HEAD_DIM = 128
N_KV_A = 2
IDX_HEADS = 16
IDX_DIM = 64
IDX_TOPK_MAX = 256
CHUNK = 64
N_LEFT_CHUNKS = 8
REL_CLIP = 256
ROPE_THETA = 10000.0
EPS = 1e-6

V7X_LANES = 128
V7X_SUBLANES = 8
V7X_BF16_SUBLANES = 16
V7X_VMEM_BYTES = 64 * 1024 * 1024
V7X_MXU_COLS = 256

PROJ_N_ALIGN = 2 * V7X_MXU_COLS

NEG_BIG = -1e30
INT_MIN = -(2 ** 31)

DSA_TQ = 256
DSA_TK = 512
SB_T = 256
SB_HEADS = 4
CB_TQ = 256
CB_HEADS = 4


def _cparams(semantics, vmem_bytes):
    limit = int(min(V7X_VMEM_BYTES * 7 // 8, max(vmem_bytes, 16 * 1024 * 1024)))
    return pltpu.CompilerParams(dimension_semantics=semantics, vmem_limit_bytes=limit)


def _largest_tile(n, unit, cap):
    best = None
    t = unit
    while t <= min(n, cap):
        if n % t == 0:
            best = t
        t += unit
    assert best is not None, (n, unit, cap)
    return best


def _adaln_kernel(c_ref, w_ref, b_ref, o_ref, acc_ref):
    k = pl.program_id(1)
    c = c_ref[...]
    cond = c * jax.nn.sigmoid(c)
    tr, n = w_ref.shape

    @pl.when(k == 0)
    def _():
        acc_ref[...] = jnp.zeros_like(acc_ref)

    for j in range(n // V7X_LANES):
        cols = slice(j * V7X_LANES, (j + 1) * V7X_LANES)
        prod = w_ref[:, cols] * cond
        part = prod[0:V7X_SUBLANES]
        for g in range(1, tr // V7X_SUBLANES):
            part = part + prod[g * V7X_SUBLANES:(g + 1) * V7X_SUBLANES]
        acc_ref[:, cols] += part

    @pl.when(k == pl.num_programs(1) - 1)
    def _():
        o_ref[...] = jnp.sum(acc_ref[...], axis=0, keepdims=True) + b_ref[...]


def _adaln(c, w_ada, b_ada):
    depth, d, n = w_ada.shape
    tr = _largest_tile(d, V7X_SUBLANES, 128)
    c_b = jnp.broadcast_to(c.reshape(d, 1), (d, V7X_LANES))
    out = pl.pallas_call(
        _adaln_kernel,
        grid=(depth, d // tr),
        in_specs=[
            pl.BlockSpec((tr, V7X_LANES), lambda l, k: (k, 0)),
            pl.BlockSpec((None, tr, n), lambda l, k: (l, k, 0)),
            pl.BlockSpec((None, 1, n), lambda l, k: (l, 0, 0)),
        ],
        out_specs=pl.BlockSpec((None, 1, n), lambda l, k: (l, 0, 0)),
        out_shape=jax.ShapeDtypeStruct((depth, 1, n), F32),
        scratch_shapes=[pltpu.VMEM((V7X_SUBLANES, n), F32)],
        compiler_params=_cparams(("parallel", "arbitrary"), 3 * tr * n * 4),
        name="adaln",
    )(c_b, w_ada, b_ada.reshape(depth, 1, n))
    return out


def _norm_kernel(x_ref, g_ref, *rest, modulate):
    o_ref = rest[-1]
    x = x_ref[...]
    ms = jnp.mean(x * x, axis=-1, keepdims=True)
    y = x * lax.rsqrt(ms + EPS) * g_ref[...]
    if modulate:
        sc_ref, sh_ref = rest[0], rest[1]
        y = y * (1.0 + sc_ref[...]) + sh_ref[...]
    o_ref[...] = y.astype(o_ref.dtype)


def _norm(x, g, scale=None, shift=None, out_dtype=BF16):
    s, d = x.shape
    ts = _largest_tile(s, V7X_SUBLANES, 256)
    modulate = scale is not None
    row = pl.BlockSpec((1, d), lambda i: (0, 0))
    args = [x, g.reshape(1, d)]
    specs = [pl.BlockSpec((ts, d), lambda i: (i, 0)), row]
    if modulate:
        args += [scale.reshape(1, d), shift.reshape(1, d)]
        specs += [row, row]
    return pl.pallas_call(
        functools.partial(_norm_kernel, modulate=modulate),
        grid=(s // ts,),
        in_specs=specs,
        out_specs=pl.BlockSpec((ts, d), lambda i: (i, 0)),
        out_shape=jax.ShapeDtypeStruct((s, d), out_dtype),
        compiler_params=_cparams(("parallel",), 6 * ts * d * 4),
        name="rmsnorm",
    )(*args)


def _dot(a, w):
    return jnp.dot(a, w.astype(BF16), preferred_element_type=F32)


def _mm_plain_kernel(a_ref, wt_ref, o_ref):
    o_ref[...] = lax.dot_general(a_ref[...], wt_ref[...].astype(BF16), (((1,), (1,)), ((), ())),
                                 preferred_element_type=F32).astype(o_ref.dtype)


def _mm_swiglu_kernel(a_ref, wg_ref, wu_ref, o_ref):
    a = a_ref[...]
    gate = _dot(a, wg_ref[...])
    up = _dot(a, wu_ref[...])
    o_ref[...] = (gate * jax.nn.sigmoid(gate) * up).astype(o_ref.dtype)


def _mm_resid_kernel(a_ref, w_ref, r_ref, g_ref, o_ref, *acc, nk):
    if nk == 1:
        o_ref[...] = r_ref[...] + g_ref[...] * _dot(a_ref[...], w_ref[...])
        return
    acc_ref, = acc
    k = pl.program_id(2)

    @pl.when(k == 0)
    def _():
        acc_ref[...] = jnp.zeros_like(acc_ref)

    acc_ref[...] += _dot(a_ref[...], w_ref[...])

    @pl.when(k == nk - 1)
    def _():
        o_ref[...] = r_ref[...] + g_ref[...] * acc_ref[...]


def _mm_tiles(m, k, n, w_itemsize):
    tn = _largest_tile(n, V7X_LANES, 768 if w_itemsize == 2 else 256)
    if k <= 4096:
        return _largest_tile(m, V7X_SUBLANES, 1024), tn, k
    if w_itemsize == 2:
        return _largest_tile(m, V7X_SUBLANES, 512), _largest_tile(n, V7X_LANES, 256), k
    return _largest_tile(m, V7X_SUBLANES, 1024), tn, _largest_tile(k, V7X_LANES, 5632)


def _mm_vmem(tm, tn, tk, w_itemsize, n_w=1):
    return (2 * tm * tk * 2 + n_w * (2 * tk * tn * w_itemsize + tk * tn * 2)
            + (4 + 2 * n_w) * tm * tn * 4)


def _matmul_nt(a, w_t, l, out_dtype):
    m, k = a.shape
    n = w_t.shape[1]
    tm, tn, tk = _mm_tiles(m, k, n, w_t.dtype.itemsize)
    assert tk == k
    return pl.pallas_call(
        _mm_plain_kernel,
        grid=(m // tm, n // tn),
        in_specs=[pl.BlockSpec((tm, k), lambda i, j: (i, 0)),
                  pl.BlockSpec((None, tn, k), lambda i, j: (l, j, 0))],
        out_specs=pl.BlockSpec((tm, tn), lambda i, j: (i, j)),
        out_shape=jax.ShapeDtypeStruct((m, n), out_dtype),
        compiler_params=_cparams(("parallel", "parallel"), _mm_vmem(tm, tn, tk, w_t.dtype.itemsize)),
        name="matmul",
    )(a, w_t)


def _matmul_swiglu(a, w_gate_up, l):
    m, k = a.shape
    n = w_gate_up.shape[2] // 2
    tm, tn, tk = _mm_tiles(m, k, n, w_gate_up.dtype.itemsize)
    assert tk == k
    nb = n // tn
    return pl.pallas_call(
        _mm_swiglu_kernel,
        grid=(m // tm, nb),
        in_specs=[pl.BlockSpec((tm, k), lambda i, j: (i, 0)),
                  pl.BlockSpec((None, k, tn), lambda i, j: (l, 0, j)),
                  pl.BlockSpec((None, k, tn), lambda i, j: (l, 0, j + nb))],
        out_specs=pl.BlockSpec((tm, tn), lambda i, j: (i, j)),
        out_shape=jax.ShapeDtypeStruct((m, n), BF16),
        compiler_params=_cparams(("parallel", "parallel"),
                                 _mm_vmem(tm, tn, tk, w_gate_up.dtype.itemsize, n_w=2)),
        name="matmul_swiglu",
    )(a, w_gate_up, w_gate_up)


def _matmul_resid(a, w, l, resid, gate):
    m, k = a.shape
    n = w.shape[2]
    tm, tn, tk = _mm_tiles(m, k, n, w.dtype.itemsize)
    nk = k // tk
    scratch = [pltpu.VMEM((tm, tn), F32)] if nk > 1 else []
    return pl.pallas_call(
        functools.partial(_mm_resid_kernel, nk=nk),
        grid=(m // tm, n // tn, nk),
        in_specs=[pl.BlockSpec((tm, tk), lambda i, j, kk: (i, kk)),
                  pl.BlockSpec((None, tk, tn), lambda i, j, kk: (l, kk, j)),
                  pl.BlockSpec((tm, tn), lambda i, j, kk: (i, j)),
                  pl.BlockSpec((1, tn), lambda i, j, kk: (0, j))],
        out_specs=pl.BlockSpec((tm, tn), lambda i, j, kk: (i, j)),
        out_shape=jax.ShapeDtypeStruct((m, n), F32),
        scratch_shapes=scratch,
        compiler_params=_cparams(("parallel", "parallel", "arbitrary"),
                                 _mm_vmem(tm, tn, tk, w.dtype.itemsize) + 3 * tm * tn * 4),
        name="matmul_resid",
    )(a, w, resid, gate.reshape(1, n))


def _rope_tables(s, d, lane_groups):
    pos = jnp.arange(s, dtype=jnp.int32).astype(F32)
    inv = ROPE_THETA ** (-jnp.arange(0, d, 2, dtype=F32) / d)
    ang = pos[:, None] * inv[None, :]
    cos, sin = jnp.cos(ang), jnp.sin(ang)
    cos_t = jnp.tile(jnp.concatenate([cos, cos], axis=1), (1, lane_groups))
    sin_t = jnp.tile(jnp.concatenate([-sin, sin], axis=1), (1, lane_groups))
    return cos_t, sin_t


def _rot_half(x, d):
    if d == V7X_LANES:
        return pltpu.roll(x, d // 2, 1)
    lane = lax.broadcasted_iota(I32, x.shape, 1)
    first = (lane & (d - 1)) < d // 2
    return jnp.where(first, pltpu.roll(x, V7X_LANES - d // 2, 1), pltpu.roll(x, d // 2, 1))


def _prep_kernel(x_ref, *rest, rope_dim, scale, head_major):
    o_ref = rest[-1]
    if rope_dim:
        cos, sin = rest[0][...], rest[1][...]
    for h in range(x_ref.shape[1] // V7X_LANES):
        cols = slice(h * V7X_LANES, (h + 1) * V7X_LANES)
        x = x_ref[:, cols]
        if rope_dim:
            x = x * cos + _rot_half(x, rope_dim) * sin
        if scale != 1.0:
            x = x * scale
        if head_major:
            o_ref[h] = x.astype(o_ref.dtype)
        else:
            o_ref[:, cols] = x.astype(o_ref.dtype)


def _prep_heads(proj, col0, n_blocks, tables=None, rope_dim=0, scale=1.0, head_major=True):
    s = proj.shape[0]
    width = n_blocks * V7X_LANES
    assert col0 % width == 0 or n_blocks == 1, "segment must start at a multiple of its width"
    ts = _largest_tile(s, V7X_SUBLANES, 512)
    args = [proj]
    specs = [pl.BlockSpec((ts, width), lambda i: (i, col0 // width))]
    if rope_dim:
        args += list(tables)
        specs += [pl.BlockSpec((ts, V7X_LANES), lambda i: (i, 0))] * 2
    if head_major:
        out_spec = pl.BlockSpec((n_blocks, ts, V7X_LANES), lambda i: (0, i, 0))
        out_shape = jax.ShapeDtypeStruct((n_blocks, s, V7X_LANES), BF16)
    else:
        out_spec = pl.BlockSpec((ts, width), lambda i: (i, 0))
        out_shape = jax.ShapeDtypeStruct((s, width), BF16)
    return pl.pallas_call(
        functools.partial(_prep_kernel, rope_dim=rope_dim, scale=scale, head_major=head_major),
        grid=(s // ts,),
        in_specs=specs,
        out_specs=out_spec,
        out_shape=out_shape,
        compiler_params=_cparams(("parallel",), 8 * ts * width * 4),
        name="prep_heads",
    )(*args)


def _ikw_kernel(x_ref, cos_ref, sin_ref, ik_ref, iw_ref):
    x = x_ref[...]
    y = x * cos_ref[...] + _rot_half(x, IDX_DIM) * sin_ref[...]
    lane = lax.broadcasted_iota(I32, x.shape, 1)
    ik_ref[...] = jnp.where(lane < IDX_DIM, y, pltpu.roll(y, IDX_DIM, 1)).astype(ik_ref.dtype)
    iw_ref[...] = pltpu.roll(x, V7X_LANES - IDX_DIM, 1)


def _prep_ikw(proj, col0, tables):
    s = proj.shape[0]
    ts = _largest_tile(s, V7X_SUBLANES, 1024)
    cb = col0 // V7X_LANES
    blk = pl.BlockSpec((ts, V7X_LANES), lambda i: (i, 0))
    return pl.pallas_call(
        _ikw_kernel,
        grid=(s // ts,),
        in_specs=[pl.BlockSpec((ts, V7X_LANES), lambda i: (i, cb)), blk, blk],
        out_specs=[blk, blk],
        out_shape=[jax.ShapeDtypeStruct((s, V7X_LANES), BF16),
                   jax.ShapeDtypeStruct((s, V7X_LANES), F32)],
        compiler_params=_cparams(("parallel",), 16 * ts * V7X_LANES * 4),
        name="prep_ikw",
    )(proj, *tables)


def _head_rms_t(x_t, g_t):
    ms = jnp.mean(x_t * x_t, axis=0, keepdims=True)
    y = x_t * lax.rsqrt(ms + EPS)
    reps = x_t.shape[1] // g_t.shape[1]
    return y * (g_t if reps == 1 else jnp.concatenate([g_t] * reps, axis=1))


def _dsa_kernel(iq_ref, iwt_ref, ik_ref, q_ref, k_ref, vt_ref, gt_ref, o_ref,
                rhs_scr, key_scr, m_scr, l_scr, acc_scr, s_scr, ma_scr,
                *, topk, rep, idx_scale, n_total):
    tq, tk = DSA_TQ, DSA_TK
    nsub = tk // V7X_SUBLANES
    r_all = rep * tq
    i = pl.program_id(0)
    start = i * tq
    n_kt = (start + tq + tk - 1) >> int(math.log2(tk))

    lane = lax.broadcasted_iota(I32, (tq, V7X_LANES), 1)
    for p in range(IDX_HEADS // 2):
        pair = iq_ref[:, p * V7X_LANES:(p + 1) * V7X_LANES]
        zero = jnp.zeros_like(pair)
        rhs_scr[p, :tq] = jnp.where(lane < IDX_DIM, pair, zero)
        rhs_scr[p, tq:] = jnp.where(lane >= IDX_DIM, pair, zero)

    iwt = iwt_ref[...]
    key_row = lax.broadcasted_iota(I32, (tk, tq), 0)
    q_lane = lax.broadcasted_iota(I32, (tk, tq), 1)
    key_lim = start + ((q_lane >> int(math.log2(CHUNK))) + 1) * CHUNK

    def score_body(kt, carry):
        off = pl.multiple_of(kt * tk, tk)
        ik_t = ik_ref[pl.ds(off, tk), :]
        acc = jnp.zeros((tk, tq), F32)
        for p in range(IDX_HEADS // 2):
            d = lax.dot_general(ik_t, rhs_scr[p], (((1,), (1,)), ((), ())),
                                preferred_element_type=F32)
            for half in range(2):
                h = 2 * p + half
                acc = acc + iwt[h:h + 1, :] * jnp.maximum(d[:, half * tq:(half + 1) * tq], 0.0)
        bits = pltpu.bitcast(acc * idx_scale, I32)
        key = jnp.where(bits < 0, bits ^ 0x7FFFFFFF, bits)
        key_scr[kt] = jnp.where(key_row + off < key_lim, key, INT_MIN)
        return carry

    lax.fori_loop(0, n_kt, score_body, 0)

    n_part = 4

    def count_if(pred):
        def body(kt, parts):
            hit = jnp.where(pred(key_scr[kt], kt), 1, 0)
            parts = list(parts)
            for c in range(nsub):
                parts[c % n_part] = parts[c % n_part] + hit[c * V7X_SUBLANES:(c + 1) * V7X_SUBLANES]
            return tuple(parts)

        parts = lax.fori_loop(0, n_kt, body,
                              tuple(jnp.zeros((V7X_SUBLANES, tq), I32) for _ in range(n_part)))
        return jnp.sum(sum(parts[1:], parts[0]), axis=0, keepdims=True)

    zero = jnp.zeros((1, tq), I32)
    n_pos = count_if(lambda keys, kt: keys >= zero)
    thr = jnp.where(n_pos >= topk, zero, zero + INT_MIN)

    def bisect(b, state):
        thr, n_ge = state
        cand = thr + lax.shift_left(jnp.int32(1), 30 - b)
        n_cand = count_if(lambda keys, kt: keys >= cand)
        ok = n_cand >= topk
        return jnp.where(ok, cand, thr), jnp.where(ok, n_cand, n_ge)

    thr, n_ge = lax.fori_loop(0, 31, bisect, (thr, n_pos))

    tied = jnp.logical_and(n_ge > topk, thr > INT_MIN)

    @pl.when(jnp.max(jnp.where(tied, 1, 0)) > 0)
    def _():
        n_idx_bits = max(1, (n_total - 1).bit_length())
        need = topk - count_if(lambda keys, kt: keys > thr)

        def n_tied_before(limit):
            return count_if(lambda keys, kt: jnp.logical_and(keys == thr, key_row + kt * tk < limit))

        def idx_bisect(b, last):
            cand = last + lax.shift_left(jnp.int32(1), n_idx_bits - 1 - b)
            return jnp.where(n_tied_before(cand) < need, cand, last)

        last = lax.fori_loop(0, n_idx_bits, idx_bisect, zero)

        def demote(kt, carry):
            keys = key_scr[kt]
            drop = jnp.logical_and(jnp.logical_and(tied, keys == thr), key_row + kt * tk > last)
            key_scr[kt] = jnp.where(drop, keys - 1, keys)
            return carry

        lax.fori_loop(0, n_kt, demote, 0)

    thr = jnp.maximum(thr, INT_MIN + 1)

    m_scr[...] = jnp.full(m_scr.shape, NEG_BIG, F32)
    l_scr[...] = jnp.zeros(l_scr.shape, F32)
    acc_scr[...] = jnp.zeros(acc_scr.shape, F32)

    def stage_a(kt):
        slot = kt & 1
        off = pl.multiple_of(kt * tk, tk)
        bias = jnp.where(key_scr[kt] >= thr, 0.0, NEG_BIG)
        bias = jnp.concatenate([bias] * rep, axis=1)
        for g in range(N_KV_A):
            qg = q_ref[g * rep:(g + 1) * rep].reshape(r_all, HEAD_DIM)
            s = lax.dot_general(k_ref[g, pl.ds(off, tk), :], qg, (((1,), (1,)), ((), ())),
                                preferred_element_type=F32) + bias
            m_old = m_scr[g]
            m_new = jnp.maximum(m_old, jnp.max(s, axis=0, keepdims=True))
            s_scr[slot, g] = s
            ma_scr[slot, g, 0:1] = m_new
            ma_scr[slot, g, 1:2] = jnp.exp2(m_old - m_new)
            m_scr[g] = m_new

    def stage_b(kt):
        slot = kt & 1
        for g in range(N_KV_A):
            m_new = ma_scr[slot, g, 0:1]
            alpha = ma_scr[slot, g, 1:2]
            p = jnp.exp2(s_scr[slot, g] - m_new)
            l_scr[g] = alpha * l_scr[g] + jnp.sum(p, axis=0, keepdims=True)
            acc_scr[g] = alpha * acc_scr[g] + jnp.dot(vt_ref[g, kt], p.astype(BF16),
                                                      preferred_element_type=F32)

    stage_a(0)

    def att_body(kt, carry):
        stage_b(kt - 1)
        stage_a(kt)
        return carry

    lax.fori_loop(1, n_kt, att_body, 0)
    stage_b(n_kt - 1)
    for g in range(N_KV_A):
        out_t = acc_scr[g] / l_scr[g]
        for r in range(rep):
            hh = g * rep + r
            head = _head_rms_t(out_t[:, r * tq:(r + 1) * tq], gt_ref[hh])
            o_ref[:, hh * HEAD_DIM:(hh + 1) * HEAD_DIM] = head.T.astype(o_ref.dtype)


def _dsa(iq, iwt, ik2, qa, ka, vat, g_t, topk):
    n_heads, s, _ = qa.shape
    rep = n_heads // N_KV_A
    tq, tk = DSA_TQ, DSA_TK
    assert s % tk == 0
    idx_scale = (IDX_HEADS ** -0.5) * (IDX_DIM ** -0.5)
    vmem = (4 * N_KV_A * s * HEAD_DIM * 2 * 2 + 2 * s * V7X_LANES * 2 + tq * s * 4
            + 32 * rep * tq * tk * 4)
    return pl.pallas_call(
        functools.partial(_dsa_kernel, topk=topk, rep=rep, idx_scale=idx_scale, n_total=s),
        grid=(s // tq,),
        in_specs=[
            pl.BlockSpec((tq, IDX_HEADS * IDX_DIM), lambda i: (i, 0)),
            pl.BlockSpec((IDX_HEADS, tq), lambda i: (0, i)),
            pl.BlockSpec((s, V7X_LANES), lambda i: (0, 0)),
            pl.BlockSpec((n_heads, tq, HEAD_DIM), lambda i: (0, i, 0)),
            pl.BlockSpec((N_KV_A, s, HEAD_DIM), lambda i: (0, 0, 0)),
            pl.BlockSpec((N_KV_A, s // tk, HEAD_DIM, tk), lambda i: (0, 0, 0, 0)),
            pl.BlockSpec((n_heads, HEAD_DIM, V7X_LANES), lambda i: (0, 0, 0)),
        ],
        out_specs=pl.BlockSpec((tq, n_heads * HEAD_DIM), lambda i: (i, 0)),
        out_shape=jax.ShapeDtypeStruct((s, n_heads * HEAD_DIM), BF16),
        scratch_shapes=[
            pltpu.VMEM((IDX_HEADS // 2, 2 * tq, V7X_LANES), BF16),
            pltpu.VMEM((s // tk, tk, tq), I32),
            pltpu.VMEM((N_KV_A, 1, rep * tq), F32),
            pltpu.VMEM((N_KV_A, 1, rep * tq), F32),
            pltpu.VMEM((N_KV_A, HEAD_DIM, rep * tq), F32),
            pltpu.VMEM((2, N_KV_A, tk, rep * tq), F32),
            pltpu.VMEM((2, N_KV_A, 2, rep * tq), F32),
        ],
        compiler_params=_cparams(("parallel",), vmem),
        name="dsa",
    )(iq, iwt, ik2, qa, ka, vat, g_t)


def _sb_kernel(q_ref, k_ref, vt_ref, gt_ref, o_ref, acc_ref):
    t = SB_T
    nrg = t // V7X_SUBLANES
    n_h = q_ref.shape[0]
    i = pl.program_id(1)
    acc_ref[...] = jnp.zeros(acc_ref.shape, F32)

    def local_scan(hh, j, diagonal):
        z = lax.dot_general(k_ref[hh, j], q_ref[hh], (((1,), (1,)), ((), ())),
                            preferred_element_type=F32)
        e = jnp.exp(-jnp.abs(z))
        r = 1.0 / (1.0 + e)
        er = e * r
        pos = z >= 0.0
        beta = jnp.where(pos, r, er)
        keep = jnp.where(pos, er, r)
        if diagonal:
            rr = lax.broadcasted_iota(I32, (t, t), 0)
            tt = lax.broadcasted_iota(I32, (t, t), 1)
            key_pos = (rr & (V7X_SUBLANES - 1)) * nrg + (rr >> int(math.log2(V7X_SUBLANES)))
            causal = key_pos < tt
            beta = jnp.where(causal, beta, 0.0)
            keep = jnp.where(causal, keep, 1.0)
        run8 = jnp.ones((V7X_SUBLANES, t), F32)
        a_loc = [None] * nrg
        for g in range(nrg - 1, -1, -1):
            rows = slice(g * V7X_SUBLANES, (g + 1) * V7X_SUBLANES)
            a_loc[g] = beta[rows] * run8
            run8 = run8 * keep[rows]
        return a_loc, run8

    def finish(hh, j, scan, carry):
        a_loc, run8 = scan
        run = carry
        offs = [None] * V7X_SUBLANES
        for u in range(V7X_SUBLANES - 1, -1, -1):
            offs[u] = run
            run = run * run8[u:u + 1]
        off8 = jnp.concatenate(offs, axis=0)
        a = jnp.concatenate([al * off8 for al in a_loc], axis=0)
        acc_ref[hh] += jnp.dot(vt_ref[hh, j], a.astype(BF16), preferred_element_type=F32)
        return run

    j_prev = jnp.maximum(i - 1, 0)
    has_prev = (i > 0).astype(F32)
    carries = []
    for hh in range(n_h):
        scan_diag = local_scan(hh, i, True)
        scan_prev = local_scan(hh, j_prev, False)
        carry = finish(hh, i, scan_diag, jnp.ones((1, t), F32)) * has_prev
        carries.append(finish(hh, j_prev, scan_prev, carry))

    def cond(state):
        j, cs = state
        alive = jnp.max(functools.reduce(jnp.maximum, cs)) > 0.0
        return jnp.logical_and(j >= 0, alive)

    def body(state):
        j, cs = state
        return j - 1, tuple(finish(hh, j, local_scan(hh, j, False), cs[hh]) for hh in range(n_h))

    lax.while_loop(cond, body, (i - 2, tuple(carries)))
    for hh in range(n_h):
        head = _head_rms_t(acc_ref[hh], gt_ref[hh])
        o_ref[:, hh * HEAD_DIM:(hh + 1) * HEAD_DIM] = head.T.astype(o_ref.dtype)


def _sb(q, k_perm, v_t, g_t):
    n_heads, s, _ = q.shape
    t = SB_T
    nb = s // t
    n_h = SB_HEADS if n_heads % SB_HEADS == 0 else 1
    return pl.pallas_call(
        _sb_kernel,
        grid=(n_heads // n_h, nb),
        in_specs=[
            pl.BlockSpec((n_h, t, HEAD_DIM), lambda h, i: (h, i, 0)),
            pl.BlockSpec((n_h, nb, t, HEAD_DIM), lambda h, i: (h, 0, 0, 0)),
            pl.BlockSpec((n_h, nb, HEAD_DIM, t), lambda h, i: (h, 0, 0, 0)),
            pl.BlockSpec((n_h, HEAD_DIM, V7X_LANES), lambda h, i: (h, 0, 0)),
        ],
        out_specs=pl.BlockSpec((t, n_h * HEAD_DIM), lambda h, i: (i, h)),
        out_shape=jax.ShapeDtypeStruct((s, n_heads * HEAD_DIM), BF16),
        scratch_shapes=[pltpu.VMEM((n_h, HEAD_DIM, t), F32)],
        compiler_params=_cparams(("parallel", "parallel"),
                                 n_h * (8 * s * HEAD_DIM * 2 + 40 * t * t * 4)),
        name="stick_breaking",
    )(q, k_perm, v_t, g_t)


def _cb_kernel(q_ref, k_ref, v_ref, b_ref, g_ref, o_ref, *, win, pad):
    tq = CB_TQ
    i = pl.program_id(1)
    off = pl.multiple_of(i * tq, tq)
    col = lax.broadcasted_iota(I32, (tq, win), 1)
    in_seq = col + off >= pad
    for hh in range(q_ref.shape[0]):
        k_w = k_ref[hh, pl.ds(off, win), :]
        v_w = v_ref[hh, pl.ds(off, win), :]
        s = lax.dot_general(q_ref[hh], k_w, (((1,), (1,)), ((), ())), preferred_element_type=F32)
        s = jnp.where(in_seq, s + b_ref[hh], NEG_BIG)
        m = jnp.max(s, axis=-1, keepdims=True)
        p = jnp.exp(s - m)
        l = jnp.sum(p, axis=-1, keepdims=True)
        o = jnp.dot(p.astype(BF16), v_w, preferred_element_type=F32) / l
        cols = slice(hh * HEAD_DIM, (hh + 1) * HEAD_DIM)
        ms = jnp.mean(o * o, axis=-1, keepdims=True)
        o_ref[:, cols] = (o * lax.rsqrt(ms + EPS) * g_ref[:, cols]).astype(o_ref.dtype)


def _cb(q, k_pad, v_pad, bias, gain, pad):
    n_heads, s, _ = q.shape
    tq = CB_TQ
    win = pad + tq
    sp = k_pad.shape[1]
    n_h = CB_HEADS if n_heads % CB_HEADS == 0 else 1
    return pl.pallas_call(
        functools.partial(_cb_kernel, win=win, pad=pad),
        grid=(n_heads // n_h, s // tq),
        in_specs=[
            pl.BlockSpec((n_h, tq, HEAD_DIM), lambda h, i: (h, i, 0)),
            pl.BlockSpec((n_h, sp, HEAD_DIM), lambda h, i: (h, 0, 0)),
            pl.BlockSpec((n_h, sp, HEAD_DIM), lambda h, i: (h, 0, 0)),
            pl.BlockSpec((n_h, tq, win), lambda h, i: (h, 0, 0)),
            pl.BlockSpec((1, n_h * HEAD_DIM), lambda h, i: (0, h)),
        ],
        out_specs=pl.BlockSpec((tq, n_h * HEAD_DIM), lambda h, i: (i, h)),
        out_shape=jax.ShapeDtypeStruct((s, n_heads * HEAD_DIM), BF16),
        compiler_params=_cparams(("parallel", "parallel"),
                                 n_h * (8 * sp * HEAD_DIM * 2 + 16 * tq * win * 4)),
        name="chunk_band",
    )(q, k_pad, v_pad, bias, gain.reshape(1, n_heads * HEAD_DIM))


def _cb_bias(rel_bias):
    tq = CB_TQ
    pad = N_LEFT_CHUNKS * CHUNK
    win = pad + tq
    n_heads = rel_bias.shape[0]
    qi = np.arange(tq)[:, None]
    kj = np.arange(win)[None, :]
    q_chunk = qi // CHUNK
    k_chunk = kj // CHUNK - N_LEFT_CHUNKS
    in_band = (k_chunk <= q_chunk) & (k_chunk >= q_chunk - N_LEFT_CHUNKS)
    p = win + tq
    n_clipped = pad + tq - 1 - REL_CLIP
    lo = 2 * REL_CLIP - (p - 1 - n_clipped)
    assert lo >= 0 and pad - (win - 1) >= -REL_CLIP
    u = jnp.concatenate([jnp.broadcast_to(rel_bias[:, 2 * REL_CLIP:], (n_heads, n_clipped)),
                         rel_bias[:, lo:2 * REL_CLIP + 1][:, ::-1]], axis=1).astype(F32)
    rows = jnp.tile(u, (1, tq + 1))[:, :tq * (p + 1)].reshape(n_heads, tq, p + 1)[:, :, :win]
    bias = rows[:, ::-1, :]
    return jnp.where(jnp.asarray(in_band)[None], bias, NEG_BIG)


def _out_proj_kernel(a_ref, b_ref, c_ref, w_ref, r_ref, g_ref, o_ref):
    acc, row = None, 0
    for x_ref in (a_ref, b_ref, c_ref):
        part = _dot(x_ref[...], w_ref[row:row + x_ref.shape[1], :])
        acc = part if acc is None else acc + part
        row += x_ref.shape[1]
    o_ref[...] = r_ref[...] + g_ref[...] * acc


def _out_proj(o_a, o_b, o_c, w, l, resid, gate):
    m = o_a.shape[0]
    k, n = w.shape[1], w.shape[2]
    assert o_a.shape[1] + o_b.shape[1] + o_c.shape[1] == k
    tm, tn, tk = _mm_tiles(m, k, n, w.dtype.itemsize)
    assert tk == k
    lhs = lambda x: pl.BlockSpec((tm, x.shape[1]), lambda i, j: (i, 0))
    return pl.pallas_call(
        _out_proj_kernel,
        grid=(m // tm, n // tn),
        in_specs=[lhs(o_a), lhs(o_b), lhs(o_c),
                  pl.BlockSpec((None, k, tn), lambda i, j: (l, 0, j)),
                  pl.BlockSpec((tm, tn), lambda i, j: (i, j)),
                  pl.BlockSpec((1, tn), lambda i, j: (0, j))],
        out_specs=pl.BlockSpec((tm, tn), lambda i, j: (i, j)),
        out_shape=jax.ShapeDtypeStruct((m, n), F32),
        compiler_params=_cparams(("parallel", "parallel"),
                                 _mm_vmem(tm, tn, tk, w.dtype.itemsize) + 3 * tm * tn * 4),
        name="out_proj",
    )(o_a, o_b, o_c, w, resid, gate.reshape(1, n))


def _pack_kernel(x_ref, o_ref, *, runs):
    end = 0
    for src, dst, height in runs:
        if dst > end:
            o_ref[end:dst, :] = jnp.zeros((dst - end, o_ref.shape[1]), o_ref.dtype)
        o_ref[dst:dst + height, :] = x_ref[src:src + height, :].astype(o_ref.dtype)
        end = dst + height
    if end < o_ref.shape[0]:
        o_ref[end:, :] = jnp.zeros((o_ref.shape[0] - end, o_ref.shape[1]), o_ref.dtype)


def _cast_kernel(x_ref, o_ref):
    o_ref[...] = x_ref[...].astype(o_ref.dtype)


def _cast_layer(w, l, dtype):
    _, k, n = w.shape
    tr = _largest_tile(k, V7X_BF16_SUBLANES, 256)
    out = pl.pallas_call(
        _cast_kernel,
        grid=(k // tr,),
        in_specs=[pl.BlockSpec((None, tr, n), lambda i: (l, i, 0))],
        out_specs=pl.BlockSpec((tr, n), lambda i: (i, 0)),
        out_shape=jax.ShapeDtypeStruct((k, n), dtype),
        compiler_params=_cparams(("parallel",), 4 * tr * n * 4),
        name="cast_layer",
    )(w)
    return out.reshape(1, k, n)


def _packed_layout(sizes):
    units, src = [], 0
    for idx, width in enumerate(sizes):
        if width >= V7X_LANES or not units or units[-1][2] >= V7X_LANES:
            units.append([[idx], src, width])
        else:
            units[-1][0].append(idx)
            units[-1][2] += width
        src += width
    srcs = np.concatenate([[0], np.cumsum(sizes)[:-1]])
    offsets, runs, dst = [0] * len(sizes), [], 0
    for members, usrc, uwidth in sorted(units, key=lambda u: -u[2]):
        padded = -(-uwidth // V7X_LANES) * V7X_LANES
        dst = -(-dst // padded) * padded
        for idx in members:
            offsets[idx] = dst + int(srcs[idx]) - usrc
        runs.append((usrc, dst, uwidth))
        dst += padded
    return offsets, tuple(runs), -(-dst // PROJ_N_ALIGN) * PROJ_N_ALIGN


def _pack_w_in(w_t, l, sizes):
    offsets, runs, n_out = _packed_layout(sizes)
    assert all(v % V7X_BF16_SUBLANES == 0 for run in runs for v in run), "segments must be bf16-tile aligned"
    _, n_in, k = w_t.shape
    tc = _largest_tile(k, V7X_LANES, 256)
    packed = pl.pallas_call(
        functools.partial(_pack_kernel, runs=runs),
        grid=(k // tc,),
        in_specs=[pl.BlockSpec((None, n_in, tc), lambda i: (l, 0, i))],
        out_specs=pl.BlockSpec((n_out, tc), lambda i: (0, i)),
        out_shape=jax.ShapeDtypeStruct((n_out, k), BF16),
        compiler_params=_cparams(("parallel",), 3 * n_in * tc * 4 + 2 * n_out * tc * 2),
        name="pack_w_in",
    )(w_t)
    return packed.reshape(1, n_out, k), offsets


def kernel(x, c, w_ada, b_ada, norm_attn_g, w_in, rel_bias, head_norm_g, w_out, norm_ffn_g,
           w_gate_up, w_down, final_norm_g):
    b, s, d = x.shape
    assert b == 1, "single-sequence prefill only"
    depth = w_ada.shape[0]
    n_heads = d // HEAD_DIM
    ha = 3 * n_heads // 8
    hb = 3 * n_heads // 8
    hc = n_heads - ha - hb
    assert ha % N_KV_A == 0 and s % SB_T == 0
    sizes = (ha * HEAD_DIM, N_KV_A * HEAD_DIM, N_KV_A * HEAD_DIM, IDX_HEADS * IDX_DIM, IDX_DIM,
             IDX_HEADS, hb * HEAD_DIM, hb * HEAD_DIM, hb * HEAD_DIM,
             hc * HEAD_DIM, hc * HEAD_DIM, hc * HEAD_DIM)
    assert w_in.shape[2] == sum(sizes)
    topk = min(IDX_TOPK_MAX, s // 4)
    scale = HEAD_DIM ** -0.5
    pad = N_LEFT_CHUNKS * CHUNK

    rope128 = _rope_tables(s, HEAD_DIM, 1)
    rope64 = _rope_tables(s, IDX_DIM, V7X_LANES // IDX_DIM)

    xs = x.reshape(s, d)
    mod = _adaln(c, w_ada, b_ada)
    w_in_t = jnp.swapaxes(w_in, 1, 2)

    for l in range(depth):
        sh1, sc1, g1, sh2, sc2, g2 = [mod[l, :, n * d:(n + 1) * d] for n in range(6)]

        h = _norm(xs, norm_attn_g[l], sc1, sh1)
        w_packed, offs = _pack_w_in(w_in_t, l, sizes)
        o_qa, o_ka, o_va, o_iq, o_ik, o_iw, o_qb, o_kb, o_vb, o_qc, o_kc, o_vc = offs
        assert o_iw == o_ik + IDX_DIM and o_ik % V7X_LANES == 0
        proj = _matmul_nt(h, w_packed, 0, F32)

        qa = _prep_heads(proj, o_qa, ha, rope128, HEAD_DIM, scale * math.log2(math.e))
        ka = _prep_heads(proj, o_ka, N_KV_A, rope128, HEAD_DIM)
        va = _prep_heads(proj, o_va, N_KV_A)
        vat = va.reshape(N_KV_A, s // DSA_TK, DSA_TK, HEAD_DIM).swapaxes(2, 3)
        iq = _prep_heads(proj, o_iq, IDX_HEADS * IDX_DIM // V7X_LANES, rope64, IDX_DIM,
                         head_major=False)
        ik2, iw = _prep_ikw(proj, o_ik, rope64)
        gain = head_norm_g[l]
        gain_t = jnp.broadcast_to(gain.reshape(n_heads, HEAD_DIM, 1), (n_heads, HEAD_DIM, V7X_LANES))
        o_a = _dsa(iq, iw[:, :IDX_HEADS].T, ik2, qa, ka, vat, gain_t[:ha], topk)

        qb = _prep_heads(proj, o_qb, hb, scale=scale)
        kb = _prep_heads(proj, o_kb, hb)
        vb = _prep_heads(proj, o_vb, hb)
        nb, sub = s // SB_T, V7X_SUBLANES
        kb = kb.reshape(hb, nb, sub, SB_T // sub, HEAD_DIM).swapaxes(2, 3).reshape(hb, nb, SB_T, HEAD_DIM)
        vbt = vb.reshape(hb, nb, sub, SB_T // sub, HEAD_DIM).swapaxes(2, 3).reshape(hb, nb, SB_T, HEAD_DIM)
        vbt = vbt.swapaxes(2, 3)
        o_b = _sb(qb, kb, vbt, gain_t[ha:ha + hb])

        qc = _prep_heads(proj, o_qc, hc, scale=scale)
        kc = _prep_heads(proj, o_kc, hc)
        vc = _prep_heads(proj, o_vc, hc)
        kc = jnp.pad(kc, ((0, 0), (pad, 0), (0, 0)))
        vc = jnp.pad(vc, ((0, 0), (pad, 0), (0, 0)))
        o_c = _cb(qc, kc, vc, _cb_bias(rel_bias[l]), gain[(ha + hb) * HEAD_DIM:], pad)

        xs = _out_proj(o_a, o_b, o_c, w_out, l, xs, g1)

        h = _norm(xs, norm_ffn_g[l], sc2, sh2)
        act = _matmul_swiglu(h, w_gate_up, l)
        xs = _matmul_resid(act, _cast_layer(w_down, l, BF16), 0, xs, g2)

    out = _norm(xs, final_norm_g, out_dtype=x.dtype)
    return out.reshape(b, s, d)
```

```python
import functools
import math

import numpy as np
import jax
import jax.numpy as jnp
from jax import lax
from jax.experimental import pallas as pl
from jax.experimental.pallas import tpu as pltpu

F32 = jnp.float32
BF16 = jnp.bfloat16
I32 = jnp.int32

HEAD_DIM = 128
N_KV_A = 2
IDX_HEADS = 16
IDX_DIM = 64
IDX_TOPK_MAX = 256
CHUNK = 64
N_LEFT_CHUNKS = 8
REL_CLIP = 256
ROPE_THETA = 10000.0
EPS = 1e-6

V7X_LANES = 128
V7X_SUBLANES = 8
V7X_BF16_SUBLANES = 16
V7X_VMEM_BYTES = 64 * 1024 * 1024
V7X_MXU_COLS = 256

PROJ_N_ALIGN = 2 * V7X_MXU_COLS

NEG_BIG = -1e30
INT_MIN = -(2 ** 31)

DSA_TQ = 256
DSA_TK = 512
SB_T = 256
SB_HEADS = 4
CB_TQ = 256
CB_HEADS = 4


def _cparams(semantics, vmem_bytes):
    limit = int(min(V7X_VMEM_BYTES * 7 // 8, max(vmem_bytes, 16 * 1024 * 1024)))
    return pltpu.CompilerParams(dimension_semantics=semantics, vmem_limit_bytes=limit)


def _largest_tile(n, unit, cap):
    best = None
    t = unit
    while t <= min(n, cap):
        if n % t == 0:
            best = t
        t += unit
    assert best is not None, (n, unit, cap)
    return best


def _adaln_kernel(c_ref, w_ref, b_ref, o_ref, acc_ref):
    k = pl.program_id(1)
    c = c_ref[...]
    cond = c * jax.nn.sigmoid(c)
    tr, n = w_ref.shape

    @pl.when(k == 0)
    def _():
        acc_ref[...] = jnp.zeros_like(acc_ref)

    for j in range(n // V7X_LANES):
        cols = slice(j * V7X_LANES, (j + 1) * V7X_LANES)
        prod = w_ref[:, cols] * cond
        part = prod[0:V7X_SUBLANES]
        for g in range(1, tr // V7X_SUBLANES):
            part = part + prod[g * V7X_SUBLANES:(g + 1) * V7X_SUBLANES]
        acc_ref[:, cols] += part

    @pl.when(k == pl.num_programs(1) - 1)
    def _():
        o_ref[...] = jnp.sum(acc_ref[...], axis=0, keepdims=True) + b_ref[...]


def _adaln(c, w_ada, b_ada):
    depth, d, n = w_ada.shape
    tr = _largest_tile(d, V7X_SUBLANES, 128)
    c_b = jnp.broadcast_to(c.reshape(d, 1), (d, V7X_LANES))
    out = pl.pallas_call(
        _adaln_kernel,
        grid=(depth, d // tr),
        in_specs=[
            pl.BlockSpec((tr, V7X_LANES), lambda l, k: (k, 0)),
            pl.BlockSpec((None, tr, n), lambda l, k: (l, k, 0)),
            pl.BlockSpec((None, 1, n), lambda l, k: (l, 0, 0)),
        ],
        out_specs=pl.BlockSpec((None, 1, n), lambda l, k: (l, 0, 0)),
        out_shape=jax.ShapeDtypeStruct((depth, 1, n), F32),
        scratch_shapes=[pltpu.VMEM((V7X_SUBLANES, n), F32)],
        compiler_params=_cparams(("parallel", "arbitrary"), 3 * tr * n * 4),
        name="adaln",
    )(c_b, w_ada, b_ada.reshape(depth, 1, n))
    return out


def _norm_kernel(x_ref, g_ref, *rest, modulate):
    o_ref = rest[-1]
    x = x_ref[...]
    ms = jnp.mean(x * x, axis=-1, keepdims=True)
    y = x * lax.rsqrt(ms + EPS) * g_ref[...]
    if modulate:
        sc_ref, sh_ref = rest[0], rest[1]
        y = y * (1.0 + sc_ref[...]) + sh_ref[...]
    o_ref[...] = y.astype(o_ref.dtype)


def _norm(x, g, scale=None, shift=None, out_dtype=BF16):
    s, d = x.shape
    ts = _largest_tile(s, V7X_SUBLANES, 256)
    modulate = scale is not None
    row = pl.BlockSpec((1, d), lambda i: (0, 0))
    args = [x, g.reshape(1, d)]
    specs = [pl.BlockSpec((ts, d), lambda i: (i, 0)), row]
    if modulate:
        args += [scale.reshape(1, d), shift.reshape(1, d)]
        specs += [row, row]
    return pl.pallas_call(
        functools.partial(_norm_kernel, modulate=modulate),
        grid=(s // ts,),
        in_specs=specs,
        out_specs=pl.BlockSpec((ts, d), lambda i: (i, 0)),
        out_shape=jax.ShapeDtypeStruct((s, d), out_dtype),
        compiler_params=_cparams(("parallel",), 6 * ts * d * 4),
        name="rmsnorm",
    )(*args)


def _dot(a, w):
    return jnp.dot(a, w.astype(BF16), preferred_element_type=F32)


def _mm_plain_kernel(a_ref, wt_ref, o_ref):
    o_ref[...] = lax.dot_general(a_ref[...], wt_ref[...].astype(BF16), (((1,), (1,)), ((), ())),
                                 preferred_element_type=F32).astype(o_ref.dtype)


def _mm_swiglu_kernel(a_ref, wg_ref, wu_ref, o_ref):
    a = a_ref[...]
    gate = _dot(a, wg_ref[...])
    up = _dot(a, wu_ref[...])
    o_ref[...] = (gate * jax.nn.sigmoid(gate) * up).astype(o_ref.dtype)


def _mm_resid_kernel(a_ref, w_ref, r_ref, g_ref, o_ref, *acc, nk):
    if nk == 1:
        o_ref[...] = r_ref[...] + g_ref[...] * _dot(a_ref[...], w_ref[...])
        return
    acc_ref, = acc
    k = pl.program_id(2)

    @pl.when(k == 0)
    def _():
        acc_ref[...] = jnp.zeros_like(acc_ref)

    acc_ref[...] += _dot(a_ref[...], w_ref[...])

    @pl.when(k == nk - 1)
    def _():
        o_ref[...] = r_ref[...] + g_ref[...] * acc_ref[...]


def _mm_tiles(m, k, n, w_itemsize):
    tn = _largest_tile(n, V7X_LANES, 768 if w_itemsize == 2 else 256)
    if k <= 4096:
        return _largest_tile(m, V7X_SUBLANES, 1024), tn, k
    if w_itemsize == 2:
        return _largest_tile(m, V7X_SUBLANES, 512), _largest_tile(n, V7X_LANES, 256), k
    return _largest_tile(m, V7X_SUBLANES, 1024), tn, _largest_tile(k, V7X_LANES, 5632)


def _mm_vmem(tm, tn, tk, w_itemsize, n_w=1):
    return (2 * tm * tk * 2 + n_w * (2 * tk * tn * w_itemsize + tk * tn * 2)
            + (4 + 2 * n_w) * tm * tn * 4)


def _matmul_nt(a, w_t, l, out_dtype):
    m, k = a.shape
    n = w_t.shape[1]
    tm, tn, tk = _mm_tiles(m, k, n, w_t.dtype.itemsize)
    assert tk == k
    return pl.pallas_call(
        _mm_plain_kernel,
        grid=(m // tm, n // tn),
        in_specs=[pl.BlockSpec((tm, k), lambda i, j: (i, 0)),
                  pl.BlockSpec((None, tn, k), lambda i, j: (l, j, 0))],
        out_specs=pl.BlockSpec((tm, tn), lambda i, j: (i, j)),
        out_shape=jax.ShapeDtypeStruct((m, n), out_dtype),
        compiler_params=_cparams(("parallel", "parallel"), _mm_vmem(tm, tn, tk, w_t.dtype.itemsize)),
        name="matmul",
    )(a, w_t)


def _matmul_swiglu(a, w_gate_up, l):
    m, k = a.shape
    n = w_gate_up.shape[2] // 2
    tm, tn, tk = _mm_tiles(m, k, n, w_gate_up.dtype.itemsize)
    assert tk == k
    nb = n // tn
    return pl.pallas_call(
        _mm_swiglu_kernel,
        grid=(m // tm, nb),
        in_specs=[pl.BlockSpec((tm, k), lambda i, j: (i, 0)),
                  pl.BlockSpec((None, k, tn), lambda i, j: (l, 0, j)),
                  pl.BlockSpec((None, k, tn), lambda i, j: (l, 0, j + nb))],
        out_specs=pl.BlockSpec((tm, tn), lambda i, j: (i, j)),
        out_shape=jax.ShapeDtypeStruct((m, n), BF16),
        compiler_params=_cparams(("parallel", "parallel"),
                                 _mm_vmem(tm, tn, tk, w_gate_up.dtype.itemsize, n_w=2)),
        name="matmul_swiglu",
    )(a, w_gate_up, w_gate_up)


def _matmul_resid(a, w, l, resid, gate):
    m, k = a.shape
    n = w.shape[2]
    tm, tn, tk = _mm_tiles(m, k, n, w.dtype.itemsize)
    nk = k // tk
    scratch = [pltpu.VMEM((tm, tn), F32)] if nk > 1 else []
    return pl.pallas_call(
        functools.partial(_mm_resid_kernel, nk=nk),
        grid=(m // tm, n // tn, nk),
        in_specs=[pl.BlockSpec((tm, tk), lambda i, j, kk: (i, kk)),
                  pl.BlockSpec((None, tk, tn), lambda i, j, kk: (l, kk, j)),
                  pl.BlockSpec((tm, tn), lambda i, j, kk: (i, j)),
                  pl.BlockSpec((1, tn), lambda i, j, kk: (0, j))],
        out_specs=pl.BlockSpec((tm, tn), lambda i, j, kk: (i, j)),
        out_shape=jax.ShapeDtypeStruct((m, n), F32),
        scratch_shapes=scratch,
        compiler_params=_cparams(("parallel", "parallel", "arbitrary"),
                                 _mm_vmem(tm, tn, tk, w.dtype.itemsize) + 3 * tm * tn * 4),
        name="matmul_resid",
    )(a, w, resid, gate.reshape(1, n))


def _rope_tables(s, d, lane_groups):
    pos = jnp.arange(s, dtype=jnp.int32).astype(F32)
    inv = ROPE_THETA ** (-jnp.arange(0, d, 2, dtype=F32) / d)
    ang = pos[:, None] * inv[None, :]
    cos, sin = jnp.cos(ang), jnp.sin(ang)
    cos_t = jnp.tile(jnp.concatenate([cos, cos], axis=1), (1, lane_groups))
    sin_t = jnp.tile(jnp.concatenate([-sin, sin], axis=1), (1, lane_groups))
    return cos_t, sin_t


def _rot_half(x, d):
    if d == V7X_LANES:
        return pltpu.roll(x, d // 2, 1)
    lane = lax.broadcasted_iota(I32, x.shape, 1)
    first = (lane & (d - 1)) < d // 2
    return jnp.where(first, pltpu.roll(x, V7X_LANES - d // 2, 1), pltpu.roll(x, d // 2, 1))


def _prep_kernel(x_ref, *rest, rope_dim, scale, head_major):
    o_ref = rest[-1]
    if rope_dim:
        cos, sin = rest[0][...], rest[1][...]
    for h in range(x_ref.shape[1] // V7X_LANES):
        cols = slice(h * V7X_LANES, (h + 1) * V7X_LANES)
        x = x_ref[:, cols]
        if rope_dim:
            x = x * cos + _rot_half(x, rope_dim) * sin
        if scale != 1.0:
            x = x * scale
        if head_major:
            o_ref[h] = x.astype(o_ref.dtype)
        else:
            o_ref[:, cols] = x.astype(o_ref.dtype)


def _prep_heads(proj, col0, n_blocks, tables=None, rope_dim=0, scale=1.0, head_major=True):
    s = proj.shape[0]
    width = n_blocks * V7X_LANES
    assert col0 % width == 0 or n_blocks == 1, "segment must start at a multiple of its width"
    ts = _largest_tile(s, V7X_SUBLANES, 512)
    args = [proj]
    specs = [pl.BlockSpec((ts, width), lambda i: (i, col0 // width))]
    if rope_dim:
        args += list(tables)
        specs += [pl.BlockSpec((ts, V7X_LANES), lambda i: (i, 0))] * 2
    if head_major:
        out_spec = pl.BlockSpec((n_blocks, ts, V7X_LANES), lambda i: (0, i, 0))
        out_shape = jax.ShapeDtypeStruct((n_blocks, s, V7X_LANES), BF16)
    else:
        out_spec = pl.BlockSpec((ts, width), lambda i: (i, 0))
        out_shape = jax.ShapeDtypeStruct((s, width), BF16)
    return pl.pallas_call(
        functools.partial(_prep_kernel, rope_dim=rope_dim, scale=scale, head_major=head_major),
        grid=(s // ts,),
        in_specs=specs,
        out_specs=out_spec,
        out_shape=out_shape,
        compiler_params=_cparams(("parallel",), 8 * ts * width * 4),
        name="prep_heads",
    )(*args)


def _ikw_kernel(x_ref, cos_ref, sin_ref, ik_ref, iw_ref):
    x = x_ref[...]
    y = x * cos_ref[...] + _rot_half(x, IDX_DIM) * sin_ref[...]
    lane = lax.broadcasted_iota(I32, x.shape, 1)
    ik_ref[...] = jnp.where(lane < IDX_DIM, y, pltpu.roll(y, IDX_DIM, 1)).astype(ik_ref.dtype)
    iw_ref[...] = pltpu.roll(x, V7X_LANES - IDX_DIM, 1)


def _prep_ikw(proj, col0, tables):
    s = proj.shape[0]
    ts = _largest_tile(s, V7X_SUBLANES, 1024)
    cb = col0 // V7X_LANES
    blk = pl.BlockSpec((ts, V7X_LANES), lambda i: (i, 0))
    return pl.pallas_call(
        _ikw_kernel,
        grid=(s // ts,),
        in_specs=[pl.BlockSpec((ts, V7X_LANES), lambda i: (i, cb)), blk, blk],
        out_specs=[blk, blk],
        out_shape=[jax.ShapeDtypeStruct((s, V7X_LANES), BF16),
                   jax.ShapeDtypeStruct((s, V7X_LANES), F32)],
        compiler_params=_cparams(("parallel",), 16 * ts * V7X_LANES * 4),
        name="prep_ikw",
    )(proj, *tables)


def _head_rms_t(x_t, g_t):
    ms = jnp.mean(x_t * x_t, axis=0, keepdims=True)
    y = x_t * lax.rsqrt(ms + EPS)
    reps = x_t.shape[1] // g_t.shape[1]
    return y * (g_t if reps == 1 else jnp.concatenate([g_t] * reps, axis=1))


def _dsa_kernel(iq_ref, iwt_ref, ik_ref, q_ref, k_ref, vt_ref, gt_ref, o_ref,
                rhs_scr, key_scr, m_scr, l_scr, acc_scr, s_scr, ma_scr,
                *, topk, rep, idx_scale, n_total):
    tq, tk = DSA_TQ, DSA_TK
    nsub = tk // V7X_SUBLANES
    r_all = rep * tq
    i = pl.program_id(0)
    start = i * tq
    n_kt = (start + tq + tk - 1) >> int(math.log2(tk))

    lane = lax.broadcasted_iota(I32, (tq, V7X_LANES), 1)
    for p in range(IDX_HEADS // 2):
        pair = iq_ref[:, p * V7X_LANES:(p + 1) * V7X_LANES]
        zero = jnp.zeros_like(pair)
        rhs_scr[p, :tq] = jnp.where(lane < IDX_DIM, pair, zero)
        rhs_scr[p, tq:] = jnp.where(lane >= IDX_DIM, pair, zero)

    iwt = iwt_ref[...]
    key_row = lax.broadcasted_iota(I32, (tk, tq), 0)
    q_lane = lax.broadcasted_iota(I32, (tk, tq), 1)
    key_lim = start + ((q_lane >> int(math.log2(CHUNK))) + 1) * CHUNK

    def score_body(kt, carry):
        off = pl.multiple_of(kt * tk, tk)
        ik_t = ik_ref[pl.ds(off, tk), :]
        acc = jnp.zeros((tk, tq), F32)
        for p in range(IDX_HEADS // 2):
            d = lax.dot_general(ik_t, rhs_scr[p], (((1,), (1,)), ((), ())),
                                preferred_element_type=F32)
            for half in range(2):
                h = 2 * p + half
                acc = acc + iwt[h:h + 1, :] * jnp.maximum(d[:, half * tq:(half + 1) * tq], 0.0)
        bits = pltpu.bitcast(acc * idx_scale, I32)
        key = jnp.where(bits < 0, bits ^ 0x7FFFFFFF, bits)
        key_scr[kt] = jnp.where(key_row + off < key_lim, key, INT_MIN)
        return carry

    lax.fori_loop(0, n_kt, score_body, 0)

    n_part = 4

    def count_if(pred):
        def body(kt, parts):
            hit = jnp.where(pred(key_scr[kt], kt), 1, 0)
            parts = list(parts)
            for c in range(nsub):
                parts[c % n_part] = parts[c % n_part] + hit[c * V7X_SUBLANES:(c + 1) * V7X_SUBLANES]
            return tuple(parts)

        parts = lax.fori_loop(0, n_kt, body,
                              tuple(jnp.zeros((V7X_SUBLANES, tq), I32) for _ in range(n_part)))
        return jnp.sum(sum(parts[1:], parts[0]), axis=0, keepdims=True)

    zero = jnp.zeros((1, tq), I32)
    n_pos = count_if(lambda keys, kt: keys >= zero)
    thr = jnp.where(n_pos >= topk, zero, zero + INT_MIN)

    def bisect(b, state):
        thr, n_ge = state
        cand = thr + lax.shift_left(jnp.int32(1), 30 - b)
        n_cand = count_if(lambda keys, kt: keys >= cand)
        ok = n_cand >= topk
        return jnp.where(ok, cand, thr), jnp.where(ok, n_cand, n_ge)

    thr, n_ge = lax.fori_loop(0, 31, bisect, (thr, n_pos))

    tied = jnp.logical_and(n_ge > topk, thr > INT_MIN)

    @pl.when(jnp.max(jnp.where(tied, 1, 0)) > 0)
    def _():
        n_idx_bits = max(1, (n_total - 1).bit_length())
        need = topk - count_if(lambda keys, kt: keys > thr)

        def n_tied_before(limit):
            return count_if(lambda keys, kt: jnp.logical_and(keys == thr, key_row + kt * tk < limit))

        def idx_bisect(b, last):
            cand = last + lax.shift_left(jnp.int32(1), n_idx_bits - 1 - b)
            return jnp.where(n_tied_before(cand) < need, cand, last)

        last = lax.fori_loop(0, n_idx_bits, idx_bisect, zero)

        def demote(kt, carry):
            keys = key_scr[kt]
            drop = jnp.logical_and(jnp.logical_and(tied, keys == thr), key_row + kt * tk > last)
            key_scr[kt] = jnp.where(drop, keys - 1, keys)
            return carry

        lax.fori_loop(0, n_kt, demote, 0)

    thr = jnp.maximum(thr, INT_MIN + 1)

    m_scr[...] = jnp.full(m_scr.shape, NEG_BIG, F32)
    l_scr[...] = jnp.zeros(l_scr.shape, F32)
    acc_scr[...] = jnp.zeros(acc_scr.shape, F32)

    def stage_a(kt):
        slot = kt & 1
        off = pl.multiple_of(kt * tk, tk)
        bias = jnp.where(key_scr[kt] >= thr, 0.0, NEG_BIG)
        bias = jnp.concatenate([bias] * rep, axis=1)
        for g in range(N_KV_A):
            qg = q_ref[g * rep:(g + 1) * rep].reshape(r_all, HEAD_DIM)
            s = lax.dot_general(k_ref[g, pl.ds(off, tk), :], qg, (((1,), (1,)), ((), ())),
                                preferred_element_type=F32) + bias
            m_old = m_scr[g]
            m_new = jnp.maximum(m_old, jnp.max(s, axis=0, keepdims=True))
            s_scr[slot, g] = s
            ma_scr[slot, g, 0:1] = m_new
            ma_scr[slot, g, 1:2] = jnp.exp2(m_old - m_new)
            m_scr[g] = m_new

    def stage_b(kt):
        slot = kt & 1
        for g in range(N_KV_A):
            m_new = ma_scr[slot, g, 0:1]
            alpha = ma_scr[slot, g, 1:2]
            p = jnp.exp2(s_scr[slot, g] - m_new)
            l_scr[g] = alpha * l_scr[g] + jnp.sum(p, axis=0, keepdims=True)
            acc_scr[g] = alpha * acc_scr[g] + jnp.dot(vt_ref[g, kt], p.astype(BF16),
                                                      preferred_element_type=F32)

    stage_a(0)

    def att_body(kt, carry):
        stage_b(kt - 1)
        stage_a(kt)
        return carry

    lax.fori_loop(1, n_kt, att_body, 0)
    stage_b(n_kt - 1)
    for g in range(N_KV_A):
        out_t = acc_scr[g] / l_scr[g]
        for r in range(rep):
            hh = g * rep + r
            head = _head_rms_t(out_t[:, r * tq:(r + 1) * tq], gt_ref[hh])
            o_ref[:, hh * HEAD_DIM:(hh + 1) * HEAD_DIM] = head.T.astype(o_ref.dtype)


def _dsa(iq, iwt, ik2, qa, ka, vat, g_t, topk):
    n_heads, s, _ = qa.shape
    rep = n_heads // N_KV_A
    tq, tk = DSA_TQ, DSA_TK
    assert s % tk == 0
    idx_scale = (IDX_HEADS ** -0.5) * (IDX_DIM ** -0.5)
    vmem = (4 * N_KV_A * s * HEAD_DIM * 2 * 2 + 2 * s * V7X_LANES * 2 + tq * s * 4
            + 32 * rep * tq * tk * 4)
    return pl.pallas_call(
        functools.partial(_dsa_kernel, topk=topk, rep=rep, idx_scale=idx_scale, n_total=s),
        grid=(s // tq,),
        in_specs=[
            pl.BlockSpec((tq, IDX_HEADS * IDX_DIM), lambda i: (i, 0)),
            pl.BlockSpec((IDX_HEADS, tq), lambda i: (0, i)),
            pl.BlockSpec((s, V7X_LANES), lambda i: (0, 0)),
            pl.BlockSpec((n_heads, tq, HEAD_DIM), lambda i: (0, i, 0)),
            pl.BlockSpec((N_KV_A, s, HEAD_DIM), lambda i: (0, 0, 0)),
            pl.BlockSpec((N_KV_A, s // tk, HEAD_DIM, tk), lambda i: (0, 0, 0, 0)),
            pl.BlockSpec((n_heads, HEAD_DIM, V7X_LANES), lambda i: (0, 0, 0)),
        ],
        out_specs=pl.BlockSpec((tq, n_heads * HEAD_DIM), lambda i: (i, 0)),
        out_shape=jax.ShapeDtypeStruct((s, n_heads * HEAD_DIM), BF16),
        scratch_shapes=[
            pltpu.VMEM((IDX_HEADS // 2, 2 * tq, V7X_LANES), BF16),
            pltpu.VMEM((s // tk, tk, tq), I32),
            pltpu.VMEM((N_KV_A, 1, rep * tq), F32),
            pltpu.VMEM((N_KV_A, 1, rep * tq), F32),
            pltpu.VMEM((N_KV_A, HEAD_DIM, rep * tq), F32),
            pltpu.VMEM((2, N_KV_A, tk, rep * tq), F32),
            pltpu.VMEM((2, N_KV_A, 2, rep * tq), F32),
        ],
        compiler_params=_cparams(("parallel",), vmem),
        name="dsa",
    )(iq, iwt, ik2, qa, ka, vat, g_t)


def _sb_kernel(q_ref, k_ref, vt_ref, gt_ref, o_ref, acc_ref):
    t = SB_T
    nrg = t // V7X_SUBLANES
    n_h = q_ref.shape[0]
    i = pl.program_id(1)
    acc_ref[...] = jnp.zeros(acc_ref.shape, F32)

    def local_scan(hh, j, diagonal):
        z = lax.dot_general(k_ref[hh, j], q_ref[hh], (((1,), (1,)), ((), ())),
                            preferred_element_type=F32)
        e = jnp.exp(-jnp.abs(z))
        r = 1.0 / (1.0 + e)
        er = e * r
        pos = z >= 0.0
        beta = jnp.where(pos, r, er)
        keep = jnp.where(pos, er, r)
        if diagonal:
            rr = lax.broadcasted_iota(I32, (t, t), 0)
            tt = lax.broadcasted_iota(I32, (t, t), 1)
            key_pos = (rr & (V7X_SUBLANES - 1)) * nrg + (rr >> int(math.log2(V7X_SUBLANES)))
            causal = key_pos < tt
            beta = jnp.where(causal, beta, 0.0)
            keep = jnp.where(causal, keep, 1.0)
        run8 = jnp.ones((V7X_SUBLANES, t), F32)
        a_loc = [None] * nrg
        for g in range(nrg - 1, -1, -1):
            rows = slice(g * V7X_SUBLANES, (g + 1) * V7X_SUBLANES)
            a_loc[g] = beta[rows] * run8
            run8 = run8 * keep[rows]
        return a_loc, run8

    def finish(hh, j, scan, carry):
        a_loc, run8 = scan
        run = carry
        offs = [None] * V7X_SUBLANES
        for u in range(V7X_SUBLANES - 1, -1, -1):
            offs[u] = run
            run = run * run8[u:u + 1]
        off8 = jnp.concatenate(offs, axis=0)
        a = jnp.concatenate([al * off8 for al in a_loc], axis=0)
        acc_ref[hh] += jnp.dot(vt_ref[hh, j], a.astype(BF16), preferred_element_type=F32)
        return run

    j_prev = jnp.maximum(i - 1, 0)
    has_prev = (i > 0).astype(F32)
    carries = []
    for hh in range(n_h):
        scan_diag = local_scan(hh, i, True)
        scan_prev = local_scan(hh, j_prev, False)
        carry = finish(hh, i, scan_diag, jnp.ones((1, t), F32)) * has_prev
        carries.append(finish(hh, j_prev, scan_prev, carry))

    def cond(state):
        j, cs = state
        alive = jnp.max(functools.reduce(jnp.maximum, cs)) > 0.0
        return jnp.logical_and(j >= 0, alive)

    def body(state):
        j, cs = state
        return j - 1, tuple(finish(hh, j, local_scan(hh, j, False), cs[hh]) for hh in range(n_h))

    lax.while_loop(cond, body, (i - 2, tuple(carries)))
    for hh in range(n_h):
        head = _head_rms_t(acc_ref[hh], gt_ref[hh])
        o_ref[:, hh * HEAD_DIM:(hh + 1) * HEAD_DIM] = head.T.astype(o_ref.dtype)


def _sb(q, k_perm, v_t, g_t):
    n_heads, s, _ = q.shape
    t = SB_T
    nb = s // t
    n_h = SB_HEADS if n_heads % SB_HEADS == 0 else 1
    return pl.pallas_call(
        _sb_kernel,
        grid=(n_heads // n_h, nb),
        in_specs=[
            pl.BlockSpec((n_h, t, HEAD_DIM), lambda h, i: (h, i, 0)),
            pl.BlockSpec((n_h, nb, t, HEAD_DIM), lambda h, i: (h, 0, 0, 0)),
            pl.BlockSpec((n_h, nb, HEAD_DIM, t), lambda h, i: (h, 0, 0, 0)),
            pl.BlockSpec((n_h, HEAD_DIM, V7X_LANES), lambda h, i: (h, 0, 0)),
        ],
        out_specs=pl.BlockSpec((t, n_h * HEAD_DIM), lambda h, i: (i, h)),
        out_shape=jax.ShapeDtypeStruct((s, n_heads * HEAD_DIM), BF16),
        scratch_shapes=[pltpu.VMEM((n_h, HEAD_DIM, t), F32)],
        compiler_params=_cparams(("parallel", "parallel"),
                                 n_h * (8 * s * HEAD_DIM * 2 + 40 * t * t * 4)),
        name="stick_breaking",
    )(q, k_perm, v_t, g_t)


def _cb_kernel(q_ref, k_ref, v_ref, b_ref, g_ref, o_ref, *, win, pad):
    tq = CB_TQ
    i = pl.program_id(1)
    off = pl.multiple_of(i * tq, tq)
    col = lax.broadcasted_iota(I32, (tq, win), 1)
    in_seq = col + off >= pad
    for hh in range(q_ref.shape[0]):
        k_w = k_ref[hh, pl.ds(off, win), :]
        v_w = v_ref[hh, pl.ds(off, win), :]
        s = lax.dot_general(q_ref[hh], k_w, (((1,), (1,)), ((), ())), preferred_element_type=F32)
        s = jnp.where(in_seq, s + b_ref[hh], NEG_BIG)
        m = jnp.max(s, axis=-1, keepdims=True)
        p = jnp.exp(s - m)
        l = jnp.sum(p, axis=-1, keepdims=True)
        o = jnp.dot(p.astype(BF16), v_w, preferred_element_type=F32) / l
        cols = slice(hh * HEAD_DIM, (hh + 1) * HEAD_DIM)
        ms = jnp.mean(o * o, axis=-1, keepdims=True)
        o_ref[:, cols] = (o * lax.rsqrt(ms + EPS) * g_ref[:, cols]).astype(o_ref.dtype)


def _cb(q, k_pad, v_pad, bias, gain, pad):
    n_heads, s, _ = q.shape
    tq = CB_TQ
    win = pad + tq
    sp = k_pad.shape[1]
    n_h = CB_HEADS if n_heads % CB_HEADS == 0 else 1
    return pl.pallas_call(
        functools.partial(_cb_kernel, win=win, pad=pad),
        grid=(n_heads // n_h, s // tq),
        in_specs=[
            pl.BlockSpec((n_h, tq, HEAD_DIM), lambda h, i: (h, i, 0)),
            pl.BlockSpec((n_h, sp, HEAD_DIM), lambda h, i: (h, 0, 0)),
            pl.BlockSpec((n_h, sp, HEAD_DIM), lambda h, i: (h, 0, 0)),
            pl.BlockSpec((n_h, tq, win), lambda h, i: (h, 0, 0)),
            pl.BlockSpec((1, n_h * HEAD_DIM), lambda h, i: (0, h)),
        ],
        out_specs=pl.BlockSpec((tq, n_h * HEAD_DIM), lambda h, i: (i, h)),
        out_shape=jax.ShapeDtypeStruct((s, n_heads * HEAD_DIM), BF16),
        compiler_params=_cparams(("parallel", "parallel"),
                                 n_h * (8 * sp * HEAD_DIM * 2 + 16 * tq * win * 4)),
        name="chunk_band",
    )(q, k_pad, v_pad, bias, gain.reshape(1, n_heads * HEAD_DIM))


def _cb_bias(rel_bias):
    tq = CB_TQ
    pad = N_LEFT_CHUNKS * CHUNK
    win = pad + tq
    n_heads = rel_bias.shape[0]
    qi = np.arange(tq)[:, None]
    kj = np.arange(win)[None, :]
    q_chunk = qi // CHUNK
    k_chunk = kj // CHUNK - N_LEFT_CHUNKS
    in_band = (k_chunk <= q_chunk) & (k_chunk >= q_chunk - N_LEFT_CHUNKS)
    p = win + tq
    n_clipped = pad + tq - 1 - REL_CLIP
    lo = 2 * REL_CLIP - (p - 1 - n_clipped)
    assert lo >= 0 and pad - (win - 1) >= -REL_CLIP
    u = jnp.concatenate([jnp.broadcast_to(rel_bias[:, 2 * REL_CLIP:], (n_heads, n_clipped)),
                         rel_bias[:, lo:2 * REL_CLIP + 1][:, ::-1]], axis=1).astype(F32)
    u_rot = jnp.concatenate([u[:, tq - 1:], u[:, :tq - 1]], axis=1)
    bias = jnp.tile(u_rot, (1, tq))[:, :tq * (p - 1)].reshape(n_heads, tq, p - 1)[:, :, :win]
    return jnp.where(jnp.asarray(in_band)[None], bias, NEG_BIG)


def _out_proj_kernel(a_ref, b_ref, c_ref, w_ref, r_ref, g_ref, o_ref):
    acc, row = None, 0
    for x_ref in (a_ref, b_ref, c_ref):
        part = _dot(x_ref[...], w_ref[row:row + x_ref.shape[1], :])
        acc = part if acc is None else acc + part
        row += x_ref.shape[1]
    o_ref[...] = r_ref[...] + g_ref[...] * acc


def _out_proj(o_a, o_b, o_c, w, l, resid, gate):
    m = o_a.shape[0]
    k, n = w.shape[1], w.shape[2]
    assert o_a.shape[1] + o_b.shape[1] + o_c.shape[1] == k
    tm, tn, tk = _mm_tiles(m, k, n, w.dtype.itemsize)
    assert tk == k
    lhs = lambda x: pl.BlockSpec((tm, x.shape[1]), lambda i, j: (i, 0))
    return pl.pallas_call(
        _out_proj_kernel,
        grid=(m // tm, n // tn),
        in_specs=[lhs(o_a), lhs(o_b), lhs(o_c),
                  pl.BlockSpec((None, k, tn), lambda i, j: (l, 0, j)),
                  pl.BlockSpec((tm, tn), lambda i, j: (i, j)),
                  pl.BlockSpec((1, tn), lambda i, j: (0, j))],
        out_specs=pl.BlockSpec((tm, tn), lambda i, j: (i, j)),
        out_shape=jax.ShapeDtypeStruct((m, n), F32),
        compiler_params=_cparams(("parallel", "parallel"),
                                 _mm_vmem(tm, tn, tk, w.dtype.itemsize) + 3 * tm * tn * 4),
        name="out_proj",
    )(o_a, o_b, o_c, w, resid, gate.reshape(1, n))


def _pack_kernel(x_ref, o_ref, *, runs):
    end = 0
    for src, dst, height in runs:
        if dst > end:
            o_ref[end:dst, :] = jnp.zeros((dst - end, o_ref.shape[1]), o_ref.dtype)
        o_ref[dst:dst + height, :] = x_ref[src:src + height, :].astype(o_ref.dtype)
        end = dst + height
    if end < o_ref.shape[0]:
        o_ref[end:, :] = jnp.zeros((o_ref.shape[0] - end, o_ref.shape[1]), o_ref.dtype)


def _cast_kernel(x_ref, o_ref):
    o_ref[...] = x_ref[...].astype(o_ref.dtype)


def _cast_layer(w, l, dtype):
    _, k, n = w.shape
    tr = _largest_tile(k, V7X_BF16_SUBLANES, 256)
    out = pl.pallas_call(
        _cast_kernel,
        grid=(k // tr,),
        in_specs=[pl.BlockSpec((None, tr, n), lambda i: (l, i, 0))],
        out_specs=pl.BlockSpec((tr, n), lambda i: (i, 0)),
        out_shape=jax.ShapeDtypeStruct((k, n), dtype),
        compiler_params=_cparams(("parallel",), 4 * tr * n * 4),
        name="cast_layer",
    )(w)
    return out.reshape(1, k, n)


def _packed_layout(sizes):
    units, src = [], 0
    for idx, width in enumerate(sizes):
        if width >= V7X_LANES or not units or units[-1][2] >= V7X_LANES:
            units.append([[idx], src, width])
        else:
            units[-1][0].append(idx)
            units[-1][2] += width
        src += width
    srcs = np.concatenate([[0], np.cumsum(sizes)[:-1]])
    offsets, runs, dst = [0] * len(sizes), [], 0
    for members, usrc, uwidth in sorted(units, key=lambda u: -u[2]):
        padded = -(-uwidth // V7X_LANES) * V7X_LANES
        dst = -(-dst // padded) * padded
        for idx in members:
            offsets[idx] = dst + int(srcs[idx]) - usrc
        runs.append((usrc, dst, uwidth))
        dst += padded
    return offsets, tuple(runs), -(-dst // PROJ_N_ALIGN) * PROJ_N_ALIGN


def _pack_w_in(w_t, l, sizes):
    offsets, runs, n_out = _packed_layout(sizes)
    assert all(v % V7X_BF16_SUBLANES == 0 for run in runs for v in run), "segments must be bf16-tile aligned"
    _, n_in, k = w_t.shape
    tc = _largest_tile(k, V7X_LANES, 256)
    packed = pl.pallas_call(
        functools.partial(_pack_kernel, runs=runs),
        grid=(k // tc,),
        in_specs=[pl.BlockSpec((None, n_in, tc), lambda i: (l, 0, i))],
        out_specs=pl.BlockSpec((n_out, tc), lambda i: (0, i)),
        out_shape=jax.ShapeDtypeStruct((n_out, k), BF16),
        compiler_params=_cparams(("parallel",), 3 * n_in * tc * 4 + 2 * n_out * tc * 2),
        name="pack_w_in",
    )(w_t)
    return packed.reshape(1, n_out, k), offsets


def kernel(x, c, w_ada, b_ada, norm_attn_g, w_in, rel_bias, head_norm_g, w_out, norm_ffn_g,
           w_gate_up, w_down, final_norm_g):
    b, s, d = x.shape
    assert b == 1, "single-sequence prefill only"
    depth = w_ada.shape[0]
    n_heads = d // HEAD_DIM
    ha = 3 * n_heads // 8
    hb = 3 * n_heads // 8
    hc = n_heads - ha - hb
    assert ha % N_KV_A == 0 and s % SB_T == 0
    sizes = (ha * HEAD_DIM, N_KV_A * HEAD_DIM, N_KV_A * HEAD_DIM, IDX_HEADS * IDX_DIM, IDX_DIM,
             IDX_HEADS, hb * HEAD_DIM, hb * HEAD_DIM, hb * HEAD_DIM,
             hc * HEAD_DIM, hc * HEAD_DIM, hc * HEAD_DIM)
    assert w_in.shape[2] == sum(sizes)
    topk = min(IDX_TOPK_MAX, s // 4)
    scale = HEAD_DIM ** -0.5
    pad = N_LEFT_CHUNKS * CHUNK

    rope128 = _rope_tables(s, HEAD_DIM, 1)
    rope64 = _rope_tables(s, IDX_DIM, V7X_LANES // IDX_DIM)

    xs = x.reshape(s, d)
    mod = _adaln(c, w_ada, b_ada)
    w_in_t = jnp.swapaxes(w_in, 1, 2)

    for l in range(depth):
        sh1, sc1, g1, sh2, sc2, g2 = [mod[l, :, n * d:(n + 1) * d] for n in range(6)]

        h = _norm(xs, norm_attn_g[l], sc1, sh1)
        w_packed, offs = _pack_w_in(w_in_t, l, sizes)
        o_qa, o_ka, o_va, o_iq, o_ik, o_iw, o_qb, o_kb, o_vb, o_qc, o_kc, o_vc = offs
        assert o_iw == o_ik + IDX_DIM and o_ik % V7X_LANES == 0
        proj = _matmul_nt(h, w_packed, 0, F32)

        qa = _prep_heads(proj, o_qa, ha, rope128, HEAD_DIM, scale * math.log2(math.e))
        ka = _prep_heads(proj, o_ka, N_KV_A, rope128, HEAD_DIM)
        va = _prep_heads(proj, o_va, N_KV_A)
        vat = va.reshape(N_KV_A, s // DSA_TK, DSA_TK, HEAD_DIM).swapaxes(2, 3)
        iq = _prep_heads(proj, o_iq, IDX_HEADS * IDX_DIM // V7X_LANES, rope64, IDX_DIM,
                         head_major=False)
        ik2, iw = _prep_ikw(proj, o_ik, rope64)
        gain = head_norm_g[l]
        gain_t = jnp.broadcast_to(gain.reshape(n_heads, HEAD_DIM, 1), (n_heads, HEAD_DIM, V7X_LANES))
        o_a = _dsa(iq, iw[:, :IDX_HEADS].T, ik2, qa, ka, vat, gain_t[:ha], topk)

        qb = _prep_heads(proj, o_qb, hb, scale=scale)
        kb = _prep_heads(proj, o_kb, hb)
        vb = _prep_heads(proj, o_vb, hb)
        nb, sub = s // SB_T, V7X_SUBLANES
        kb = kb.reshape(hb, nb, sub, SB_T // sub, HEAD_DIM).swapaxes(2, 3).reshape(hb, nb, SB_T, HEAD_DIM)
        vbt = vb.reshape(hb, nb, sub, SB_T // sub, HEAD_DIM).swapaxes(2, 3).reshape(hb, nb, SB_T, HEAD_DIM)
        vbt = vbt.swapaxes(2, 3)
        o_b = _sb(qb, kb, vbt, gain_t[ha:ha + hb])

        qc = _prep_heads(proj, o_qc, hc, scale=scale)
        kc = _prep_heads(proj, o_kc, hc)
        vc = _prep_heads(proj, o_vc, hc)
        kc = jnp.pad(kc, ((0, 0), (pad, 0), (0, 0)))
        vc = jnp.pad(vc, ((0, 0), (pad, 0), (0, 0)))
        o_c = _cb(qc, kc, vc, _cb_bias(rel_bias[l]), gain[(ha + hb) * HEAD_DIM:], pad)

        xs = _out_proj(o_a, o_b, o_c, w_out, l, xs, g1)

        h = _norm(xs, norm_ffn_g[l], sc2, sh2)
        act = _matmul_swiglu(h, w_gate_up, l)
        xs = _matmul_resid(act, _cast_layer(w_down, l, BF16), 0, xs, g2)

    out = _norm(xs, final_norm_g, out_dtype=x.dtype)
    return out.reshape(b, s, d)
```

```python
import functools
import math

import numpy as np
import jax
import jax.numpy as jnp
from jax import lax
from jax.experimental import pallas as pl
from jax.experimental.pallas import tpu as pltpu

F32 = jnp.float32
BF16 = jnp.bfloat16
I32 = jnp.int32

HEAD_DIM = 128
N_KV_A = 2
IDX_HEADS = 16
IDX_DIM = 64
IDX_TOPK_MAX = 256
CHUNK = 64
N_LEFT_CHUNKS = 8
REL_CLIP = 256
ROPE_THETA = 10000.0
EPS = 1e-6

V7X_LANES = 128
V7X_SUBLANES = 8
V7X_BF16_SUBLANES = 16
V7X_VMEM_BYTES = 64 * 1024 * 1024
V7X_MXU_COLS = 256

PROJ_N_ALIGN = 2 * V7X_MXU_COLS

NEG_BIG = -1e30
INT_MIN = -(2 ** 31)

DSA_TQ = 256
DSA_TK = 512
SB_T = 256
SB_HEADS = 4
CB_TQ = 256
CB_HEADS = 4


def _cparams(semantics, vmem_bytes):
    limit = int(min(V7X_VMEM_BYTES * 7 // 8, max(vmem_bytes, 16 * 1024 * 1024)))
    return pltpu.CompilerParams(dimension_semantics=semantics, vmem_limit_bytes=limit)


def _largest_tile(n, unit, cap):
    best = None
    t = unit
    while t <= min(n, cap):
        if n % t == 0:
            best = t
        t += unit
    assert best is not None, (n, unit, cap)
    return best


def _adaln_kernel(c_ref, w_ref, b_ref, o_ref, acc_ref):
    k = pl.program_id(1)
    c = c_ref[...]
    cond = c * jax.nn.sigmoid(c)
    tr, n = w_ref.shape

    @pl.when(k == 0)
    def _():
        acc_ref[...] = jnp.zeros_like(acc_ref)

    for j in range(n // V7X_LANES):
        cols = slice(j * V7X_LANES, (j + 1) * V7X_LANES)
        prod = w_ref[:, cols] * cond
        part = prod[0:V7X_SUBLANES]
        for g in range(1, tr // V7X_SUBLANES):
            part = part + prod[g * V7X_SUBLANES:(g + 1) * V7X_SUBLANES]
        acc_ref[:, cols] += part

    @pl.when(k == pl.num_programs(1) - 1)
    def _():
        o_ref[...] = jnp.sum(acc_ref[...], axis=0, keepdims=True) + b_ref[...]


def _adaln(c, w_ada, b_ada):
    depth, d, n = w_ada.shape
    tr = _largest_tile(d, V7X_SUBLANES, 128)
    c_b = jnp.broadcast_to(c.reshape(d, 1), (d, V7X_LANES))
    out = pl.pallas_call(
        _adaln_kernel,
        grid=(depth, d // tr),
        in_specs=[
            pl.BlockSpec((tr, V7X_LANES), lambda l, k: (k, 0)),
            pl.BlockSpec((None, tr, n), lambda l, k: (l, k, 0)),
            pl.BlockSpec((None, 1, n), lambda l, k: (l, 0, 0)),
        ],
        out_specs=pl.BlockSpec((None, 1, n), lambda l, k: (l, 0, 0)),
        out_shape=jax.ShapeDtypeStruct((depth, 1, n), F32),
        scratch_shapes=[pltpu.VMEM((V7X_SUBLANES, n), F32)],
        compiler_params=_cparams(("parallel", "arbitrary"), 3 * tr * n * 4),
        name="adaln",
    )(c_b, w_ada, b_ada.reshape(depth, 1, n))
    return out


def _norm_kernel(x_ref, g_ref, *rest, modulate):
    o_ref = rest[-1]
    x = x_ref[...]
    ms = jnp.mean(x * x, axis=-1, keepdims=True)
    y = x * lax.rsqrt(ms + EPS) * g_ref[...]
    if modulate:
        sc_ref, sh_ref = rest[0], rest[1]
        y = y * (1.0 + sc_ref[...]) + sh_ref[...]
    o_ref[...] = y.astype(o_ref.dtype)


def _norm(x, g, scale=None, shift=None, out_dtype=BF16):
    s, d = x.shape
    ts = _largest_tile(s, V7X_SUBLANES, 512)
    modulate = scale is not None
    row = pl.BlockSpec((1, d), lambda i: (0, 0))
    args = [x, g.reshape(1, d)]
    specs = [pl.BlockSpec((ts, d), lambda i: (i, 0)), row]
    if modulate:
        args += [scale.reshape(1, d), shift.reshape(1, d)]
        specs += [row, row]
    return pl.pallas_call(
        functools.partial(_norm_kernel, modulate=modulate),
        grid=(s // ts,),
        in_specs=specs,
        out_specs=pl.BlockSpec((ts, d), lambda i: (i, 0)),
        out_shape=jax.ShapeDtypeStruct((s, d), out_dtype),
        compiler_params=_cparams(("parallel",), 6 * ts * d * 4),
        name="rmsnorm",
    )(*args)


def _dot(a, w):
    return jnp.dot(a, w.astype(BF16), preferred_element_type=F32)


def _mm_plain_kernel(a_ref, wt_ref, o_ref):
    o_ref[...] = lax.dot_general(a_ref[...], wt_ref[...].astype(BF16), (((1,), (1,)), ((), ())),
                                 preferred_element_type=F32).astype(o_ref.dtype)


def _mm_swiglu_kernel(a_ref, wg_ref, wu_ref, o_ref):
    a = a_ref[...]
    gate = _dot(a, wg_ref[...])
    up = _dot(a, wu_ref[...])
    o_ref[...] = (gate * jax.nn.sigmoid(gate) * up).astype(o_ref.dtype)


def _mm_resid_kernel(a_ref, w_ref, r_ref, g_ref, o_ref, *acc, nk):
    if nk == 1:
        o_ref[...] = r_ref[...] + g_ref[...] * _dot(a_ref[...], w_ref[...])
        return
    acc_ref, = acc
    k = pl.program_id(2)

    @pl.when(k == 0)
    def _():
        acc_ref[...] = jnp.zeros_like(acc_ref)

    acc_ref[...] += _dot(a_ref[...], w_ref[...])

    @pl.when(k == nk - 1)
    def _():
        o_ref[...] = r_ref[...] + g_ref[...] * acc_ref[...]


def _mm_tiles(m, k, n, w_itemsize):
    tn = _largest_tile(n, V7X_LANES, 768 if w_itemsize == 2 else 256)
    if k <= 4096:
        return _largest_tile(m, V7X_SUBLANES, 1024), tn, k
    if w_itemsize == 2:
        return _largest_tile(m, V7X_SUBLANES, 512), _largest_tile(n, V7X_LANES, 256), k
    return _largest_tile(m, V7X_SUBLANES, 1024), tn, _largest_tile(k, V7X_LANES, 5632)


def _mm_vmem(tm, tn, tk, w_itemsize, n_w=1):
    return (2 * tm * tk * 2 + n_w * (2 * tk * tn * w_itemsize + tk * tn * 2)
            + (4 + 2 * n_w) * tm * tn * 4)


def _matmul_nt(a, w_t, l, out_dtype):
    m, k = a.shape
    n = w_t.shape[1]
    tm, tn, tk = _mm_tiles(m, k, n, w_t.dtype.itemsize)
    assert tk == k
    return pl.pallas_call(
        _mm_plain_kernel,
        grid=(m // tm, n // tn),
        in_specs=[pl.BlockSpec((tm, k), lambda i, j: (i, 0)),
                  pl.BlockSpec((None, tn, k), lambda i, j: (l, j, 0))],
        out_specs=pl.BlockSpec((tm, tn), lambda i, j: (i, j)),
        out_shape=jax.ShapeDtypeStruct((m, n), out_dtype),
        compiler_params=_cparams(("parallel", "parallel"), _mm_vmem(tm, tn, tk, w_t.dtype.itemsize)),
        name="matmul",
    )(a, w_t)


def _matmul_swiglu(a, w_gate_up, l):
    m, k = a.shape
    n = w_gate_up.shape[2] // 2
    tm, tn, tk = _mm_tiles(m, k, n, w_gate_up.dtype.itemsize)
    assert tk == k
    nb = n // tn
    return pl.pallas_call(
        _mm_swiglu_kernel,
        grid=(m // tm, nb),
        in_specs=[pl.BlockSpec((tm, k), lambda i, j: (i, 0)),
                  pl.BlockSpec((None, k, tn), lambda i, j: (l, 0, j)),
                  pl.BlockSpec((None, k, tn), lambda i, j: (l, 0, j + nb))],
        out_specs=pl.BlockSpec((tm, tn), lambda i, j: (i, j)),
        out_shape=jax.ShapeDtypeStruct((m, n), BF16),
        compiler_params=_cparams(("parallel", "parallel"),
                                 _mm_vmem(tm, tn, tk, w_gate_up.dtype.itemsize, n_w=2)),
        name="matmul_swiglu",
    )(a, w_gate_up, w_gate_up)


def _matmul_resid(a, w, l, resid, gate):
    m, k = a.shape
    n = w.shape[2]
    tm, tn, tk = _mm_tiles(m, k, n, w.dtype.itemsize)
    nk = k // tk
    scratch = [pltpu.VMEM((tm, tn), F32)] if nk > 1 else []
    return pl.pallas_call(
        functools.partial(_mm_resid_kernel, nk=nk),
        grid=(m // tm, n // tn, nk),
        in_specs=[pl.BlockSpec((tm, tk), lambda i, j, kk: (i, kk)),
                  pl.BlockSpec((None, tk, tn), lambda i, j, kk: (l, kk, j)),
                  pl.BlockSpec((tm, tn), lambda i, j, kk: (i, j)),
                  pl.BlockSpec((1, tn), lambda i, j, kk: (0, j))],
        out_specs=pl.BlockSpec((tm, tn), lambda i, j, kk: (i, j)),
        out_shape=jax.ShapeDtypeStruct((m, n), F32),
        scratch_shapes=scratch,
        compiler_params=_cparams(("parallel", "parallel", "arbitrary"),
                                 _mm_vmem(tm, tn, tk, w.dtype.itemsize) + 3 * tm * tn * 4),
        name="matmul_resid",
    )(a, w, resid, gate.reshape(1, n))


def _rope_tables(s, d, lane_groups):
    pos = jnp.arange(s, dtype=jnp.int32).astype(F32)
    inv = ROPE_THETA ** (-jnp.arange(0, d, 2, dtype=F32) / d)
    ang = pos[:, None] * inv[None, :]
    cos, sin = jnp.cos(ang), jnp.sin(ang)
    cos_t = jnp.tile(jnp.concatenate([cos, cos], axis=1), (1, lane_groups))
    sin_t = jnp.tile(jnp.concatenate([-sin, sin], axis=1), (1, lane_groups))
    return cos_t, sin_t


def _rot_half(x, d):
    if d == V7X_LANES:
        return pltpu.roll(x, d // 2, 1)
    lane = lax.broadcasted_iota(I32, x.shape, 1)
    first = (lane & (d - 1)) < d // 2
    return jnp.where(first, pltpu.roll(x, V7X_LANES - d // 2, 1), pltpu.roll(x, d // 2, 1))


def _prep_kernel(x_ref, *rest, rope_dim, scale, head_major):
    o_ref = rest[-1]
    if rope_dim:
        cos, sin = rest[0][...], rest[1][...]
    for h in range(x_ref.shape[1] // V7X_LANES):
        cols = slice(h * V7X_LANES, (h + 1) * V7X_LANES)
        x = x_ref[:, cols]
        if rope_dim:
            x = x * cos + _rot_half(x, rope_dim) * sin
        if scale != 1.0:
            x = x * scale
        if head_major:
            o_ref[h] = x.astype(o_ref.dtype)
        else:
            o_ref[:, cols] = x.astype(o_ref.dtype)


def _prep_heads(proj, col0, n_blocks, tables=None, rope_dim=0, scale=1.0, head_major=True):
    s = proj.shape[0]
    width = n_blocks * V7X_LANES
    assert col0 % width == 0 or n_blocks == 1, "segment must start at a multiple of its width"
    ts = _largest_tile(s, V7X_SUBLANES, 512)
    args = [proj]
    specs = [pl.BlockSpec((ts, width), lambda i: (i, col0 // width))]
    if rope_dim:
        args += list(tables)
        specs += [pl.BlockSpec((ts, V7X_LANES), lambda i: (i, 0))] * 2
    if head_major:
        out_spec = pl.BlockSpec((n_blocks, ts, V7X_LANES), lambda i: (0, i, 0))
        out_shape = jax.ShapeDtypeStruct((n_blocks, s, V7X_LANES), BF16)
    else:
        out_spec = pl.BlockSpec((ts, width), lambda i: (i, 0))
        out_shape = jax.ShapeDtypeStruct((s, width), BF16)
    return pl.pallas_call(
        functools.partial(_prep_kernel, rope_dim=rope_dim, scale=scale, head_major=head_major),
        grid=(s // ts,),
        in_specs=specs,
        out_specs=out_spec,
        out_shape=out_shape,
        compiler_params=_cparams(("parallel",), 8 * ts * width * 4),
        name="prep_heads",
    )(*args)


def _ikw_kernel(x_ref, cos_ref, sin_ref, ik_ref, iw_ref):
    x = x_ref[...]
    y = x * cos_ref[...] + _rot_half(x, IDX_DIM) * sin_ref[...]
    lane = lax.broadcasted_iota(I32, x.shape, 1)
    ik_ref[...] = jnp.where(lane < IDX_DIM, y, pltpu.roll(y, IDX_DIM, 1)).astype(ik_ref.dtype)
    iw_ref[...] = pltpu.roll(x, V7X_LANES - IDX_DIM, 1)


def _prep_ikw(proj, col0, tables):
    s = proj.shape[0]
    ts = _largest_tile(s, V7X_SUBLANES, 1024)
    cb = col0 // V7X_LANES
    blk = pl.BlockSpec((ts, V7X_LANES), lambda i: (i, 0))
    return pl.pallas_call(
        _ikw_kernel,
        grid=(s // ts,),
        in_specs=[pl.BlockSpec((ts, V7X_LANES), lambda i: (i, cb)), blk, blk],
        out_specs=[blk, blk],
        out_shape=[jax.ShapeDtypeStruct((s, V7X_LANES), BF16),
                   jax.ShapeDtypeStruct((s, V7X_LANES), F32)],
        compiler_params=_cparams(("parallel",), 16 * ts * V7X_LANES * 4),
        name="prep_ikw",
    )(proj, *tables)


def _head_rms_t(x_t, g_t):
    ms = jnp.mean(x_t * x_t, axis=0, keepdims=True)
    y = x_t * lax.rsqrt(ms + EPS)
    reps = x_t.shape[1] // g_t.shape[1]
    return y * (g_t if reps == 1 else jnp.concatenate([g_t] * reps, axis=1))


def _dsa_kernel(iq_ref, iwt_ref, ik_ref, q_ref, k_ref, vt_ref, gt_ref, o_ref,
                rhs_scr, key_scr, m_scr, l_scr, acc_scr, s_scr, ma_scr,
                *, topk, rep, idx_scale, n_total):
    tq, tk = DSA_TQ, DSA_TK
    nsub = tk // V7X_SUBLANES
    r_all = rep * tq
    i = pl.program_id(0)
    start = i * tq
    n_kt = (start + tq + tk - 1) >> int(math.log2(tk))

    lane = lax.broadcasted_iota(I32, (tq, V7X_LANES), 1)
    for p in range(IDX_HEADS // 2):
        pair = iq_ref[:, p * V7X_LANES:(p + 1) * V7X_LANES]
        zero = jnp.zeros_like(pair)
        rhs_scr[p, :tq] = jnp.where(lane < IDX_DIM, pair, zero)
        rhs_scr[p, tq:] = jnp.where(lane >= IDX_DIM, pair, zero)

    iwt = iwt_ref[...]
    key_row = lax.broadcasted_iota(I32, (tk, tq), 0)
    q_lane = lax.broadcasted_iota(I32, (tk, tq), 1)
    key_lim = start + ((q_lane >> int(math.log2(CHUNK))) + 1) * CHUNK

    def score_body(kt, carry):
        off = pl.multiple_of(kt * tk, tk)
        ik_t = ik_ref[pl.ds(off, tk), :]
        acc = jnp.zeros((tk, tq), F32)
        for p in range(IDX_HEADS // 2):
            d = lax.dot_general(ik_t, rhs_scr[p], (((1,), (1,)), ((), ())),
                                preferred_element_type=F32)
            for half in range(2):
                h = 2 * p + half
                acc = acc + iwt[h:h + 1, :] * jnp.maximum(d[:, half * tq:(half + 1) * tq], 0.0)
        bits = pltpu.bitcast(acc * idx_scale, I32)
        key = jnp.where(bits < 0, bits ^ 0x7FFFFFFF, bits)
        key_scr[kt] = jnp.where(key_row + off < key_lim, key, INT_MIN)
        return carry

    lax.fori_loop(0, n_kt, score_body, 0)

    n_part = 4

    def count_if(pred):
        def body(kt, parts):
            hit = jnp.where(pred(key_scr[kt], kt), 1, 0)
            parts = list(parts)
            for c in range(nsub):
                parts[c % n_part] = parts[c % n_part] + hit[c * V7X_SUBLANES:(c + 1) * V7X_SUBLANES]
            return tuple(parts)

        parts = lax.fori_loop(0, n_kt, body,
                              tuple(jnp.zeros((V7X_SUBLANES, tq), I32) for _ in range(n_part)))
        return jnp.sum(sum(parts[1:], parts[0]), axis=0, keepdims=True)

    zero = jnp.zeros((1, tq), I32)
    n_pos = count_if(lambda keys, kt: keys >= zero)
    thr = jnp.where(n_pos >= topk, zero, zero + INT_MIN)

    def bisect(b, state):
        thr, n_ge = state
        cand = thr + lax.shift_left(jnp.int32(1), 30 - b)
        n_cand = count_if(lambda keys, kt: keys >= cand)
        ok = n_cand >= topk
        return jnp.where(ok, cand, thr), jnp.where(ok, n_cand, n_ge)

    thr, n_ge = lax.fori_loop(0, 31, bisect, (thr, n_pos))

    tied = jnp.logical_and(n_ge > topk, thr > INT_MIN)

    @pl.when(jnp.max(jnp.where(tied, 1, 0)) > 0)
    def _():
        n_idx_bits = max(1, (n_total - 1).bit_length())
        need = topk - count_if(lambda keys, kt: keys > thr)

        def n_tied_before(limit):
            return count_if(lambda keys, kt: jnp.logical_and(keys == thr, key_row + kt * tk < limit))

        def idx_bisect(b, last):
            cand = last + lax.shift_left(jnp.int32(1), n_idx_bits - 1 - b)
            return jnp.where(n_tied_before(cand) < need, cand, last)

        last = lax.fori_loop(0, n_idx_bits, idx_bisect, zero)

        def demote(kt, carry):
            keys = key_scr[kt]
            drop = jnp.logical_and(jnp.logical_and(tied, keys == thr), key_row + kt * tk > last)
            key_scr[kt] = jnp.where(drop, keys - 1, keys)
            return carry

        lax.fori_loop(0, n_kt, demote, 0)

    thr = jnp.maximum(thr, INT_MIN + 1)

    m_scr[...] = jnp.full(m_scr.shape, NEG_BIG, F32)
    l_scr[...] = jnp.zeros(l_scr.shape, F32)
    acc_scr[...] = jnp.zeros(acc_scr.shape, F32)

    def stage_a(kt):
        slot = kt & 1
        off = pl.multiple_of(kt * tk, tk)
        bias = jnp.where(key_scr[kt] >= thr, 0.0, NEG_BIG)
        bias = jnp.concatenate([bias] * rep, axis=1)
        for g in range(N_KV_A):
            qg = q_ref[g * rep:(g + 1) * rep].reshape(r_all, HEAD_DIM)
            s = lax.dot_general(k_ref[g, pl.ds(off, tk), :], qg, (((1,), (1,)), ((), ())),
                                preferred_element_type=F32) + bias
            m_old = m_scr[g]
            m_new = jnp.maximum(m_old, jnp.max(s, axis=0, keepdims=True))
            s_scr[slot, g] = s
            ma_scr[slot, g, 0:1] = m_new
            ma_scr[slot, g, 1:2] = jnp.exp2(m_old - m_new)
            m_scr[g] = m_new

    def stage_b(kt):
        slot = kt & 1
        for g in range(N_KV_A):
            m_new = ma_scr[slot, g, 0:1]
            alpha = ma_scr[slot, g, 1:2]
            p = jnp.exp2(s_scr[slot, g] - m_new)
            l_scr[g] = alpha * l_scr[g] + jnp.sum(p, axis=0, keepdims=True)
            acc_scr[g] = alpha * acc_scr[g] + jnp.dot(vt_ref[g, kt], p.astype(BF16),
                                                      preferred_element_type=F32)

    stage_a(0)

    def att_body(kt, carry):
        stage_b(kt - 1)
        stage_a(kt)
        return carry

    lax.fori_loop(1, n_kt, att_body, 0)
    stage_b(n_kt - 1)
    for g in range(N_KV_A):
        out_t = acc_scr[g] / l_scr[g]
        for r in range(rep):
            hh = g * rep + r
            head = _head_rms_t(out_t[:, r * tq:(r + 1) * tq], gt_ref[hh])
            o_ref[:, hh * HEAD_DIM:(hh + 1) * HEAD_DIM] = head.T.astype(o_ref.dtype)


def _dsa(iq, iwt, ik2, qa, ka, vat, g_t, topk):
    n_heads, s, _ = qa.shape
    rep = n_heads // N_KV_A
    tq, tk = DSA_TQ, DSA_TK
    assert s % tk == 0
    idx_scale = (IDX_HEADS ** -0.5) * (IDX_DIM ** -0.5)
    vmem = (4 * N_KV_A * s * HEAD_DIM * 2 * 2 + 2 * s * V7X_LANES * 2 + tq * s * 4
            + 32 * rep * tq * tk * 4)
    return pl.pallas_call(
        functools.partial(_dsa_kernel, topk=topk, rep=rep, idx_scale=idx_scale, n_total=s),
        grid=(s // tq,),
        in_specs=[
            pl.BlockSpec((tq, IDX_HEADS * IDX_DIM), lambda i: (i, 0)),
            pl.BlockSpec((IDX_HEADS, tq), lambda i: (0, i)),
            pl.BlockSpec((s, V7X_LANES), lambda i: (0, 0)),
            pl.BlockSpec((n_heads, tq, HEAD_DIM), lambda i: (0, i, 0)),
            pl.BlockSpec((N_KV_A, s, HEAD_DIM), lambda i: (0, 0, 0)),
            pl.BlockSpec((N_KV_A, s // tk, HEAD_DIM, tk), lambda i: (0, 0, 0, 0)),
            pl.BlockSpec((n_heads, HEAD_DIM, V7X_LANES), lambda i: (0, 0, 0)),
        ],
        out_specs=pl.BlockSpec((tq, n_heads * HEAD_DIM), lambda i: (i, 0)),
        out_shape=jax.ShapeDtypeStruct((s, n_heads * HEAD_DIM), BF16),
        scratch_shapes=[
            pltpu.VMEM((IDX_HEADS // 2, 2 * tq, V7X_LANES), BF16),
            pltpu.VMEM((s // tk, tk, tq), I32),
            pltpu.VMEM((N_KV_A, 1, rep * tq), F32),
            pltpu.VMEM((N_KV_A, 1, rep * tq), F32),
            pltpu.VMEM((N_KV_A, HEAD_DIM, rep * tq), F32),
            pltpu.VMEM((2, N_KV_A, tk, rep * tq), F32),
            pltpu.VMEM((2, N_KV_A, 2, rep * tq), F32),
        ],
        compiler_params=_cparams(("parallel",), vmem),
        name="dsa",
    )(iq, iwt, ik2, qa, ka, vat, g_t)


def _sb_kernel(q_ref, k_ref, vt_ref, gt_ref, o_ref, acc_ref):
    t = SB_T
    nrg = t // V7X_SUBLANES
    n_h = q_ref.shape[0]
    i = pl.program_id(1)
    acc_ref[...] = jnp.zeros(acc_ref.shape, F32)

    def local_scan(hh, j, diagonal):
        z = lax.dot_general(k_ref[hh, j], q_ref[hh], (((1,), (1,)), ((), ())),
                            preferred_element_type=F32)
        e = jnp.exp(-jnp.abs(z))
        r = 1.0 / (1.0 + e)
        er = e * r
        pos = z >= 0.0
        beta = jnp.where(pos, r, er)
        keep = jnp.where(pos, er, r)
        if diagonal:
            rr = lax.broadcasted_iota(I32, (t, t), 0)
            tt = lax.broadcasted_iota(I32, (t, t), 1)
            key_pos = (rr & (V7X_SUBLANES - 1)) * nrg + (rr >> int(math.log2(V7X_SUBLANES)))
            causal = key_pos < tt
            beta = jnp.where(causal, beta, 0.0)
            keep = jnp.where(causal, keep, 1.0)
        run8 = jnp.ones((V7X_SUBLANES, t), F32)
        a_loc = [None] * nrg
        for g in range(nrg - 1, -1, -1):
            rows = slice(g * V7X_SUBLANES, (g + 1) * V7X_SUBLANES)
            a_loc[g] = beta[rows] * run8
            run8 = run8 * keep[rows]
        return a_loc, run8

    def finish(hh, j, scan, carry):
        a_loc, run8 = scan
        run = carry
        offs = [None] * V7X_SUBLANES
        for u in range(V7X_SUBLANES - 1, -1, -1):
            offs[u] = run
            run = run * run8[u:u + 1]
        off8 = jnp.concatenate(offs, axis=0)
        a = jnp.concatenate([al * off8 for al in a_loc], axis=0)
        acc_ref[hh] += jnp.dot(vt_ref[hh, j], a.astype(BF16), preferred_element_type=F32)
        return run

    j_prev = jnp.maximum(i - 1, 0)
    has_prev = (i > 0).astype(F32)
    carries = []
    for hh in range(n_h):
        scan_diag = local_scan(hh, i, True)
        scan_prev = local_scan(hh, j_prev, False)
        carry = finish(hh, i, scan_diag, jnp.ones((1, t), F32)) * has_prev
        carries.append(finish(hh, j_prev, scan_prev, carry))

    def cond(state):
        j, cs = state
        alive = jnp.max(functools.reduce(jnp.maximum, cs)) > 0.0
        return jnp.logical_and(j >= 0, alive)

    def body(state):
        j, cs = state
        return j - 1, tuple(finish(hh, j, local_scan(hh, j, False), cs[hh]) for hh in range(n_h))

    lax.while_loop(cond, body, (i - 2, tuple(carries)))
    for hh in range(n_h):
        head = _head_rms_t(acc_ref[hh], gt_ref[hh])
        o_ref[:, hh * HEAD_DIM:(hh + 1) * HEAD_DIM] = head.T.astype(o_ref.dtype)


def _sb(q, k_perm, v_t, g_t):
    n_heads, s, _ = q.shape
    t = SB_T
    nb = s // t
    n_h = SB_HEADS if n_heads % SB_HEADS == 0 else 1
    return pl.pallas_call(
        _sb_kernel,
        grid=(n_heads // n_h, nb),
        in_specs=[
            pl.BlockSpec((n_h, t, HEAD_DIM), lambda h, i: (h, i, 0)),
            pl.BlockSpec((n_h, nb, t, HEAD_DIM), lambda h, i: (h, 0, 0, 0)),
            pl.BlockSpec((n_h, nb, HEAD_DIM, t), lambda h, i: (h, 0, 0, 0)),
            pl.BlockSpec((n_h, HEAD_DIM, V7X_LANES), lambda h, i: (h, 0, 0)),
        ],
        out_specs=pl.BlockSpec((t, n_h * HEAD_DIM), lambda h, i: (i, h)),
        out_shape=jax.ShapeDtypeStruct((s, n_heads * HEAD_DIM), BF16),
        scratch_shapes=[pltpu.VMEM((n_h, HEAD_DIM, t), F32)],
        compiler_params=_cparams(("parallel", "parallel"),
                                 n_h * (8 * s * HEAD_DIM * 2 + 40 * t * t * 4)),
        name="stick_breaking",
    )(q, k_perm, v_t, g_t)


def _cb_kernel(q_ref, k_ref, v_ref, b_ref, g_ref, o_ref, *, win, pad):
    tq = CB_TQ
    i = pl.program_id(1)
    off = pl.multiple_of(i * tq, tq)
    col = lax.broadcasted_iota(I32, (tq, win), 1)
    in_seq = col + off >= pad
    for hh in range(q_ref.shape[0]):
        k_w = k_ref[hh, pl.ds(off, win), :]
        v_w = v_ref[hh, pl.ds(off, win), :]
        s = lax.dot_general(q_ref[hh], k_w, (((1,), (1,)), ((), ())), preferred_element_type=F32)
        s = jnp.where(in_seq, s + b_ref[hh], NEG_BIG)
        m = jnp.max(s, axis=-1, keepdims=True)
        p = jnp.exp(s - m)
        l = jnp.sum(p, axis=-1, keepdims=True)
        o = jnp.dot(p.astype(BF16), v_w, preferred_element_type=F32) / l
        cols = slice(hh * HEAD_DIM, (hh + 1) * HEAD_DIM)
        ms = jnp.mean(o * o, axis=-1, keepdims=True)
        o_ref[:, cols] = (o * lax.rsqrt(ms + EPS) * g_ref[:, cols]).astype(o_ref.dtype)


def _cb(q, k_pad, v_pad, bias, gain, pad):
    n_heads, s, _ = q.shape
    tq = CB_TQ
    win = pad + tq
    sp = k_pad.shape[1]
    n_h = CB_HEADS if n_heads % CB_HEADS == 0 else 1
    return pl.pallas_call(
        functools.partial(_cb_kernel, win=win, pad=pad),
        grid=(n_heads // n_h, s // tq),
        in_specs=[
            pl.BlockSpec((n_h, tq, HEAD_DIM), lambda h, i: (h, i, 0)),
            pl.BlockSpec((n_h, sp, HEAD_DIM), lambda h, i: (h, 0, 0)),
            pl.BlockSpec((n_h, sp, HEAD_DIM), lambda h, i: (h, 0, 0)),
            pl.BlockSpec((n_h, tq, win), lambda h, i: (h, 0, 0)),
            pl.BlockSpec((1, n_h * HEAD_DIM), lambda h, i: (0, h)),
        ],
        out_specs=pl.BlockSpec((tq, n_h * HEAD_DIM), lambda h, i: (i, h)),
        out_shape=jax.ShapeDtypeStruct((s, n_heads * HEAD_DIM), BF16),
        compiler_params=_cparams(("parallel", "parallel"),
                                 n_h * (8 * sp * HEAD_DIM * 2 + 16 * tq * win * 4)),
        name="chunk_band",
    )(q, k_pad, v_pad, bias, gain.reshape(1, n_heads * HEAD_DIM))


def _cb_bias(rel_bias):
    tq = CB_TQ
    pad = N_LEFT_CHUNKS * CHUNK
    win = pad + tq
    n_heads = rel_bias.shape[0]
    qi = np.arange(tq)[:, None]
    kj = np.arange(win)[None, :]
    q_chunk = qi // CHUNK
    k_chunk = kj // CHUNK - N_LEFT_CHUNKS
    in_band = (k_chunk <= q_chunk) & (k_chunk >= q_chunk - N_LEFT_CHUNKS)
    p = win + tq
    n_clipped = pad + tq - 1 - REL_CLIP
    lo = 2 * REL_CLIP - (p - 1 - n_clipped)
    assert lo >= 0 and pad - (win - 1) >= -REL_CLIP
    u = jnp.concatenate([jnp.broadcast_to(rel_bias[:, 2 * REL_CLIP:], (n_heads, n_clipped)),
                         rel_bias[:, lo:2 * REL_CLIP + 1][:, ::-1]], axis=1).astype(F32)
    u_rot = jnp.concatenate([u[:, tq - 1:], u[:, :tq - 1]], axis=1)
    bias = jnp.tile(u_rot, (1, tq))[:, :tq * (p - 1)].reshape(n_heads, tq, p - 1)[:, :, :win]
    return jnp.where(jnp.asarray(in_band)[None], bias, NEG_BIG)


def _out_proj_kernel(a_ref, b_ref, c_ref, w_ref, r_ref, g_ref, o_ref):
    acc, row = None, 0
    for x_ref in (a_ref, b_ref, c_ref):
        part = _dot(x_ref[...], w_ref[row:row + x_ref.shape[1], :])
        acc = part if acc is None else acc + part
        row += x_ref.shape[1]
    o_ref[...] = r_ref[...] + g_ref[...] * acc


def _out_proj(o_a, o_b, o_c, w, l, resid, gate):
    m = o_a.shape[0]
    k, n = w.shape[1], w.shape[2]
    assert o_a.shape[1] + o_b.shape[1] + o_c.shape[1] == k
    tm, tn, tk = _mm_tiles(m, k, n, w.dtype.itemsize)
    assert tk == k
    lhs = lambda x: pl.BlockSpec((tm, x.shape[1]), lambda i, j: (i, 0))
    return pl.pallas_call(
        _out_proj_kernel,
        grid=(m // tm, n // tn),
        in_specs=[lhs(o_a), lhs(o_b), lhs(o_c),
                  pl.BlockSpec((None, k, tn), lambda i, j: (l, 0, j)),
                  pl.BlockSpec((tm, tn), lambda i, j: (i, j)),
                  pl.BlockSpec((1, tn), lambda i, j: (0, j))],
        out_specs=pl.BlockSpec((tm, tn), lambda i, j: (i, j)),
        out_shape=jax.ShapeDtypeStruct((m, n), F32),
        compiler_params=_cparams(("parallel", "parallel"),
                                 _mm_vmem(tm, tn, tk, w.dtype.itemsize) + 3 * tm * tn * 4),
        name="out_proj",
    )(o_a, o_b, o_c, w, resid, gate.reshape(1, n))


def _pack_kernel(x_ref, o_ref, *, runs):
    end = 0
    for src, dst, height in runs:
        if dst > end:
            o_ref[end:dst, :] = jnp.zeros((dst - end, o_ref.shape[1]), o_ref.dtype)
        o_ref[dst:dst + height, :] = x_ref[src:src + height, :].astype(o_ref.dtype)
        end = dst + height
    if end < o_ref.shape[0]:
        o_ref[end:, :] = jnp.zeros((o_ref.shape[0] - end, o_ref.shape[1]), o_ref.dtype)


def _cast_kernel(x_ref, o_ref):
    o_ref[...] = x_ref[...].astype(o_ref.dtype)


def _cast_layer(w, l, dtype):
    _, k, n = w.shape
    tr = _largest_tile(k, V7X_BF16_SUBLANES, 256)
    out = pl.pallas_call(
        _cast_kernel,
        grid=(k // tr,),
        in_specs=[pl.BlockSpec((None, tr, n), lambda i: (l, i, 0))],
        out_specs=pl.BlockSpec((tr, n), lambda i: (i, 0)),
        out_shape=jax.ShapeDtypeStruct((k, n), dtype),
        compiler_params=_cparams(("parallel",), 4 * tr * n * 4),
        name="cast_layer",
    )(w)
    return out.reshape(1, k, n)


def _packed_layout(sizes):
    units, src = [], 0
    for idx, width in enumerate(sizes):
        if width >= V7X_LANES or not units or units[-1][2] >= V7X_LANES:
            units.append([[idx], src, width])
        else:
            units[-1][0].append(idx)
            units[-1][2] += width
        src += width
    srcs = np.concatenate([[0], np.cumsum(sizes)[:-1]])
    offsets, runs, dst = [0] * len(sizes), [], 0
    for members, usrc, uwidth in sorted(units, key=lambda u: -u[2]):
        padded = -(-uwidth // V7X_LANES) * V7X_LANES
        dst = -(-dst // padded) * padded
        for idx in members:
            offsets[idx] = dst + int(srcs[idx]) - usrc
        runs.append((usrc, dst, uwidth))
        dst += padded
    return offsets, tuple(runs), -(-dst // PROJ_N_ALIGN) * PROJ_N_ALIGN


def _pack_w_in(w_t, l, sizes):
    offsets, runs, n_out = _packed_layout(sizes)
    assert all(v % V7X_BF16_SUBLANES == 0 for run in runs for v in run), "segments must be bf16-tile aligned"
    _, n_in, k = w_t.shape
    tc = _largest_tile(k, V7X_LANES, 256)
    packed = pl.pallas_call(
        functools.partial(_pack_kernel, runs=runs),
        grid=(k // tc,),
        in_specs=[pl.BlockSpec((None, n_in, tc), lambda i: (l, 0, i))],
        out_specs=pl.BlockSpec((n_out, tc), lambda i: (0, i)),
        out_shape=jax.ShapeDtypeStruct((n_out, k), BF16),
        compiler_params=_cparams(("parallel",), 3 * n_in * tc * 4 + 2 * n_out * tc * 2),
        name="pack_w_in",
    )(w_t)
    return packed.reshape(1, n_out, k), offsets


def kernel(x, c, w_ada, b_ada, norm_attn_g, w_in, rel_bias, head_norm_g, w_out, norm_ffn_g,
           w_gate_up, w_down, final_norm_g):
    b, s, d = x.shape
    assert b == 1, "single-sequence prefill only"
    depth = w_ada.shape[0]
    n_heads = d // HEAD_DIM
    ha = 3 * n_heads // 8
    hb = 3 * n_heads // 8
    hc = n_heads - ha - hb
    assert ha % N_KV_A == 0 and s % SB_T == 0
    sizes = (ha * HEAD_DIM, N_KV_A * HEAD_DIM, N_KV_A * HEAD_DIM, IDX_HEADS * IDX_DIM, IDX_DIM,
             IDX_HEADS, hb * HEAD_DIM, hb * HEAD_DIM, hb * HEAD_DIM,
             hc * HEAD_DIM, hc * HEAD_DIM, hc * HEAD_DIM)
    assert w_in.shape[2] == sum(sizes)
    topk = min(IDX_TOPK_MAX, s // 4)
    scale = HEAD_DIM ** -0.5
    pad = N_LEFT_CHUNKS * CHUNK

    rope128 = _rope_tables(s, HEAD_DIM, 1)
    rope64 = _rope_tables(s, IDX_DIM, V7X_LANES // IDX_DIM)

    xs = x.reshape(s, d)
    mod = _adaln(c, w_ada, b_ada)
    w_in_t = jnp.swapaxes(w_in, 1, 2)

    for l in range(depth):
        sh1, sc1, g1, sh2, sc2, g2 = [mod[l, :, n * d:(n + 1) * d] for n in range(6)]

        h = _norm(xs, norm_attn_g[l], sc1, sh1)
        w_packed, offs = _pack_w_in(w_in_t, l, sizes)
        o_qa, o_ka, o_va, o_iq, o_ik, o_iw, o_qb, o_kb, o_vb, o_qc, o_kc, o_vc = offs
        assert o_iw == o_ik + IDX_DIM and o_ik % V7X_LANES == 0
        proj = _matmul_nt(h, w_packed, 0, F32)

        qa = _prep_heads(proj, o_qa, ha, rope128, HEAD_DIM, scale * math.log2(math.e))
        ka = _prep_heads(proj, o_ka, N_KV_A, rope128, HEAD_DIM)
        va = _prep_heads(proj, o_va, N_KV_A)
        vat = va.reshape(N_KV_A, s // DSA_TK, DSA_TK, HEAD_DIM).swapaxes(2, 3)
        iq = _prep_heads(proj, o_iq, IDX_HEADS * IDX_DIM // V7X_LANES, rope64, IDX_DIM,
                         head_major=False)
        ik2, iw = _prep_ikw(proj, o_ik, rope64)
        gain = head_norm_g[l]
        gain_t = jnp.broadcast_to(gain.reshape(n_heads, HEAD_DIM, 1), (n_heads, HEAD_DIM, V7X_LANES))
        o_a = _dsa(iq, iw[:, :IDX_HEADS].T, ik2, qa, ka, vat, gain_t[:ha], topk)

        qb = _prep_heads(proj, o_qb, hb, scale=scale)
        kb = _prep_heads(proj, o_kb, hb)
        vb = _prep_heads(proj, o_vb, hb)
        nb, sub = s // SB_T, V7X_SUBLANES
        kb = kb.reshape(hb, nb, sub, SB_T // sub, HEAD_DIM).swapaxes(2, 3).reshape(hb, nb, SB_T, HEAD_DIM)
        vbt = vb.reshape(hb, nb, sub, SB_T // sub, HEAD_DIM).swapaxes(2, 3).reshape(hb, nb, SB_T, HEAD_DIM)
        vbt = vbt.swapaxes(2, 3)
        o_b = _sb(qb, kb, vbt, gain_t[ha:ha + hb])

        qc = _prep_heads(proj, o_qc, hc, scale=scale)
        kc = _prep_heads(proj, o_kc, hc)
        vc = _prep_heads(proj, o_vc, hc)
        kc = jnp.pad(kc, ((0, 0), (pad, 0), (0, 0)))
        vc = jnp.pad(vc, ((0, 0), (pad, 0), (0, 0)))
        o_c = _cb(qc, kc, vc, _cb_bias(rel_bias[l]), gain[(ha + hb) * HEAD_DIM:], pad)

        xs = _out_proj(o_a, o_b, o_c, _cast_layer(w_out, l, BF16), 0, xs, g1)

        h = _norm(xs, norm_ffn_g[l], sc2, sh2)
        act = _matmul_swiglu(h, w_gate_up, l)
        xs = _matmul_resid(act, _cast_layer(w_down, l, BF16), 0, xs, g2)

    out = _norm(xs, final_norm_g, out_dtype=x.dtype)
    return out.reshape(b, s, d)
```

```python
import functools
import math

import numpy as np
import jax
import jax.numpy as jnp
from jax import lax
from jax.experimental import pallas as pl
from jax.experimental.pallas import tpu as pltpu

F32 = jnp.float32
BF16 = jnp.bfloat16
I32 = jnp.int32

HEAD_DIM = 128
N_KV_A = 2
IDX_HEADS = 16
IDX_DIM = 64
IDX_TOPK_MAX = 256
CHUNK = 64
N_LEFT_CHUNKS = 8
REL_CLIP = 256
ROPE_THETA = 10000.0
EPS = 1e-6

V7X_LANES = 128
V7X_SUBLANES = 8
V7X_BF16_SUBLANES = 16
V7X_VMEM_BYTES = 64 * 1024 * 1024
V7X_MXU_COLS = 256

PROJ_N_ALIGN = 2 * V7X_MXU_COLS

NEG_BIG = -1e30
INT_MIN = -(2 ** 31)

DSA_TQ = 256
DSA_TK = 512
SB_T = 256
SB_HEADS = 4
CB_TQ = 256
CB_HEADS = 4


def _cparams(semantics, vmem_bytes):
    limit = int(min(V7X_VMEM_BYTES * 7 // 8, max(vmem_bytes, 16 * 1024 * 1024)))
    return pltpu.CompilerParams(dimension_semantics=semantics, vmem_limit_bytes=limit)


def _largest_tile(n, unit, cap):
    best = None
    t = unit
    while t <= min(n, cap):
        if n % t == 0:
            best = t
        t += unit
    assert best is not None, (n, unit, cap)
    return best


def _adaln_kernel(c_ref, w_ref, b_ref, o_ref, acc_ref):
    k = pl.program_id(1)
    c = c_ref[...]
    cond = c * jax.nn.sigmoid(c)
    tr, n = w_ref.shape

    @pl.when(k == 0)
    def _():
        acc_ref[...] = jnp.zeros_like(acc_ref)

    for j in range(n // V7X_LANES):
        cols = slice(j * V7X_LANES, (j + 1) * V7X_LANES)
        prod = w_ref[:, cols] * cond
        part = prod[0:V7X_SUBLANES]
        for g in range(1, tr // V7X_SUBLANES):
            part = part + prod[g * V7X_SUBLANES:(g + 1) * V7X_SUBLANES]
        acc_ref[:, cols] += part

    @pl.when(k == pl.num_programs(1) - 1)
    def _():
        o_ref[...] = jnp.sum(acc_ref[...], axis=0, keepdims=True) + b_ref[...]


def _adaln(c, w_ada, b_ada):
    depth, d, n = w_ada.shape
    tr = _largest_tile(d, V7X_SUBLANES, 128)
    c_b = jnp.broadcast_to(c.reshape(d, 1), (d, V7X_LANES))
    out = pl.pallas_call(
        _adaln_kernel,
        grid=(depth, d // tr),
        in_specs=[
            pl.BlockSpec((tr, V7X_LANES), lambda l, k: (k, 0)),
            pl.BlockSpec((None, tr, n), lambda l, k: (l, k, 0)),
            pl.BlockSpec((None, 1, n), lambda l, k: (l, 0, 0)),
        ],
        out_specs=pl.BlockSpec((None, 1, n), lambda l, k: (l, 0, 0)),
        out_shape=jax.ShapeDtypeStruct((depth, 1, n), F32),
        scratch_shapes=[pltpu.VMEM((V7X_SUBLANES, n), F32)],
        compiler_params=_cparams(("parallel", "arbitrary"), 3 * tr * n * 4),
        name="adaln",
    )(c_b, w_ada, b_ada.reshape(depth, 1, n))
    return out


def _norm_kernel(x_ref, g_ref, *rest, modulate):
    o_ref = rest[-1]
    x = x_ref[...]
    ms = jnp.mean(x * x, axis=-1, keepdims=True)
    y = x * lax.rsqrt(ms + EPS) * g_ref[...]
    if modulate:
        sc_ref, sh_ref = rest[0], rest[1]
        y = y * (1.0 + sc_ref[...]) + sh_ref[...]
    o_ref[...] = y.astype(o_ref.dtype)


def _norm(x, g, scale=None, shift=None, out_dtype=BF16):
    s, d = x.shape
    ts = _largest_tile(s, V7X_SUBLANES, 512)
    modulate = scale is not None
    row = pl.BlockSpec((1, d), lambda i: (0, 0))
    args = [x, g.reshape(1, d)]
    specs = [pl.BlockSpec((ts, d), lambda i: (i, 0)), row]
    if modulate:
        args += [scale.reshape(1, d), shift.reshape(1, d)]
        specs += [row, row]
    return pl.pallas_call(
        functools.partial(_norm_kernel, modulate=modulate),
        grid=(s // ts,),
        in_specs=specs,
        out_specs=pl.BlockSpec((ts, d), lambda i: (i, 0)),
        out_shape=jax.ShapeDtypeStruct((s, d), out_dtype),
        compiler_params=_cparams(("parallel",), 6 * ts * d * 4),
        name="rmsnorm",
    )(*args)


def _dot(a, w):
    return jnp.dot(a, w.astype(BF16), preferred_element_type=F32)


def _mm_plain_kernel(a_ref, wt_ref, o_ref):
    o_ref[...] = lax.dot_general(a_ref[...], wt_ref[...].astype(BF16), (((1,), (1,)), ((), ())),
                                 preferred_element_type=F32).astype(o_ref.dtype)


def _mm_swiglu_kernel(a_ref, wg_ref, wu_ref, o_ref):
    a = a_ref[...]
    gate = _dot(a, wg_ref[...])
    up = _dot(a, wu_ref[...])
    o_ref[...] = (gate * jax.nn.sigmoid(gate) * up).astype(o_ref.dtype)


def _mm_resid_kernel(a_ref, w_ref, r_ref, g_ref, o_ref, *acc, nk):
    if nk == 1:
        o_ref[...] = r_ref[...] + g_ref[...] * _dot(a_ref[...], w_ref[...])
        return
    acc_ref, = acc
    k = pl.program_id(2)

    @pl.when(k == 0)
    def _():
        acc_ref[...] = jnp.zeros_like(acc_ref)

    acc_ref[...] += _dot(a_ref[...], w_ref[...])

    @pl.when(k == nk - 1)
    def _():
        o_ref[...] = r_ref[...] + g_ref[...] * acc_ref[...]


def _mm_tiles(m, k, n, w_itemsize):
    tn = _largest_tile(n, V7X_LANES, 768 if w_itemsize == 2 else 256)
    if k <= 4096:
        return _largest_tile(m, V7X_SUBLANES, 1024), tn, k
    if w_itemsize == 2:
        return _largest_tile(m, V7X_SUBLANES, 512), _largest_tile(n, V7X_LANES, 512), k
    return _largest_tile(m, V7X_SUBLANES, 1024), tn, _largest_tile(k, V7X_LANES, 5632)


def _mm_vmem(tm, tn, tk, w_itemsize, n_w=1):
    return (2 * tm * tk * 2 + n_w * (2 * tk * tn * w_itemsize + tk * tn * 2)
            + (4 + 2 * n_w) * tm * tn * 4)


def _matmul_nt(a, w_t, l, out_dtype):
    m, k = a.shape
    n = w_t.shape[1]
    tm, tn, tk = _mm_tiles(m, k, n, w_t.dtype.itemsize)
    assert tk == k
    return pl.pallas_call(
        _mm_plain_kernel,
        grid=(m // tm, n // tn),
        in_specs=[pl.BlockSpec((tm, k), lambda i, j: (i, 0)),
                  pl.BlockSpec((None, tn, k), lambda i, j: (l, j, 0))],
        out_specs=pl.BlockSpec((tm, tn), lambda i, j: (i, j)),
        out_shape=jax.ShapeDtypeStruct((m, n), out_dtype),
        compiler_params=_cparams(("parallel", "parallel"), _mm_vmem(tm, tn, tk, w_t.dtype.itemsize)),
        name="matmul",
    )(a, w_t)


def _matmul_swiglu(a, w_gate_up, l):
    m, k = a.shape
    n = w_gate_up.shape[2] // 2
    tm, tn, tk = _mm_tiles(m, k, n, w_gate_up.dtype.itemsize)
    assert tk == k
    nb = n // tn
    return pl.pallas_call(
        _mm_swiglu_kernel,
        grid=(m // tm, nb),
        in_specs=[pl.BlockSpec((tm, k), lambda i, j: (i, 0)),
                  pl.BlockSpec((None, k, tn), lambda i, j: (l, 0, j)),
                  pl.BlockSpec((None, k, tn), lambda i, j: (l, 0, j + nb))],
        out_specs=pl.BlockSpec((tm, tn), lambda i, j: (i, j)),
        out_shape=jax.ShapeDtypeStruct((m, n), BF16),
        compiler_params=_cparams(("parallel", "parallel"),
                                 _mm_vmem(tm, tn, tk, w_gate_up.dtype.itemsize, n_w=2)),
        name="matmul_swiglu",
    )(a, w_gate_up, w_gate_up)


def _matmul_resid(a, w, l, resid, gate):
    m, k = a.shape
    n = w.shape[2]
    tm, tn, tk = _mm_tiles(m, k, n, w.dtype.itemsize)
    nk = k // tk
    scratch = [pltpu.VMEM((tm, tn), F32)] if nk > 1 else []
    return pl.pallas_call(
        functools.partial(_mm_resid_kernel, nk=nk),
        grid=(m // tm, n // tn, nk),
        in_specs=[pl.BlockSpec((tm, tk), lambda i, j, kk: (i, kk)),
                  pl.BlockSpec((None, tk, tn), lambda i, j, kk: (l, kk, j)),
                  pl.BlockSpec((tm, tn), lambda i, j, kk: (i, j)),
                  pl.BlockSpec((1, tn), lambda i, j, kk: (0, j))],
        out_specs=pl.BlockSpec((tm, tn), lambda i, j, kk: (i, j)),
        out_shape=jax.ShapeDtypeStruct((m, n), F32),
        scratch_shapes=scratch,
        compiler_params=_cparams(("parallel", "parallel", "arbitrary"),
                                 _mm_vmem(tm, tn, tk, w.dtype.itemsize) + 3 * tm * tn * 4),
        name="matmul_resid",
    )(a, w, resid, gate.reshape(1, n))


def _rope_tables(s, d, lane_groups):
    pos = jnp.arange(s, dtype=jnp.int32).astype(F32)
    inv = ROPE_THETA ** (-jnp.arange(0, d, 2, dtype=F32) / d)
    ang = pos[:, None] * inv[None, :]
    cos, sin = jnp.cos(ang), jnp.sin(ang)
    cos_t = jnp.tile(jnp.concatenate([cos, cos], axis=1), (1, lane_groups))
    sin_t = jnp.tile(jnp.concatenate([-sin, sin], axis=1), (1, lane_groups))
    return cos_t, sin_t


def _rot_half(x, d):
    if d == V7X_LANES:
        return pltpu.roll(x, d // 2, 1)
    lane = lax.broadcasted_iota(I32, x.shape, 1)
    first = (lane & (d - 1)) < d // 2
    return jnp.where(first, pltpu.roll(x, V7X_LANES - d // 2, 1), pltpu.roll(x, d // 2, 1))


def _prep_kernel(x_ref, *rest, rope_dim, scale, head_major):
    o_ref = rest[-1]
    if rope_dim:
        cos, sin = rest[0][...], rest[1][...]
    for h in range(x_ref.shape[1] // V7X_LANES):
        cols = slice(h * V7X_LANES, (h + 1) * V7X_LANES)
        x = x_ref[:, cols]
        if rope_dim:
            x = x * cos + _rot_half(x, rope_dim) * sin
        if scale != 1.0:
            x = x * scale
        if head_major:
            o_ref[h] = x.astype(o_ref.dtype)
        else:
            o_ref[:, cols] = x.astype(o_ref.dtype)


def _prep_heads(proj, col0, n_blocks, tables=None, rope_dim=0, scale=1.0, head_major=True):
    s = proj.shape[0]
    width = n_blocks * V7X_LANES
    assert col0 % width == 0 or n_blocks == 1, "segment must start at a multiple of its width"
    ts = _largest_tile(s, V7X_SUBLANES, 512)
    args = [proj]
    specs = [pl.BlockSpec((ts, width), lambda i: (i, col0 // width))]
    if rope_dim:
        args += list(tables)
        specs += [pl.BlockSpec((ts, V7X_LANES), lambda i: (i, 0))] * 2
    if head_major:
        out_spec = pl.BlockSpec((n_blocks, ts, V7X_LANES), lambda i: (0, i, 0))
        out_shape = jax.ShapeDtypeStruct((n_blocks, s, V7X_LANES), BF16)
    else:
        out_spec = pl.BlockSpec((ts, width), lambda i: (i, 0))
        out_shape = jax.ShapeDtypeStruct((s, width), BF16)
    return pl.pallas_call(
        functools.partial(_prep_kernel, rope_dim=rope_dim, scale=scale, head_major=head_major),
        grid=(s // ts,),
        in_specs=specs,
        out_specs=out_spec,
        out_shape=out_shape,
        compiler_params=_cparams(("parallel",), 8 * ts * width * 4),
        name="prep_heads",
    )(*args)


def _ikw_kernel(x_ref, cos_ref, sin_ref, ik_ref, iw_ref):
    x = x_ref[...]
    y = x * cos_ref[...] + _rot_half(x, IDX_DIM) * sin_ref[...]
    lane = lax.broadcasted_iota(I32, x.shape, 1)
    ik_ref[...] = jnp.where(lane < IDX_DIM, y, pltpu.roll(y, IDX_DIM, 1)).astype(ik_ref.dtype)
    iw_ref[...] = pltpu.roll(x, V7X_LANES - IDX_DIM, 1)


def _prep_ikw(proj, col0, tables):
    s = proj.shape[0]
    ts = _largest_tile(s, V7X_SUBLANES, 1024)
    cb = col0 // V7X_LANES
    blk = pl.BlockSpec((ts, V7X_LANES), lambda i: (i, 0))
    return pl.pallas_call(
        _ikw_kernel,
        grid=(s // ts,),
        in_specs=[pl.BlockSpec((ts, V7X_LANES), lambda i: (i, cb)), blk, blk],
        out_specs=[blk, blk],
        out_shape=[jax.ShapeDtypeStruct((s, V7X_LANES), BF16),
                   jax.ShapeDtypeStruct((s, V7X_LANES), F32)],
        compiler_params=_cparams(("parallel",), 16 * ts * V7X_LANES * 4),
        name="prep_ikw",
    )(proj, *tables)


def _head_rms_t(x_t, g_t):
    ms = jnp.mean(x_t * x_t, axis=0, keepdims=True)
    y = x_t * lax.rsqrt(ms + EPS)
    reps = x_t.shape[1] // g_t.shape[1]
    return y * (g_t if reps == 1 else jnp.concatenate([g_t] * reps, axis=1))


def _dsa_kernel(iq_ref, iwt_ref, ik_ref, q_ref, k_ref, vt_ref, gt_ref, o_ref,
                rhs_scr, key_scr, m_scr, l_scr, acc_scr, s_scr, ma_scr,
                *, topk, rep, idx_scale, n_total):
    tq, tk = DSA_TQ, DSA_TK
    nsub = tk // V7X_SUBLANES
    r_all = rep * tq
    i = pl.program_id(0)
    start = i * tq
    n_kt = (start + tq + tk - 1) >> int(math.log2(tk))

    lane = lax.broadcasted_iota(I32, (tq, V7X_LANES), 1)
    for p in range(IDX_HEADS // 2):
        pair = iq_ref[:, p * V7X_LANES:(p + 1) * V7X_LANES]
        zero = jnp.zeros_like(pair)
        rhs_scr[p, :tq] = jnp.where(lane < IDX_DIM, pair, zero)
        rhs_scr[p, tq:] = jnp.where(lane >= IDX_DIM, pair, zero)

    iwt = iwt_ref[...]
    key_row = lax.broadcasted_iota(I32, (tk, tq), 0)
    q_lane = lax.broadcasted_iota(I32, (tk, tq), 1)
    key_lim = start + ((q_lane >> int(math.log2(CHUNK))) + 1) * CHUNK

    def score_body(kt, carry):
        off = pl.multiple_of(kt * tk, tk)
        ik_t = ik_ref[pl.ds(off, tk), :]
        acc = jnp.zeros((tk, tq), F32)
        for p in range(IDX_HEADS // 2):
            d = lax.dot_general(ik_t, rhs_scr[p], (((1,), (1,)), ((), ())),
                                preferred_element_type=F32)
            for half in range(2):
                h = 2 * p + half
                acc = acc + iwt[h:h + 1, :] * jnp.maximum(d[:, half * tq:(half + 1) * tq], 0.0)
        bits = pltpu.bitcast(acc * idx_scale, I32)
        key = jnp.where(bits < 0, bits ^ 0x7FFFFFFF, bits)
        key_scr[kt] = jnp.where(key_row + off < key_lim, key, INT_MIN)
        return carry

    lax.fori_loop(0, n_kt, score_body, 0)

    n_part = 4

    def count_if(pred):
        def body(kt, parts):
            hit = jnp.where(pred(key_scr[kt], kt), 1, 0)
            parts = list(parts)
            for c in range(nsub):
                parts[c % n_part] = parts[c % n_part] + hit[c * V7X_SUBLANES:(c + 1) * V7X_SUBLANES]
            return tuple(parts)

        parts = lax.fori_loop(0, n_kt, body,
                              tuple(jnp.zeros((V7X_SUBLANES, tq), I32) for _ in range(n_part)))
        return jnp.sum(sum(parts[1:], parts[0]), axis=0, keepdims=True)

    zero = jnp.zeros((1, tq), I32)
    n_pos = count_if(lambda keys, kt: keys >= zero)
    thr = jnp.where(n_pos >= topk, zero, zero + INT_MIN)

    def bisect(b, state):
        thr, n_ge = state
        cand = thr + lax.shift_left(jnp.int32(1), 30 - b)
        n_cand = count_if(lambda keys, kt: keys >= cand)
        ok = n_cand >= topk
        return jnp.where(ok, cand, thr), jnp.where(ok, n_cand, n_ge)

    thr, n_ge = lax.fori_loop(0, 31, bisect, (thr, n_pos))

    tied = jnp.logical_and(n_ge > topk, thr > INT_MIN)

    @pl.when(jnp.max(jnp.where(tied, 1, 0)) > 0)
    def _():
        n_idx_bits = max(1, (n_total - 1).bit_length())
        need = topk - count_if(lambda keys, kt: keys > thr)

        def n_tied_before(limit):
            return count_if(lambda keys, kt: jnp.logical_and(keys == thr, key_row + kt * tk < limit))

        def idx_bisect(b, last):
            cand = last + lax.shift_left(jnp.int32(1), n_idx_bits - 1 - b)
            return jnp.where(n_tied_before(cand) < need, cand, last)

        last = lax.fori_loop(0, n_idx_bits, idx_bisect, zero)

        def demote(kt, carry):
            keys = key_scr[kt]
            drop = jnp.logical_and(jnp.logical_and(tied, keys == thr), key_row + kt * tk > last)
            key_scr[kt] = jnp.where(drop, keys - 1, keys)
            return carry

        lax.fori_loop(0, n_kt, demote, 0)

    thr = jnp.maximum(thr, INT_MIN + 1)

    m_scr[...] = jnp.full(m_scr.shape, NEG_BIG, F32)
    l_scr[...] = jnp.zeros(l_scr.shape, F32)
    acc_scr[...] = jnp.zeros(acc_scr.shape, F32)

    def stage_a(kt):
        slot = kt & 1
        off = pl.multiple_of(kt * tk, tk)
        bias = jnp.where(key_scr[kt] >= thr, 0.0, NEG_BIG)
        bias = jnp.concatenate([bias] * rep, axis=1)
        for g in range(N_KV_A):
            qg = q_ref[g * rep:(g + 1) * rep].reshape(r_all, HEAD_DIM)
            s = lax.dot_general(k_ref[g, pl.ds(off, tk), :], qg, (((1,), (1,)), ((), ())),
                                preferred_element_type=F32) + bias
            m_old = m_scr[g]
            m_new = jnp.maximum(m_old, jnp.max(s, axis=0, keepdims=True))
            s_scr[slot, g] = s
            ma_scr[slot, g, 0:1] = m_new
            ma_scr[slot, g, 1:2] = jnp.exp2(m_old - m_new)
            m_scr[g] = m_new

    def stage_b(kt):
        slot = kt & 1
        for g in range(N_KV_A):
            m_new = ma_scr[slot, g, 0:1]
            alpha = ma_scr[slot, g, 1:2]
            p = jnp.exp2(s_scr[slot, g] - m_new)
            l_scr[g] = alpha * l_scr[g] + jnp.sum(p, axis=0, keepdims=True)
            acc_scr[g] = alpha * acc_scr[g] + jnp.dot(vt_ref[g, kt], p.astype(BF16),
                                                      preferred_element_type=F32)

    stage_a(0)

    def att_body(kt, carry):
        stage_b(kt - 1)
        stage_a(kt)
        return carry

    lax.fori_loop(1, n_kt, att_body, 0)
    stage_b(n_kt - 1)
    for g in range(N_KV_A):
        out_t = acc_scr[g] / l_scr[g]
        for r in range(rep):
            hh = g * rep + r
            head = _head_rms_t(out_t[:, r * tq:(r + 1) * tq], gt_ref[hh])
            o_ref[:, hh * HEAD_DIM:(hh + 1) * HEAD_DIM] = head.T.astype(o_ref.dtype)


def _dsa(iq, iwt, ik2, qa, ka, vat, g_t, topk):
    n_heads, s, _ = qa.shape
    rep = n_heads // N_KV_A
    tq, tk = DSA_TQ, DSA_TK
    assert s % tk == 0
    idx_scale = (IDX_HEADS ** -0.5) * (IDX_DIM ** -0.5)
    vmem = (4 * N_KV_A * s * HEAD_DIM * 2 * 2 + 2 * s * V7X_LANES * 2 + tq * s * 4
            + 32 * rep * tq * tk * 4)
    return pl.pallas_call(
        functools.partial(_dsa_kernel, topk=topk, rep=rep, idx_scale=idx_scale, n_total=s),
        grid=(s // tq,),
        in_specs=[
            pl.BlockSpec((tq, IDX_HEADS * IDX_DIM), lambda i: (i, 0)),
            pl.BlockSpec((IDX_HEADS, tq), lambda i: (0, i)),
            pl.BlockSpec((s, V7X_LANES), lambda i: (0, 0)),
            pl.BlockSpec((n_heads, tq, HEAD_DIM), lambda i: (0, i, 0)),
            pl.BlockSpec((N_KV_A, s, HEAD_DIM), lambda i: (0, 0, 0)),
            pl.BlockSpec((N_KV_A, s // tk, HEAD_DIM, tk), lambda i: (0, 0, 0, 0)),
            pl.BlockSpec((n_heads, HEAD_DIM, V7X_LANES), lambda i: (0, 0, 0)),
        ],
        out_specs=pl.BlockSpec((tq, n_heads * HEAD_DIM), lambda i: (i, 0)),
        out_shape=jax.ShapeDtypeStruct((s, n_heads * HEAD_DIM), BF16),
        scratch_shapes=[
            pltpu.VMEM((IDX_HEADS // 2, 2 * tq, V7X_LANES), BF16),
            pltpu.VMEM((s // tk, tk, tq), I32),
            pltpu.VMEM((N_KV_A, 1, rep * tq), F32),
            pltpu.VMEM((N_KV_A, 1, rep * tq), F32),
            pltpu.VMEM((N_KV_A, HEAD_DIM, rep * tq), F32),
            pltpu.VMEM((2, N_KV_A, tk, rep * tq), F32),
            pltpu.VMEM((2, N_KV_A, 2, rep * tq), F32),
        ],
        compiler_params=_cparams(("parallel",), vmem),
        name="dsa",
    )(iq, iwt, ik2, qa, ka, vat, g_t)


def _sb_kernel(q_ref, k_ref, vt_ref, gt_ref, o_ref, acc_ref):
    t = SB_T
    nrg = t // V7X_SUBLANES
    n_h = q_ref.shape[0]
    i = pl.program_id(1)
    acc_ref[...] = jnp.zeros(acc_ref.shape, F32)

    def local_scan(hh, j, diagonal):
        z = lax.dot_general(k_ref[hh, j], q_ref[hh], (((1,), (1,)), ((), ())),
                            preferred_element_type=F32)
        e = jnp.exp(-jnp.abs(z))
        r = 1.0 / (1.0 + e)
        er = e * r
        pos = z >= 0.0
        beta = jnp.where(pos, r, er)
        keep = jnp.where(pos, er, r)
        if diagonal:
            rr = lax.broadcasted_iota(I32, (t, t), 0)
            tt = lax.broadcasted_iota(I32, (t, t), 1)
            key_pos = (rr & (V7X_SUBLANES - 1)) * nrg + (rr >> int(math.log2(V7X_SUBLANES)))
            causal = key_pos < tt
            beta = jnp.where(causal, beta, 0.0)
            keep = jnp.where(causal, keep, 1.0)
        run8 = jnp.ones((V7X_SUBLANES, t), F32)
        a_loc = [None] * nrg
        for g in range(nrg - 1, -1, -1):
            rows = slice(g * V7X_SUBLANES, (g + 1) * V7X_SUBLANES)
            a_loc[g] = beta[rows] * run8
            run8 = run8 * keep[rows]
        return a_loc, run8

    def finish(hh, j, scan, carry):
        a_loc, run8 = scan
        run = carry
        offs = [None] * V7X_SUBLANES
        for u in range(V7X_SUBLANES - 1, -1, -1):
            offs[u] = run
            run = run * run8[u:u + 1]
        off8 = jnp.concatenate(offs, axis=0)
        a = jnp.concatenate([al * off8 for al in a_loc], axis=0)
        acc_ref[hh] += jnp.dot(vt_ref[hh, j], a.astype(BF16), preferred_element_type=F32)
        return run

    j_prev = jnp.maximum(i - 1, 0)
    has_prev = (i > 0).astype(F32)
    carries = []
    for hh in range(n_h):
        scan_diag = local_scan(hh, i, True)
        scan_prev = local_scan(hh, j_prev, False)
        carry = finish(hh, i, scan_diag, jnp.ones((1, t), F32)) * has_prev
        carries.append(finish(hh, j_prev, scan_prev, carry))

    def cond(state):
        j, cs = state
        alive = jnp.max(functools.reduce(jnp.maximum, cs)) > 0.0
        return jnp.logical_and(j >= 0, alive)

    def body(state):
        j, cs = state
        return j - 1, tuple(finish(hh, j, local_scan(hh, j, False), cs[hh]) for hh in range(n_h))

    lax.while_loop(cond, body, (i - 2, tuple(carries)))
    for hh in range(n_h):
        head = _head_rms_t(acc_ref[hh], gt_ref[hh])
        o_ref[:, hh * HEAD_DIM:(hh + 1) * HEAD_DIM] = head.T.astype(o_ref.dtype)


def _sb(q, k_perm, v_t, g_t):
    n_heads, s, _ = q.shape
    t = SB_T
    nb = s // t
    n_h = SB_HEADS if n_heads % SB_HEADS == 0 else 1
    return pl.pallas_call(
        _sb_kernel,
        grid=(n_heads // n_h, nb),
        in_specs=[
            pl.BlockSpec((n_h, t, HEAD_DIM), lambda h, i: (h, i, 0)),
            pl.BlockSpec((n_h, nb, t, HEAD_DIM), lambda h, i: (h, 0, 0, 0)),
            pl.BlockSpec((n_h, nb, HEAD_DIM, t), lambda h, i: (h, 0, 0, 0)),
            pl.BlockSpec((n_h, HEAD_DIM, V7X_LANES), lambda h, i: (h, 0, 0)),
        ],
        out_specs=pl.BlockSpec((t, n_h * HEAD_DIM), lambda h, i: (i, h)),
        out_shape=jax.ShapeDtypeStruct((s, n_heads * HEAD_DIM), BF16),
        scratch_shapes=[pltpu.VMEM((n_h, HEAD_DIM, t), F32)],
        compiler_params=_cparams(("parallel", "parallel"),
                                 n_h * (8 * s * HEAD_DIM * 2 + 40 * t * t * 4)),
        name="stick_breaking",
    )(q, k_perm, v_t, g_t)


def _cb_kernel(q_ref, k_ref, v_ref, b_ref, g_ref, o_ref, *, win, pad):
    tq = CB_TQ
    i = pl.program_id(1)
    off = pl.multiple_of(i * tq, tq)
    col = lax.broadcasted_iota(I32, (tq, win), 1)
    in_seq = col + off >= pad
    for hh in range(q_ref.shape[0]):
        k_w = k_ref[hh, pl.ds(off, win), :]
        v_w = v_ref[hh, pl.ds(off, win), :]
        s = lax.dot_general(q_ref[hh], k_w, (((1,), (1,)), ((), ())), preferred_element_type=F32)
        s = jnp.where(in_seq, s + b_ref[hh], NEG_BIG)
        m = jnp.max(s, axis=-1, keepdims=True)
        p = jnp.exp(s - m)
        l = jnp.sum(p, axis=-1, keepdims=True)
        o = jnp.dot(p.astype(BF16), v_w, preferred_element_type=F32) / l
        cols = slice(hh * HEAD_DIM, (hh + 1) * HEAD_DIM)
        ms = jnp.mean(o * o, axis=-1, keepdims=True)
        o_ref[:, cols] = (o * lax.rsqrt(ms + EPS) * g_ref[:, cols]).astype(o_ref.dtype)


def _cb(q, k_pad, v_pad, bias, gain, pad):
    n_heads, s, _ = q.shape
    tq = CB_TQ
    win = pad + tq
    sp = k_pad.shape[1]
    n_h = CB_HEADS if n_heads % CB_HEADS == 0 else 1
    return pl.pallas_call(
        functools.partial(_cb_kernel, win=win, pad=pad),
        grid=(n_heads // n_h, s // tq),
        in_specs=[
            pl.BlockSpec((n_h, tq, HEAD_DIM), lambda h, i: (h, i, 0)),
            pl.BlockSpec((n_h, sp, HEAD_DIM), lambda h, i: (h, 0, 0)),
            pl.BlockSpec((n_h, sp, HEAD_DIM), lambda h, i: (h, 0, 0)),
            pl.BlockSpec((n_h, tq, win), lambda h, i: (h, 0, 0)),
            pl.BlockSpec((1, n_h * HEAD_DIM), lambda h, i: (0, h)),
        ],
        out_specs=pl.BlockSpec((tq, n_h * HEAD_DIM), lambda h, i: (i, h)),
        out_shape=jax.ShapeDtypeStruct((s, n_heads * HEAD_DIM), BF16),
        compiler_params=_cparams(("parallel", "parallel"),
                                 n_h * (8 * sp * HEAD_DIM * 2 + 16 * tq * win * 4)),
        name="chunk_band",
    )(q, k_pad, v_pad, bias, gain.reshape(1, n_heads * HEAD_DIM))


def _cb_bias(rel_bias):
    tq = CB_TQ
    pad = N_LEFT_CHUNKS * CHUNK
    win = pad + tq
    n_heads = rel_bias.shape[0]
    qi = np.arange(tq)[:, None]
    kj = np.arange(win)[None, :]
    q_chunk = qi // CHUNK
    k_chunk = kj // CHUNK - N_LEFT_CHUNKS
    in_band = (k_chunk <= q_chunk) & (k_chunk >= q_chunk - N_LEFT_CHUNKS)
    p = win + tq
    n_clipped = pad + tq - 1 - REL_CLIP
    lo = 2 * REL_CLIP - (p - 1 - n_clipped)
    assert lo >= 0 and pad - (win - 1) >= -REL_CLIP
    u = jnp.concatenate([jnp.broadcast_to(rel_bias[:, 2 * REL_CLIP:], (n_heads, n_clipped)),
                         rel_bias[:, lo:2 * REL_CLIP + 1][:, ::-1]], axis=1).astype(F32)
    u_rot = jnp.concatenate([u[:, tq - 1:], u[:, :tq - 1]], axis=1)
    bias = jnp.tile(u_rot, (1, tq))[:, :tq * (p - 1)].reshape(n_heads, tq, p - 1)[:, :, :win]
    return jnp.where(jnp.asarray(in_band)[None], bias, NEG_BIG)


def _out_proj_kernel(a_ref, b_ref, c_ref, w_ref, r_ref, g_ref, o_ref):
    acc, row = None, 0
    for x_ref in (a_ref, b_ref, c_ref):
        part = _dot(x_ref[...], w_ref[row:row + x_ref.shape[1], :])
        acc = part if acc is None else acc + part
        row += x_ref.shape[1]
    o_ref[...] = r_ref[...] + g_ref[...] * acc


def _out_proj(o_a, o_b, o_c, w, l, resid, gate):
    m = o_a.shape[0]
    k, n = w.shape[1], w.shape[2]
    assert o_a.shape[1] + o_b.shape[1] + o_c.shape[1] == k
    tm, tn, tk = _mm_tiles(m, k, n, w.dtype.itemsize)
    assert tk == k
    lhs = lambda x: pl.BlockSpec((tm, x.shape[1]), lambda i, j: (i, 0))
    return pl.pallas_call(
        _out_proj_kernel,
        grid=(m // tm, n // tn),
        in_specs=[lhs(o_a), lhs(o_b), lhs(o_c),
                  pl.BlockSpec((None, k, tn), lambda i, j: (l, 0, j)),
                  pl.BlockSpec((tm, tn), lambda i, j: (i, j)),
                  pl.BlockSpec((1, tn), lambda i, j: (0, j))],
        out_specs=pl.BlockSpec((tm, tn), lambda i, j: (i, j)),
        out_shape=jax.ShapeDtypeStruct((m, n), F32),
        compiler_params=_cparams(("parallel", "parallel"),
                                 _mm_vmem(tm, tn, tk, w.dtype.itemsize) + 3 * tm * tn * 4),
        name="out_proj",
    )(o_a, o_b, o_c, w, resid, gate.reshape(1, n))


def _pack_kernel(x_ref, o_ref, *, runs):
    end = 0
    for src, dst, height in runs:
        if dst > end:
            o_ref[end:dst, :] = jnp.zeros((dst - end, o_ref.shape[1]), o_ref.dtype)
        o_ref[dst:dst + height, :] = x_ref[src:src + height, :].astype(o_ref.dtype)
        end = dst + height
    if end < o_ref.shape[0]:
        o_ref[end:, :] = jnp.zeros((o_ref.shape[0] - end, o_ref.shape[1]), o_ref.dtype)


def _cast_kernel(x_ref, o_ref):
    o_ref[...] = x_ref[...].astype(o_ref.dtype)


def _cast_layer(w, l, dtype):
    _, k, n = w.shape
    tr = _largest_tile(k, V7X_BF16_SUBLANES, 256)
    out = pl.pallas_call(
        _cast_kernel,
        grid=(k // tr,),
        in_specs=[pl.BlockSpec((None, tr, n), lambda i: (l, i, 0))],
        out_specs=pl.BlockSpec((tr, n), lambda i: (i, 0)),
        out_shape=jax.ShapeDtypeStruct((k, n), dtype),
        compiler_params=_cparams(("parallel",), 4 * tr * n * 4),
        name="cast_layer",
    )(w)
    return out.reshape(1, k, n)


def _packed_layout(sizes):
    units, src = [], 0
    for idx, width in enumerate(sizes):
        if width >= V7X_LANES or not units or units[-1][2] >= V7X_LANES:
            units.append([[idx], src, width])
        else:
            units[-1][0].append(idx)
            units[-1][2] += width
        src += width
    srcs = np.concatenate([[0], np.cumsum(sizes)[:-1]])
    offsets, runs, dst = [0] * len(sizes), [], 0
    for members, usrc, uwidth in sorted(units, key=lambda u: -u[2]):
        padded = -(-uwidth // V7X_LANES) * V7X_LANES
        dst = -(-dst // padded) * padded
        for idx in members:
            offsets[idx] = dst + int(srcs[idx]) - usrc
        runs.append((usrc, dst, uwidth))
        dst += padded
    return offsets, tuple(runs), -(-dst // PROJ_N_ALIGN) * PROJ_N_ALIGN


def _pack_w_in(w_t, l, sizes):
    offsets, runs, n_out = _packed_layout(sizes)
    assert all(v % V7X_BF16_SUBLANES == 0 for run in runs for v in run), "segments must be bf16-tile aligned"
    _, n_in, k = w_t.shape
    tc = _largest_tile(k, V7X_LANES, 256)
    packed = pl.pallas_call(
        functools.partial(_pack_kernel, runs=runs),
        grid=(k // tc,),
        in_specs=[pl.BlockSpec((None, n_in, tc), lambda i: (l, 0, i))],
        out_specs=pl.BlockSpec((n_out, tc), lambda i: (0, i)),
        out_shape=jax.ShapeDtypeStruct((n_out, k), BF16),
        compiler_params=_cparams(("parallel",), 3 * n_in * tc * 4 + 2 * n_out * tc * 2),
        name="pack_w_in",
    )(w_t)
    return packed.reshape(1, n_out, k), offsets


def kernel(x, c, w_ada, b_ada, norm_attn_g, w_in, rel_bias, head_norm_g, w_out, norm_ffn_g,
           w_gate_up, w_down, final_norm_g):
    b, s, d = x.shape
    assert b == 1, "single-sequence prefill only"
    depth = w_ada.shape[0]
    n_heads = d // HEAD_DIM
    ha = 3 * n_heads // 8
    hb = 3 * n_heads // 8
    hc = n_heads - ha - hb
    assert ha % N_KV_A == 0 and s % SB_T == 0
    sizes = (ha * HEAD_DIM, N_KV_A * HEAD_DIM, N_KV_A * HEAD_DIM, IDX_HEADS * IDX_DIM, IDX_DIM,
             IDX_HEADS, hb * HEAD_DIM, hb * HEAD_DIM, hb * HEAD_DIM,
             hc * HEAD_DIM, hc * HEAD_DIM, hc * HEAD_DIM)
    assert w_in.shape[2] == sum(sizes)
    topk = min(IDX_TOPK_MAX, s // 4)
    scale = HEAD_DIM ** -0.5
    pad = N_LEFT_CHUNKS * CHUNK

    rope128 = _rope_tables(s, HEAD_DIM, 1)
    rope64 = _rope_tables(s, IDX_DIM, V7X_LANES // IDX_DIM)

    xs = x.reshape(s, d)
    mod = _adaln(c, w_ada, b_ada)
    w_in_t = jnp.swapaxes(w_in, 1, 2)

    for l in range(depth):
        sh1, sc1, g1, sh2, sc2, g2 = [mod[l, :, n * d:(n + 1) * d] for n in range(6)]

        h = _norm(xs, norm_attn_g[l], sc1, sh1)
        w_packed, offs = _pack_w_in(w_in_t, l, sizes)
        o_qa, o_ka, o_va, o_iq, o_ik, o_iw, o_qb, o_kb, o_vb, o_qc, o_kc, o_vc = offs
        assert o_iw == o_ik + IDX_DIM and o_ik % V7X_LANES == 0
        proj = _matmul_nt(h, w_packed, 0, F32)

        qa = _prep_heads(proj, o_qa, ha, rope128, HEAD_DIM, scale * math.log2(math.e))
        ka = _prep_heads(proj, o_ka, N_KV_A, rope128, HEAD_DIM)
        va = _prep_heads(proj, o_va, N_KV_A)
        vat = va.reshape(N_KV_A, s // DSA_TK, DSA_TK, HEAD_DIM).swapaxes(2, 3)
        iq = _prep_heads(proj, o_iq, IDX_HEADS * IDX_DIM // V7X_LANES, rope64, IDX_DIM,
                         head_major=False)
        ik2, iw = _prep_ikw(proj, o_ik, rope64)
        gain = head_norm_g[l]
        gain_t = jnp.broadcast_to(gain.reshape(n_heads, HEAD_DIM, 1), (n_heads, HEAD_DIM, V7X_LANES))
        o_a = _dsa(iq, iw[:, :IDX_HEADS].T, ik2, qa, ka, vat, gain_t[:ha], topk)

        qb = _prep_heads(proj, o_qb, hb, scale=scale)
        kb = _prep_heads(proj, o_kb, hb)
        vb = _prep_heads(proj, o_vb, hb)
        nb, sub = s // SB_T, V7X_SUBLANES
        kb = kb.reshape(hb, nb, sub, SB_T // sub, HEAD_DIM).swapaxes(2, 3).reshape(hb, nb, SB_T, HEAD_DIM)
        vbt = vb.reshape(hb, nb, sub, SB_T // sub, HEAD_DIM).swapaxes(2, 3).reshape(hb, nb, SB_T, HEAD_DIM)
        vbt = vbt.swapaxes(2, 3)
        o_b = _sb(qb, kb, vbt, gain_t[ha:ha + hb])

        qc = _prep_heads(proj, o_qc, hc, scale=scale)
        kc = _prep_heads(proj, o_kc, hc)
        vc = _prep_heads(proj, o_vc, hc)
        kc = jnp.pad(kc, ((0, 0), (pad, 0), (0, 0)))
        vc = jnp.pad(vc, ((0, 0), (pad, 0), (0, 0)))
        o_c = _cb(qc, kc, vc, _cb_bias(rel_bias[l]), gain[(ha + hb) * HEAD_DIM:], pad)

        xs = _out_proj(o_a, o_b, o_c, _cast_layer(w_out, l, BF16), 0, xs, g1)

        h = _norm(xs, norm_ffn_g[l], sc2, sh2)
        act = _matmul_swiglu(h, w_gate_up, l)
        xs = _matmul_resid(act, _cast_layer(w_down, l, BF16), 0, xs, g2)

    out = _norm(xs, final_norm_g, out_dtype=x.dtype)
    return out.reshape(b, s, d)
```

```python
import functools
import math

import numpy as np
import jax
import jax.numpy as jnp
from jax import lax
from jax.experimental import pallas as pl
from jax.experimental.pallas import tpu as pltpu

F32 = jnp.float32
BF16 = jnp.bfloat16
I32 = jnp.int32

HEAD_DIM = 128
N_KV_A = 2
IDX_HEADS = 16
IDX_DIM = 64
IDX_TOPK_MAX = 256
CHUNK = 64
N_LEFT_CHUNKS = 8
REL_CLIP = 256
ROPE_THETA = 10000.0
EPS = 1e-6

V7X_LANES = 128
V7X_SUBLANES = 8
V7X_BF16_SUBLANES = 16
V7X_VMEM_BYTES = 64 * 1024 * 1024
V7X_MXU_COLS = 256

PROJ_N_ALIGN = 2 * V7X_MXU_COLS

NEG_BIG = -1e30
INT_MIN = -(2 ** 31)

DSA_TQ = 256
DSA_TK = 512
SB_T = 256
SB_HEADS = 4
CB_TQ = 256
CB_HEADS = 4


def _cparams(semantics, vmem_bytes):
    limit = int(min(V7X_VMEM_BYTES * 7 // 8, max(vmem_bytes, 16 * 1024 * 1024)))
    return pltpu.CompilerParams(dimension_semantics=semantics, vmem_limit_bytes=limit)


def _largest_tile(n, unit, cap):
    best = None
    t = unit
    while t <= min(n, cap):
        if n % t == 0:
            best = t
        t += unit
    assert best is not None, (n, unit, cap)
    return best


def _adaln_kernel(c_ref, w_ref, b_ref, o_ref, acc_ref):
    k = pl.program_id(1)
    c = c_ref[...]
    cond = c * jax.nn.sigmoid(c)
    tr, n = w_ref.shape

    @pl.when(k == 0)
    def _():
        acc_ref[...] = jnp.zeros_like(acc_ref)

    for j in range(n // V7X_LANES):
        cols = slice(j * V7X_LANES, (j + 1) * V7X_LANES)
        prod = w_ref[:, cols] * cond
        part = prod[0:V7X_SUBLANES]
        for g in range(1, tr // V7X_SUBLANES):
            part = part + prod[g * V7X_SUBLANES:(g + 1) * V7X_SUBLANES]
        acc_ref[:, cols] += part

    @pl.when(k == pl.num_programs(1) - 1)
    def _():
        o_ref[...] = jnp.sum(acc_ref[...], axis=0, keepdims=True) + b_ref[...]


def _adaln(c, w_ada, b_ada):
    depth, d, n = w_ada.shape
    tr = _largest_tile(d, V7X_SUBLANES, 128)
    c_b = jnp.broadcast_to(c.reshape(d, 1), (d, V7X_LANES))
    out = pl.pallas_call(
        _adaln_kernel,
        grid=(depth, d // tr),
        in_specs=[
            pl.BlockSpec((tr, V7X_LANES), lambda l, k: (k, 0)),
            pl.BlockSpec((None, tr, n), lambda l, k: (l, k, 0)),
            pl.BlockSpec((None, 1, n), lambda l, k: (l, 0, 0)),
        ],
        out_specs=pl.BlockSpec((None, 1, n), lambda l, k: (l, 0, 0)),
        out_shape=jax.ShapeDtypeStruct((depth, 1, n), F32),
        scratch_shapes=[pltpu.VMEM((V7X_SUBLANES, n), F32)],
        compiler_params=_cparams(("parallel", "arbitrary"), 3 * tr * n * 4),
        name="adaln",
    )(c_b, w_ada, b_ada.reshape(depth, 1, n))
    return out


def _norm_kernel(x_ref, g_ref, *rest, modulate):
    o_ref = rest[-1]
    x = x_ref[...]
    ms = jnp.mean(x * x, axis=-1, keepdims=True)
    y = x * lax.rsqrt(ms + EPS) * g_ref[...]
    if modulate:
        sc_ref, sh_ref = rest[0], rest[1]
        y = y * (1.0 + sc_ref[...]) + sh_ref[...]
    o_ref[...] = y.astype(o_ref.dtype)


def _norm(x, g, scale=None, shift=None, out_dtype=BF16):
    s, d = x.shape
    ts = _largest_tile(s, V7X_SUBLANES, 512)
    modulate = scale is not None
    row = pl.BlockSpec((1, d), lambda i: (0, 0))
    args = [x, g.reshape(1, d)]
    specs = [pl.BlockSpec((ts, d), lambda i: (i, 0)), row]
    if modulate:
        args += [scale.reshape(1, d), shift.reshape(1, d)]
        specs += [row, row]
    return pl.pallas_call(
        functools.partial(_norm_kernel, modulate=modulate),
        grid=(s // ts,),
        in_specs=specs,
        out_specs=pl.BlockSpec((ts, d), lambda i: (i, 0)),
        out_shape=jax.ShapeDtypeStruct((s, d), out_dtype),
        compiler_params=_cparams(("parallel",), 6 * ts * d * 4),
        name="rmsnorm",
    )(*args)


def _dot(a, w):
    return jnp.dot(a, w.astype(BF16), preferred_element_type=F32)


def _mm_plain_kernel(a_ref, wt_ref, o_ref):
    o_ref[...] = lax.dot_general(a_ref[...], wt_ref[...].astype(BF16), (((1,), (1,)), ((), ())),
                                 preferred_element_type=F32).astype(o_ref.dtype)


def _mm_swiglu_kernel(a_ref, wg_ref, wu_ref, o_ref):
    a = a_ref[...]
    gate = _dot(a, wg_ref[...])
    up = _dot(a, wu_ref[...])
    o_ref[...] = (gate * jax.nn.sigmoid(gate) * up).astype(o_ref.dtype)


def _mm_resid_kernel(a_ref, w_ref, r_ref, g_ref, o_ref, *acc, nk):
    if nk == 1:
        o_ref[...] = r_ref[...] + g_ref[...] * _dot(a_ref[...], w_ref[...])
        return
    acc_ref, = acc
    k = pl.program_id(2)

    @pl.when(k == 0)
    def _():
        acc_ref[...] = jnp.zeros_like(acc_ref)

    acc_ref[...] += _dot(a_ref[...], w_ref[...])

    @pl.when(k == nk - 1)
    def _():
        o_ref[...] = r_ref[...] + g_ref[...] * acc_ref[...]


def _mm_tiles(m, k, n, w_itemsize):
    tn = _largest_tile(n, V7X_LANES, 768 if w_itemsize == 2 else 256)
    if k <= 4096:
        return _largest_tile(m, V7X_SUBLANES, 1024), tn, k
    if w_itemsize == 2:
        return _largest_tile(m, V7X_SUBLANES, 512), _largest_tile(n, V7X_LANES, 512), k
    return _largest_tile(m, V7X_SUBLANES, 1024), tn, _largest_tile(k, V7X_LANES, 5632)


def _mm_vmem(tm, tn, tk, w_itemsize, n_w=1):
    return (2 * tm * tk * 2 + n_w * (2 * tk * tn * w_itemsize + tk * tn * 2)
            + (4 + 2 * n_w) * tm * tn * 4)


def _matmul_nt(a, w_t, l, out_dtype):
    m, k = a.shape
    n = w_t.shape[1]
    tm, tn, tk = _mm_tiles(m, k, n, w_t.dtype.itemsize)
    assert tk == k
    return pl.pallas_call(
        _mm_plain_kernel,
        grid=(m // tm, n // tn),
        in_specs=[pl.BlockSpec((tm, k), lambda i, j: (i, 0)),
                  pl.BlockSpec((None, tn, k), lambda i, j: (l, j, 0))],
        out_specs=pl.BlockSpec((tm, tn), lambda i, j: (i, j)),
        out_shape=jax.ShapeDtypeStruct((m, n), out_dtype),
        compiler_params=_cparams(("parallel", "parallel"), _mm_vmem(tm, tn, tk, w_t.dtype.itemsize)),
        name="matmul",
    )(a, w_t)


def _matmul_swiglu(a, w_gate_up, l):
    m, k = a.shape
    n = w_gate_up.shape[2] // 2
    tm, tn, tk = _mm_tiles(m, k, n, w_gate_up.dtype.itemsize)
    assert tk == k
    nb = n // tn
    return pl.pallas_call(
        _mm_swiglu_kernel,
        grid=(m // tm, nb),
        in_specs=[pl.BlockSpec((tm, k), lambda i, j: (i, 0)),
                  pl.BlockSpec((None, k, tn), lambda i, j: (l, 0, j)),
                  pl.BlockSpec((None, k, tn), lambda i, j: (l, 0, j + nb))],
        out_specs=pl.BlockSpec((tm, tn), lambda i, j: (i, j)),
        out_shape=jax.ShapeDtypeStruct((m, n), BF16),
        compiler_params=_cparams(("parallel", "parallel"),
                                 _mm_vmem(tm, tn, tk, w_gate_up.dtype.itemsize, n_w=2)),
        name="matmul_swiglu",
    )(a, w_gate_up, w_gate_up)


def _matmul_resid(a, w, l, resid, gate):
    m, k = a.shape
    n = w.shape[2]
    tm, tn, tk = _mm_tiles(m, k, n, w.dtype.itemsize)
    nk = k // tk
    scratch = [pltpu.VMEM((tm, tn), F32)] if nk > 1 else []
    return pl.pallas_call(
        functools.partial(_mm_resid_kernel, nk=nk),
        grid=(m // tm, n // tn, nk),
        in_specs=[pl.BlockSpec((tm, tk), lambda i, j, kk: (i, kk)),
                  pl.BlockSpec((None, tk, tn), lambda i, j, kk: (l, kk, j)),
                  pl.BlockSpec((tm, tn), lambda i, j, kk: (i, j)),
                  pl.BlockSpec((1, tn), lambda i, j, kk: (0, j))],
        out_specs=pl.BlockSpec((tm, tn), lambda i, j, kk: (i, j)),
        out_shape=jax.ShapeDtypeStruct((m, n), F32),
        scratch_shapes=scratch,
        compiler_params=_cparams(("parallel", "parallel", "arbitrary"),
                                 _mm_vmem(tm, tn, tk, w.dtype.itemsize) + 3 * tm * tn * 4),
        name="matmul_resid",
    )(a, w, resid, gate.reshape(1, n))


def _rope_tables(s, d, lane_groups):
    pos = jnp.arange(s, dtype=jnp.int32).astype(F32)
    inv = ROPE_THETA ** (-jnp.arange(0, d, 2, dtype=F32) / d)
    ang = pos[:, None] * inv[None, :]
    cos, sin = jnp.cos(ang), jnp.sin(ang)
    cos_t = jnp.tile(jnp.concatenate([cos, cos], axis=1), (1, lane_groups))
    sin_t = jnp.tile(jnp.concatenate([-sin, sin], axis=1), (1, lane_groups))
    return cos_t, sin_t


def _rot_half(x, d):
    if d == V7X_LANES:
        return pltpu.roll(x, d // 2, 1)
    lane = lax.broadcasted_iota(I32, x.shape, 1)
    first = (lane & (d - 1)) < d // 2
    return jnp.where(first, pltpu.roll(x, V7X_LANES - d // 2, 1), pltpu.roll(x, d // 2, 1))


def _prep_kernel(x_ref, *rest, rope_dim, scale, head_major):
    o_ref = rest[-1]
    if rope_dim:
        cos, sin = rest[0][...], rest[1][...]
    for h in range(x_ref.shape[1] // V7X_LANES):
        cols = slice(h * V7X_LANES, (h + 1) * V7X_LANES)
        x = x_ref[:, cols]
        if rope_dim:
            x = x * cos + _rot_half(x, rope_dim) * sin
        if scale != 1.0:
            x = x * scale
        if head_major:
            o_ref[h] = x.astype(o_ref.dtype)
        else:
            o_ref[:, cols] = x.astype(o_ref.dtype)


def _prep_heads(proj, col0, n_blocks, tables=None, rope_dim=0, scale=1.0, head_major=True):
    s = proj.shape[0]
    width = n_blocks * V7X_LANES
    assert col0 % width == 0 or n_blocks == 1, "segment must start at a multiple of its width"
    ts = _largest_tile(s, V7X_SUBLANES, 512)
    args = [proj]
    specs = [pl.BlockSpec((ts, width), lambda i: (i, col0 // width))]
    if rope_dim:
        args += list(tables)
        specs += [pl.BlockSpec((ts, V7X_LANES), lambda i: (i, 0))] * 2
    if head_major:
        out_spec = pl.BlockSpec((n_blocks, ts, V7X_LANES), lambda i: (0, i, 0))
        out_shape = jax.ShapeDtypeStruct((n_blocks, s, V7X_LANES), BF16)
    else:
        out_spec = pl.BlockSpec((ts, width), lambda i: (i, 0))
        out_shape = jax.ShapeDtypeStruct((s, width), BF16)
    return pl.pallas_call(
        functools.partial(_prep_kernel, rope_dim=rope_dim, scale=scale, head_major=head_major),
        grid=(s // ts,),
        in_specs=specs,
        out_specs=out_spec,
        out_shape=out_shape,
        compiler_params=_cparams(("parallel",), 8 * ts * width * 4),
        name="prep_heads",
    )(*args)


def _ikw_kernel(x_ref, cos_ref, sin_ref, ik_ref, iw_ref):
    x = x_ref[...]
    y = x * cos_ref[...] + _rot_half(x, IDX_DIM) * sin_ref[...]
    lane = lax.broadcasted_iota(I32, x.shape, 1)
    ik_ref[...] = jnp.where(lane < IDX_DIM, y, pltpu.roll(y, IDX_DIM, 1)).astype(ik_ref.dtype)
    iw_ref[...] = pltpu.roll(x, V7X_LANES - IDX_DIM, 1)


def _prep_ikw(proj, col0, tables):
    s = proj.shape[0]
    ts = _largest_tile(s, V7X_SUBLANES, 1024)
    cb = col0 // V7X_LANES
    blk = pl.BlockSpec((ts, V7X_LANES), lambda i: (i, 0))
    return pl.pallas_call(
        _ikw_kernel,
        grid=(s // ts,),
        in_specs=[pl.BlockSpec((ts, V7X_LANES), lambda i: (i, cb)), blk, blk],
        out_specs=[blk, blk],
        out_shape=[jax.ShapeDtypeStruct((s, V7X_LANES), BF16),
                   jax.ShapeDtypeStruct((s, V7X_LANES), F32)],
        compiler_params=_cparams(("parallel",), 16 * ts * V7X_LANES * 4),
        name="prep_ikw",
    )(proj, *tables)


def _head_rms_t(x_t, g_t):
    ms = jnp.mean(x_t * x_t, axis=0, keepdims=True)
    y = x_t * lax.rsqrt(ms + EPS)
    reps = x_t.shape[1] // g_t.shape[1]
    return y * (g_t if reps == 1 else jnp.concatenate([g_t] * reps, axis=1))


def _dsa_kernel(iq_ref, iwt_ref, ik_ref, q_ref, k_ref, vt_ref, gt_ref, o_ref,
                rhs_scr, key_scr, m_scr, l_scr, acc_scr, s_scr, ma_scr,
                *, topk, rep, idx_scale, n_total):
    tq, tk = DSA_TQ, DSA_TK
    nsub = tk // V7X_SUBLANES
    r_all = rep * tq
    i = pl.program_id(0)
    start = i * tq
    n_kt = (start + tq + tk - 1) >> int(math.log2(tk))

    lane = lax.broadcasted_iota(I32, (tq, V7X_LANES), 1)
    for p in range(IDX_HEADS // 2):
        pair = iq_ref[:, p * V7X_LANES:(p + 1) * V7X_LANES]
        zero = jnp.zeros_like(pair)
        rhs_scr[p, :tq] = jnp.where(lane < IDX_DIM, pair, zero)
        rhs_scr[p, tq:] = jnp.where(lane >= IDX_DIM, pair, zero)

    iwt = iwt_ref[...]
    key_row = lax.broadcasted_iota(I32, (tk, tq), 0)
    q_lane = lax.broadcasted_iota(I32, (tk, tq), 1)
    key_lim = start + ((q_lane >> int(math.log2(CHUNK))) + 1) * CHUNK

    def score_body(kt, carry):
        off = pl.multiple_of(kt * tk, tk)
        ik_t = ik_ref[pl.ds(off, tk), :]
        acc = jnp.zeros((tk, tq), F32)
        for p in range(IDX_HEADS // 2):
            d = lax.dot_general(ik_t, rhs_scr[p], (((1,), (1,)), ((), ())),
                                preferred_element_type=F32)
            for half in range(2):
                h = 2 * p + half
                acc = acc + iwt[h:h + 1, :] * jnp.maximum(d[:, half * tq:(half + 1) * tq], 0.0)
        bits = pltpu.bitcast(acc * idx_scale, I32)
        key = jnp.where(bits < 0, bits ^ 0x7FFFFFFF, bits)
        key = jnp.where(key_row + off < key_lim, key, INT_MIN)
        key_scr[kt] = key
        hit = jnp.where(key >= 0, 1, 0)
        for c in range(nsub):
            carry = carry + hit[c * V7X_SUBLANES:(c + 1) * V7X_SUBLANES]
        return carry

    pos8 = lax.fori_loop(0, n_kt, score_body, jnp.zeros((V7X_SUBLANES, tq), I32))
    n_pos = jnp.sum(pos8, axis=0, keepdims=True)

    n_part = 4

    def count_if(pred):
        def body(kt, parts):
            hit = jnp.where(pred(key_scr[kt], kt), 1, 0)
            parts = list(parts)
            for c in range(nsub):
                parts[c % n_part] = parts[c % n_part] + hit[c * V7X_SUBLANES:(c + 1) * V7X_SUBLANES]
            return tuple(parts)

        parts = lax.fori_loop(0, n_kt, body,
                              tuple(jnp.zeros((V7X_SUBLANES, tq), I32) for _ in range(n_part)))
        return jnp.sum(sum(parts[1:], parts[0]), axis=0, keepdims=True)

    zero = jnp.zeros((1, tq), I32)
    thr = jnp.where(n_pos >= topk, zero, zero + INT_MIN)

    def bisect(b, state):
        thr, n_ge = state
        cand = thr + lax.shift_left(jnp.int32(1), 30 - b)
        n_cand = count_if(lambda keys, kt: keys >= cand)
        ok = n_cand >= topk
        return jnp.where(ok, cand, thr), jnp.where(ok, n_cand, n_ge)

    thr, n_ge = lax.fori_loop(0, 31, bisect, (thr, n_pos))

    tied = jnp.logical_and(n_ge > topk, thr > INT_MIN)

    @pl.when(jnp.max(jnp.where(tied, 1, 0)) > 0)
    def _():
        n_idx_bits = max(1, (n_total - 1).bit_length())
        need = topk - count_if(lambda keys, kt: keys > thr)

        def n_tied_before(limit):
            return count_if(lambda keys, kt: jnp.logical_and(keys == thr, key_row + kt * tk < limit))

        def idx_bisect(b, last):
            cand = last + lax.shift_left(jnp.int32(1), n_idx_bits - 1 - b)
            return jnp.where(n_tied_before(cand) < need, cand, last)

        last = lax.fori_loop(0, n_idx_bits, idx_bisect, zero)

        def demote(kt, carry):
            keys = key_scr[kt]
            drop = jnp.logical_and(jnp.logical_and(tied, keys == thr), key_row + kt * tk > last)
            key_scr[kt] = jnp.where(drop, keys - 1, keys)
            return carry

        lax.fori_loop(0, n_kt, demote, 0)

    thr = jnp.maximum(thr, INT_MIN + 1)

    m_scr[...] = jnp.full(m_scr.shape, NEG_BIG, F32)
    l_scr[...] = jnp.zeros(l_scr.shape, F32)
    acc_scr[...] = jnp.zeros(acc_scr.shape, F32)

    def stage_a(kt):
        slot = kt & 1
        off = pl.multiple_of(kt * tk, tk)
        bias = jnp.where(key_scr[kt] >= thr, 0.0, NEG_BIG)
        bias = jnp.concatenate([bias] * rep, axis=1)
        for g in range(N_KV_A):
            qg = q_ref[g * rep:(g + 1) * rep].reshape(r_all, HEAD_DIM)
            s = lax.dot_general(k_ref[g, pl.ds(off, tk), :], qg, (((1,), (1,)), ((), ())),
                                preferred_element_type=F32) + bias
            m_old = m_scr[g]
            m_new = jnp.maximum(m_old, jnp.max(s, axis=0, keepdims=True))
            s_scr[slot, g] = s
            ma_scr[slot, g, 0:1] = m_new
            ma_scr[slot, g, 1:2] = jnp.exp2(m_old - m_new)
            m_scr[g] = m_new

    def stage_b(kt):
        slot = kt & 1
        for g in range(N_KV_A):
            m_new = ma_scr[slot, g, 0:1]
            alpha = ma_scr[slot, g, 1:2]
            p = jnp.exp2(s_scr[slot, g] - m_new)
            l_scr[g] = alpha * l_scr[g] + jnp.sum(p, axis=0, keepdims=True)
            acc_scr[g] = alpha * acc_scr[g] + jnp.dot(vt_ref[g, kt], p.astype(BF16),
                                                      preferred_element_type=F32)

    stage_a(0)

    def att_body(kt, carry):
        stage_b(kt - 1)
        stage_a(kt)
        return carry

    lax.fori_loop(1, n_kt, att_body, 0)
    stage_b(n_kt - 1)
    for g in range(N_KV_A):
        out_t = acc_scr[g] / l_scr[g]
        for r in range(rep):
            hh = g * rep + r
            head = _head_rms_t(out_t[:, r * tq:(r + 1) * tq], gt_ref[hh])
            o_ref[:, hh * HEAD_DIM:(hh + 1) * HEAD_DIM] = head.T.astype(o_ref.dtype)


def _dsa(iq, iwt, ik2, qa, ka, vat, g_t, topk):
    n_heads, s, _ = qa.shape
    rep = n_heads // N_KV_A
    tq, tk = DSA_TQ, DSA_TK
    assert s % tk == 0
    idx_scale = (IDX_HEADS ** -0.5) * (IDX_DIM ** -0.5)
    vmem = (4 * N_KV_A * s * HEAD_DIM * 2 * 2 + 2 * s * V7X_LANES * 2 + tq * s * 4
            + 32 * rep * tq * tk * 4)
    return pl.pallas_call(
        functools.partial(_dsa_kernel, topk=topk, rep=rep, idx_scale=idx_scale, n_total=s),
        grid=(s // tq,),
        in_specs=[
            pl.BlockSpec((tq, IDX_HEADS * IDX_DIM), lambda i: (i, 0)),
            pl.BlockSpec((IDX_HEADS, tq), lambda i: (0, i)),
            pl.BlockSpec((s, V7X_LANES), lambda i: (0, 0)),
            pl.BlockSpec((n_heads, tq, HEAD_DIM), lambda i: (0, i, 0)),
            pl.BlockSpec((N_KV_A, s, HEAD_DIM), lambda i: (0, 0, 0)),
            pl.BlockSpec((N_KV_A, s // tk, HEAD_DIM, tk), lambda i: (0, 0, 0, 0)),
            pl.BlockSpec((n_heads, HEAD_DIM, V7X_LANES), lambda i: (0, 0, 0)),
        ],
        out_specs=pl.BlockSpec((tq, n_heads * HEAD_DIM), lambda i: (i, 0)),
        out_shape=jax.ShapeDtypeStruct((s, n_heads * HEAD_DIM), BF16),
        scratch_shapes=[
            pltpu.VMEM((IDX_HEADS // 2, 2 * tq, V7X_LANES), BF16),
            pltpu.VMEM((s // tk, tk, tq), I32),
            pltpu.VMEM((N_KV_A, 1, rep * tq), F32),
            pltpu.VMEM((N_KV_A, 1, rep * tq), F32),
            pltpu.VMEM((N_KV_A, HEAD_DIM, rep * tq), F32),
            pltpu.VMEM((2, N_KV_A, tk, rep * tq), F32),
            pltpu.VMEM((2, N_KV_A, 2, rep * tq), F32),
        ],
        compiler_params=_cparams(("parallel",), vmem),
        name="dsa",
    )(iq, iwt, ik2, qa, ka, vat, g_t)


def _sb_kernel(q_ref, k_ref, vt_ref, gt_ref, o_ref, acc_ref):
    t = SB_T
    nrg = t // V7X_SUBLANES
    n_h = q_ref.shape[0]
    i = pl.program_id(1)
    acc_ref[...] = jnp.zeros(acc_ref.shape, F32)

    def local_scan(hh, j, diagonal):
        z = lax.dot_general(k_ref[hh, j], q_ref[hh], (((1,), (1,)), ((), ())),
                            preferred_element_type=F32)
        e = jnp.exp(-jnp.abs(z))
        r = 1.0 / (1.0 + e)
        er = e * r
        pos = z >= 0.0
        beta = jnp.where(pos, r, er)
        keep = jnp.where(pos, er, r)
        if diagonal:
            rr = lax.broadcasted_iota(I32, (t, t), 0)
            tt = lax.broadcasted_iota(I32, (t, t), 1)
            key_pos = (rr & (V7X_SUBLANES - 1)) * nrg + (rr >> int(math.log2(V7X_SUBLANES)))
            causal = key_pos < tt
            beta = jnp.where(causal, beta, 0.0)
            keep = jnp.where(causal, keep, 1.0)
        run8 = jnp.ones((V7X_SUBLANES, t), F32)
        a_loc = [None] * nrg
        for g in range(nrg - 1, -1, -1):
            rows = slice(g * V7X_SUBLANES, (g + 1) * V7X_SUBLANES)
            a_loc[g] = beta[rows] * run8
            run8 = run8 * keep[rows]
        return a_loc, run8

    def finish(hh, j, scan, carry):
        a_loc, run8 = scan
        run = carry
        offs = [None] * V7X_SUBLANES
        for u in range(V7X_SUBLANES - 1, -1, -1):
            offs[u] = run
            run = run * run8[u:u + 1]
        off8 = jnp.concatenate(offs, axis=0)
        a = jnp.concatenate([al * off8 for al in a_loc], axis=0)
        acc_ref[hh] += jnp.dot(vt_ref[hh, j], a.astype(BF16), preferred_element_type=F32)
        return run

    j_prev = jnp.maximum(i - 1, 0)
    has_prev = (i > 0).astype(F32)
    carries = []
    for hh in range(n_h):
        scan_diag = local_scan(hh, i, True)
        scan_prev = local_scan(hh, j_prev, False)
        carry = finish(hh, i, scan_diag, jnp.ones((1, t), F32)) * has_prev
        carries.append(finish(hh, j_prev, scan_prev, carry))

    def cond(state):
        j, cs = state
        alive = jnp.max(functools.reduce(jnp.maximum, cs)) > 0.0
        return jnp.logical_and(j >= 0, alive)

    def body(state):
        j, cs = state
        return j - 1, tuple(finish(hh, j, local_scan(hh, j, False), cs[hh]) for hh in range(n_h))

    lax.while_loop(cond, body, (i - 2, tuple(carries)))
    for hh in range(n_h):
        head = _head_rms_t(acc_ref[hh], gt_ref[hh])
        o_ref[:, hh * HEAD_DIM:(hh + 1) * HEAD_DIM] = head.T.astype(o_ref.dtype)


def _sb(q, k_perm, v_t, g_t):
    n_heads, s, _ = q.shape
    t = SB_T
    nb = s // t
    n_h = SB_HEADS if n_heads % SB_HEADS == 0 else 1
    return pl.pallas_call(
        _sb_kernel,
        grid=(n_heads // n_h, nb),
        in_specs=[
            pl.BlockSpec((n_h, t, HEAD_DIM), lambda h, i: (h, i, 0)),
            pl.BlockSpec((n_h, nb, t, HEAD_DIM), lambda h, i: (h, 0, 0, 0)),
            pl.BlockSpec((n_h, nb, HEAD_DIM, t), lambda h, i: (h, 0, 0, 0)),
            pl.BlockSpec((n_h, HEAD_DIM, V7X_LANES), lambda h, i: (h, 0, 0)),
        ],
        out_specs=pl.BlockSpec((t, n_h * HEAD_DIM), lambda h, i: (i, h)),
        out_shape=jax.ShapeDtypeStruct((s, n_heads * HEAD_DIM), BF16),
        scratch_shapes=[pltpu.VMEM((n_h, HEAD_DIM, t), F32)],
        compiler_params=_cparams(("parallel", "parallel"),
                                 n_h * (8 * s * HEAD_DIM * 2 + 40 * t * t * 4)),
        name="stick_breaking",
    )(q, k_perm, v_t, g_t)


def _cb_kernel(q_ref, k_ref, v_ref, b_ref, g_ref, o_ref, *, win, pad):
    tq = CB_TQ
    i = pl.program_id(1)
    off = pl.multiple_of(i * tq, tq)
    col = lax.broadcasted_iota(I32, (tq, win), 1)
    in_seq = col + off >= pad
    for hh in range(q_ref.shape[0]):
        k_w = k_ref[hh, pl.ds(off, win), :]
        v_w = v_ref[hh, pl.ds(off, win), :]
        s = lax.dot_general(q_ref[hh], k_w, (((1,), (1,)), ((), ())), preferred_element_type=F32)
        s = jnp.where(in_seq, s + b_ref[hh], NEG_BIG)
        m = jnp.max(s, axis=-1, keepdims=True)
        p = jnp.exp(s - m)
        l = jnp.sum(p, axis=-1, keepdims=True)
        o = jnp.dot(p.astype(BF16), v_w, preferred_element_type=F32) / l
        cols = slice(hh * HEAD_DIM, (hh + 1) * HEAD_DIM)
        ms = jnp.mean(o * o, axis=-1, keepdims=True)
        o_ref[:, cols] = (o * lax.rsqrt(ms + EPS) * g_ref[:, cols]).astype(o_ref.dtype)


def _cb(q, k_pad, v_pad, bias, gain, pad):
    n_heads, s, _ = q.shape
    tq = CB_TQ
    win = pad + tq
    sp = k_pad.shape[1]
    n_h = CB_HEADS if n_heads % CB_HEADS == 0 else 1
    return pl.pallas_call(
        functools.partial(_cb_kernel, win=win, pad=pad),
        grid=(n_heads // n_h, s // tq),
        in_specs=[
            pl.BlockSpec((n_h, tq, HEAD_DIM), lambda h, i: (h, i, 0)),
            pl.BlockSpec((n_h, sp, HEAD_DIM), lambda h, i: (h, 0, 0)),
            pl.BlockSpec((n_h, sp, HEAD_DIM), lambda h, i: (h, 0, 0)),
            pl.BlockSpec((n_h, tq, win), lambda h, i: (h, 0, 0)),
            pl.BlockSpec((1, n_h * HEAD_DIM), lambda h, i: (0, h)),
        ],
        out_specs=pl.BlockSpec((tq, n_h * HEAD_DIM), lambda h, i: (i, h)),
        out_shape=jax.ShapeDtypeStruct((s, n_heads * HEAD_DIM), BF16),
        compiler_params=_cparams(("parallel", "parallel"),
                                 n_h * (8 * sp * HEAD_DIM * 2 + 16 * tq * win * 4)),
        name="chunk_band",
    )(q, k_pad, v_pad, bias, gain.reshape(1, n_heads * HEAD_DIM))


def _cb_bias(rel_bias):
    tq = CB_TQ
    pad = N_LEFT_CHUNKS * CHUNK
    win = pad + tq
    n_heads = rel_bias.shape[0]
    qi = np.arange(tq)[:, None]
    kj = np.arange(win)[None, :]
    q_chunk = qi // CHUNK
    k_chunk = kj // CHUNK - N_LEFT_CHUNKS
    in_band = (k_chunk <= q_chunk) & (k_chunk >= q_chunk - N_LEFT_CHUNKS)
    p = win + tq
    n_clipped = pad + tq - 1 - REL_CLIP
    lo = 2 * REL_CLIP - (p - 1 - n_clipped)
    assert lo >= 0 and pad - (win - 1) >= -REL_CLIP
    u = jnp.concatenate([jnp.broadcast_to(rel_bias[:, 2 * REL_CLIP:], (n_heads, n_clipped)),
                         rel_bias[:, lo:2 * REL_CLIP + 1][:, ::-1]], axis=1).astype(F32)
    u_rot = jnp.concatenate([u[:, tq - 1:], u[:, :tq - 1]], axis=1)
    bias = jnp.tile(u_rot, (1, tq))[:, :tq * (p - 1)].reshape(n_heads, tq, p - 1)[:, :, :win]
    return jnp.where(jnp.asarray(in_band)[None], bias, NEG_BIG)


def _out_proj_kernel(a_ref, b_ref, c_ref, w_ref, r_ref, g_ref, o_ref):
    acc, row = None, 0
    for x_ref in (a_ref, b_ref, c_ref):
        part = _dot(x_ref[...], w_ref[row:row + x_ref.shape[1], :])
        acc = part if acc is None else acc + part
        row += x_ref.shape[1]
    o_ref[...] = r_ref[...] + g_ref[...] * acc


def _out_proj(o_a, o_b, o_c, w, l, resid, gate):
    m = o_a.shape[0]
    k, n = w.shape[1], w.shape[2]
    assert o_a.shape[1] + o_b.shape[1] + o_c.shape[1] == k
    tm, tn, tk = _mm_tiles(m, k, n, w.dtype.itemsize)
    assert tk == k
    lhs = lambda x: pl.BlockSpec((tm, x.shape[1]), lambda i, j: (i, 0))
    return pl.pallas_call(
        _out_proj_kernel,
        grid=(m // tm, n // tn),
        in_specs=[lhs(o_a), lhs(o_b), lhs(o_c),
                  pl.BlockSpec((None, k, tn), lambda i, j: (l, 0, j)),
                  pl.BlockSpec((tm, tn), lambda i, j: (i, j)),
                  pl.BlockSpec((1, tn), lambda i, j: (0, j))],
        out_specs=pl.BlockSpec((tm, tn), lambda i, j: (i, j)),
        out_shape=jax.ShapeDtypeStruct((m, n), F32),
        compiler_params=_cparams(("parallel", "parallel"),
                                 _mm_vmem(tm, tn, tk, w.dtype.itemsize) + 3 * tm * tn * 4),
        name="out_proj",
    )(o_a, o_b, o_c, w, resid, gate.reshape(1, n))


def _pack_kernel(x_ref, o_ref, *, runs):
    end = 0
    for src, dst, height in runs:
        if dst > end:
            o_ref[end:dst, :] = jnp.zeros((dst - end, o_ref.shape[1]), o_ref.dtype)
        o_ref[dst:dst + height, :] = x_ref[src:src + height, :].astype(o_ref.dtype)
        end = dst + height
    if end < o_ref.shape[0]:
        o_ref[end:, :] = jnp.zeros((o_ref.shape[0] - end, o_ref.shape[1]), o_ref.dtype)


def _cast_kernel(x_ref, o_ref):
    o_ref[...] = x_ref[...].astype(o_ref.dtype)


def _cast_layer(w, l, dtype):
    _, k, n = w.shape
    tr = _largest_tile(k, V7X_BF16_SUBLANES, 256)
    out = pl.pallas_call(
        _cast_kernel,
        grid=(k // tr,),
        in_specs=[pl.BlockSpec((None, tr, n), lambda i: (l, i, 0))],
        out_specs=pl.BlockSpec((tr, n), lambda i: (i, 0)),
        out_shape=jax.ShapeDtypeStruct((k, n), dtype),
        compiler_params=_cparams(("parallel",), 4 * tr * n * 4),
        name="cast_layer",
    )(w)
    return out.reshape(1, k, n)


def _packed_layout(sizes):
    units, src = [], 0
    for idx, width in enumerate(sizes):
        if width >= V7X_LANES or not units or units[-1][2] >= V7X_LANES:
            units.append([[idx], src, width])
        else:
            units[-1][0].append(idx)
            units[-1][2] += width
        src += width
    srcs = np.concatenate([[0], np.cumsum(sizes)[:-1]])
    offsets, runs, dst = [0] * len(sizes), [], 0
    for members, usrc, uwidth in sorted(units, key=lambda u: -u[2]):
        padded = -(-uwidth // V7X_LANES) * V7X_LANES
        dst = -(-dst // padded) * padded
        for idx in members:
            offsets[idx] = dst + int(srcs[idx]) - usrc
        runs.append((usrc, dst, uwidth))
        dst += padded
    return offsets, tuple(runs), -(-dst // PROJ_N_ALIGN) * PROJ_N_ALIGN


def _pack_w_in(w_t, l, sizes):
    offsets, runs, n_out = _packed_layout(sizes)
    assert all(v % V7X_BF16_SUBLANES == 0 for run in runs for v in run), "segments must be bf16-tile aligned"
    _, n_in, k = w_t.shape
    tc = _largest_tile(k, V7X_LANES, 256)
    packed = pl.pallas_call(
        functools.partial(_pack_kernel, runs=runs),
        grid=(k // tc,),
        in_specs=[pl.BlockSpec((None, n_in, tc), lambda i: (l, 0, i))],
        out_specs=pl.BlockSpec((n_out, tc), lambda i: (0, i)),
        out_shape=jax.ShapeDtypeStruct((n_out, k), BF16),
        compiler_params=_cparams(("parallel",), 3 * n_in * tc * 4 + 2 * n_out * tc * 2),
        name="pack_w_in",
    )(w_t)
    return packed.reshape(1, n_out, k), offsets


def kernel(x, c, w_ada, b_ada, norm_attn_g, w_in, rel_bias, head_norm_g, w_out, norm_ffn_g,
           w_gate_up, w_down, final_norm_g):
    b, s, d = x.shape
    assert b == 1, "single-sequence prefill only"
    depth = w_ada.shape[0]
    n_heads = d // HEAD_DIM
    ha = 3 * n_heads // 8
    hb = 3 * n_heads // 8
    hc = n_heads - ha - hb
    assert ha % N_KV_A == 0 and s % SB_T == 0
    sizes = (ha * HEAD_DIM, N_KV_A * HEAD_DIM, N_KV_A * HEAD_DIM, IDX_HEADS * IDX_DIM, IDX_DIM,
             IDX_HEADS, hb * HEAD_DIM, hb * HEAD_DIM, hb * HEAD_DIM,
             hc * HEAD_DIM, hc * HEAD_DIM, hc * HEAD_DIM)
    assert w_in.shape[2] == sum(sizes)
    topk = min(IDX_TOPK_MAX, s // 4)
    scale = HEAD_DIM ** -0.5
    pad = N_LEFT_CHUNKS * CHUNK

    rope128 = _rope_tables(s, HEAD_DIM, 1)
    rope64 = _rope_tables(s, IDX_DIM, V7X_LANES // IDX_DIM)

    xs = x.reshape(s, d)
    mod = _adaln(c, w_ada, b_ada)
    w_in_t = jnp.swapaxes(w_in, 1, 2)

    for l in range(depth):
        sh1, sc1, g1, sh2, sc2, g2 = [mod[l, :, n * d:(n + 1) * d] for n in range(6)]

        h = _norm(xs, norm_attn_g[l], sc1, sh1)
        w_packed, offs = _pack_w_in(w_in_t, l, sizes)
        o_qa, o_ka, o_va, o_iq, o_ik, o_iw, o_qb, o_kb, o_vb, o_qc, o_kc, o_vc = offs
        assert o_iw == o_ik + IDX_DIM and o_ik % V7X_LANES == 0
        proj = _matmul_nt(h, w_packed, 0, F32)

        qa = _prep_heads(proj, o_qa, ha, rope128, HEAD_DIM, scale * math.log2(math.e))
        ka = _prep_heads(proj, o_ka, N_KV_A, rope128, HEAD_DIM)
        va = _prep_heads(proj, o_va, N_KV_A)
        vat = va.reshape(N_KV_A, s // DSA_TK, DSA_TK, HEAD_DIM).swapaxes(2, 3)
        iq = _prep_heads(proj, o_iq, IDX_HEADS * IDX_DIM // V7X_LANES, rope64, IDX_DIM,
                         head_major=False)
        ik2, iw = _prep_ikw(proj, o_ik, rope64)
        gain = head_norm_g[l]
        gain_t = jnp.broadcast_to(gain.reshape(n_heads, HEAD_DIM, 1), (n_heads, HEAD_DIM, V7X_LANES))
        o_a = _dsa(iq, iw[:, :IDX_HEADS].T, ik2, qa, ka, vat, gain_t[:ha], topk)

        qb = _prep_heads(proj, o_qb, hb, scale=scale)
        kb = _prep_heads(proj, o_kb, hb)
        vb = _prep_heads(proj, o_vb, hb)
        nb, sub = s // SB_T, V7X_SUBLANES
        kb = kb.reshape(hb, nb, sub, SB_T // sub, HEAD_DIM).swapaxes(2, 3).reshape(hb, nb, SB_T, HEAD_DIM)
        vbt = vb.reshape(hb, nb, sub, SB_T // sub, HEAD_DIM).swapaxes(2, 3).reshape(hb, nb, SB_T, HEAD_DIM)
        vbt = vbt.swapaxes(2, 3)
        o_b = _sb(qb, kb, vbt, gain_t[ha:ha + hb])

        qc = _prep_heads(proj, o_qc, hc, scale=scale)
        kc = _prep_heads(proj, o_kc, hc)
        vc = _prep_heads(proj, o_vc, hc)
        kc = jnp.pad(kc, ((0, 0), (pad, 0), (0, 0)))
        vc = jnp.pad(vc, ((0, 0), (pad, 0), (0, 0)))
        o_c = _cb(qc, kc, vc, _cb_bias(rel_bias[l]), gain[(ha + hb) * HEAD_DIM:], pad)

        xs = _out_proj(o_a, o_b, o_c, _cast_layer(w_out, l, BF16), 0, xs, g1)

        h = _norm(xs, norm_ffn_g[l], sc2, sh2)
        act = _matmul_swiglu(h, w_gate_up, l)
        xs = _matmul_resid(act, _cast_layer(w_down, l, BF16), 0, xs, g2)

    out = _norm(xs, final_norm_g, out_dtype=x.dtype)
    return out.reshape(b, s, d)
```
